```python
import math
import jax
import jax.numpy as jnp
from jax import lax
import numpy as np

D_MODEL = 1024
BATCH = 16
SEQ = 2048
DEPTH = 4

GRID_W = 64
CTX_LEN = 256
N_MIXERS = 2
N_ATTN_LAYERS = (DEPTH + N_MIXERS - 1) // N_MIXERS
N_HYENA_LAYERS = DEPTH // N_MIXERS
EPS = 1e-6

HEAD_DIM = 64
N_Q_HEADS = D_MODEL // HEAD_DIM
N_KV_HEADS = 4
Q_PER_KV = N_Q_HEADS // N_KV_HEADS
Q_BLOCK = 128
ROPE_THETA = 10000.0
ROPE_PAIRS_PER_AXIS = HEAD_DIM // 4

HY_WIDTH = D_MODEL
HY_SHORT = 3
HY_EMB = 33
HY_BANDS = (HY_EMB - 1) // 2
HY_FFN = 64
HY_FAST_DECAY = 0.3
HY_SLOW_DECAY = 1.5
HY_TARGET = 1e-2

N_EXPERTS = 32
TOP_K = 4
D_EXPERT = D_MODEL
SWIGLU_LIMIT = 7.0
SWIGLU_ALPHA = 1.702
MOE_BLOCK = 128

kernel_name = 'hybrid_gqa_hyena_moe_dit'


def rms_norm(x, g):
    xf = x.astype(jnp.float32)
    y = xf * lax.rsqrt(jnp.mean(xf * xf, axis=-1, keepdims=True) + EPS)
    return (y * g.astype(jnp.float32)).astype(x.dtype)


def modulate(h, shift, scale):
    return h * (1.0 + scale[:, None, :]) + shift[:, None, :]


def axial_rope_angles(n_tokens):
    n_rows = n_tokens // GRID_W
    row = jnp.repeat(jnp.arange(n_rows, dtype=jnp.float32), GRID_W)
    col = jnp.tile(jnp.arange(GRID_W, dtype=jnp.float32), n_rows)
    inv_freq = ROPE_THETA ** (-jnp.arange(ROPE_PAIRS_PER_AXIS, dtype=jnp.float32) / ROPE_PAIRS_PER_AXIS)
    return row[:, None] * inv_freq, col[:, None] * inv_freq


def rope_1d(x, ang):
    cos = jnp.cos(ang)[:, None, :]
    sin = jnp.sin(ang)[:, None, :]
    x1, x2 = jnp.split(x.astype(jnp.float32), 2, axis=-1)
    return jnp.concatenate([x1 * cos - x2 * sin, x2 * cos + x1 * sin], axis=-1)


def rope_2d(x, ang_row, ang_col):
    x_row, x_col = jnp.split(x, 2, axis=-1)
    return jnp.concatenate([rope_1d(x_row, ang_row), rope_1d(x_col, ang_col)], axis=-1).astype(x.dtype)


def split_qkv(p, q_gain, k_gain, with_q):
    B, T, _ = p.shape
    q_cols = N_Q_HEADS * HEAD_DIM if with_q else 0
    kv_cols = N_KV_HEADS * HEAD_DIM
    k = rms_norm(p[..., q_cols:q_cols + kv_cols].reshape(B, T, N_KV_HEADS, HEAD_DIM), k_gain)
    v = p[..., q_cols + kv_cols:].reshape(B, T, N_KV_HEADS, HEAD_DIM)
    q = rms_norm(p[..., :q_cols].reshape(B, T, N_Q_HEADS, HEAD_DIM), q_gain) if with_q else None
    return q, k, v


def gqa_attend(q, k, v):
    B, Tq = q.shape[:2]
    qg = q.reshape(B, Tq, N_KV_HEADS, Q_PER_KV, HEAD_DIM)
    s = jnp.einsum('bqkgd,bskd->bkgqs', qg, k).astype(jnp.float32) * (HEAD_DIM ** -0.5)
    p = jax.nn.softmax(s, axis=-1).astype(v.dtype)
    o = jnp.einsum('bkgqs,bskd->bqkgd', p, v)
    return o.reshape(B, Tq, N_Q_HEADS * HEAD_DIM)


def attention_mixer(h, h_c, w_qkv, q_gain, k_gain, w_o, ang_row, ang_col, ctx_queries):
    B, L, _ = h.shape
    q, k, v = split_qkv(h @ w_qkv, q_gain, k_gain, True)
    q = rope_2d(q, ang_row, ang_col)
    k = rope_2d(k, ang_row, ang_col)
    w_ctx = w_qkv if ctx_queries else w_qkv[:, N_Q_HEADS * HEAD_DIM:]
    q_c, k_c, v_c = split_qkv(h_c @ w_ctx, q_gain, k_gain, ctx_queries)
    k_all = jnp.concatenate([k, k_c], axis=1)
    v_all = jnp.concatenate([v, v_c], axis=1)
    n_blocks = L // Q_BLOCK
    q_blocks = q.reshape(B, n_blocks, Q_BLOCK, N_Q_HEADS, HEAD_DIM).swapaxes(0, 1)
    o = lax.map(lambda qb: gqa_attend(qb, k_all, v_all), q_blocks)
    o = o.swapaxes(0, 1).reshape(B, L, N_Q_HEADS * HEAD_DIM) @ w_o
    o_c = gqa_attend(q_c, k_c, v_c) @ w_o if ctx_queries else None
    return o, o_c


def hyena_filter(n_tokens, w1, b1, w2, b2, w3, b3, sin_freq):
    L = n_tokens
    t = jnp.linspace(0.0, 1.0, L, dtype=jnp.float32)[:, None]
    w = 2.0 * math.pi * jnp.arange(L, dtype=jnp.float32)[:, None] / L
    f = jnp.linspace(1e-4, HY_BANDS - 1, HY_BANDS, dtype=jnp.float32)
    z = jnp.concatenate([t, jnp.cos(f * w), -jnp.sin(f * w)], axis=-1)
    a = jnp.sin(sin_freq * (z @ w1 + b1))
    a = jnp.sin(sin_freq * (a @ w2 + b2))
    filt = (a @ w3 + b3).astype(jnp.float32).reshape(L, 2, HY_WIDTH)
    max_decay = math.log(HY_TARGET) / HY_FAST_DECAY
    min_decay = math.log(HY_TARGET) / HY_SLOW_DECAY
    deltas = jnp.linspace(min_decay, max_decay, HY_WIDTH, dtype=jnp.float32)
    filt = filt * jnp.exp(-t[:, :, None] * jnp.abs(deltas))
    h_fwd, h_bwd = filt[:, 0], filt[:, 1]
    kern = jnp.concatenate([h_fwd, jnp.zeros((1, HY_WIDTH), jnp.float32), h_bwd[:0:-1]], axis=0)
    return kern / (jnp.sum(jnp.abs(kern), axis=0, keepdims=True) + EPS)


def short_conv(u, w, b):
    L = u.shape[1]
    pad = HY_SHORT // 2
    up = jnp.pad(u, ((0, 0), (pad, pad), (0, 0)))
    return sum(up[:, j:j + L] * w[j] for j in range(HY_SHORT)) + b


def long_conv(v, kern, d_bias):
    L = v.shape[1]
    vf = jnp.fft.rfft(v.astype(jnp.float32), n=2 * L, axis=1)
    kf = jnp.fft.rfft(kern, n=2 * L, axis=0)
    y = jnp.fft.irfft(vf * kf[None], n=2 * L, axis=1)[:, :L]
    return (y + v.astype(jnp.float32) * d_bias).astype(v.dtype)


def hyena_mixer(h, w_in, b_in, conv_w, conv_b, w1, b1, w2, b2, w3, b3, sin_freq, d_bias, w_out, b_out):
    u = short_conv(h @ w_in + b_in, conv_w, conv_b)
    x0, x1, v = jnp.split(u, 3, axis=-1)
    kern = hyena_filter(h.shape[1], w1, b1, w2, b2, w3, b3, sin_freq)
    y = x0 * long_conv(v * x1, kern, d_bias)
    return y @ w_out + b_out


def moe_ffn(h, router_w, router_b, w_gu, b_gu, w_down, b_down):
    n, d = h.shape
    logits = (h @ router_w + router_b).astype(jnp.float32)
    top_logit, top_e = lax.top_k(logits, TOP_K)
    gates = jax.nn.softmax(top_logit, axis=-1)
    n_pairs = n * TOP_K
    flat_e = top_e.reshape(-1)
    order = jnp.argsort(flat_e)
    sorted_e = flat_e[order]
    counts = jnp.bincount(flat_e, length=N_EXPERTS)
    padded = (counts + MOE_BLOCK - 1) // MOE_BLOCK * MOE_BLOCK
    pad_end = jnp.cumsum(padded)
    pad_start = pad_end - padded
    start = jnp.cumsum(counts) - counts
    dest = pad_start[sorted_e] + jnp.arange(n_pairs) - start[sorted_e]
    n_blocks = (n_pairs + N_EXPERTS * (MOE_BLOCK - 1) + MOE_BLOCK - 1) // MOE_BLOCK
    xs = jnp.zeros((n_blocks * MOE_BLOCK, d), h.dtype).at[dest].set(h[order // TOP_K])
    blk_e = jnp.minimum(jnp.searchsorted(pad_end, jnp.arange(n_blocks) * MOE_BLOCK, side='right'), N_EXPERTS - 1)

    def expert_block(args):
        xb, e = args
        gu = xb @ w_gu[e] + b_gu[e]
        g = jnp.minimum(gu[:, 0::2], SWIGLU_LIMIT)
        u = jnp.clip(gu[:, 1::2], -SWIGLU_LIMIT, SWIGLU_LIMIT)
        a = (u + 1.0) * g * jax.nn.sigmoid(SWIGLU_ALPHA * g)
        return a @ w_down[e] + b_down[e]

    ys = lax.map(expert_block, (xs.reshape(n_blocks, MOE_BLOCK, d), blk_e)).reshape(-1, d)
    pair_dest = jnp.zeros((n_pairs,), dest.dtype).at[order].set(dest)
    y_pairs = ys[pair_dest].reshape(n, TOP_K, d)
    return jnp.einsum('nk,nkd->nd', gates.astype(y_pairs.dtype), y_pairs)


def setup_inputs(seed: int = 0) -> dict:
    key = jax.random.key(seed)
    ks = iter(jax.random.split(key, 32))

    def nrm(shape, scale):
        return jax.random.normal(next(ks), shape, jnp.float32) * scale

    D, W, E, F = D_MODEL, HY_WIDTH, N_EXPERTS, D_EXPERT
    NA, NH = N_ATTN_LAYERS, N_HYENA_LAYERS
    qkv_cols = (N_Q_HEADS + 2 * N_KV_HEADS) * HEAD_DIM
    return {
        'x': nrm((BATCH, SEQ, D), 1.0),
        'c': nrm((BATCH, D), 1.0),
        'ctx': nrm((BATCH, CTX_LEN, D), 1.0),
        'c_ctx': nrm((D,), 1.0),
        'w_mod': nrm((DEPTH, D, 6 * D), 0.5 * D ** -0.5),
        'b_mod': nrm((DEPTH, 6 * D), 0.01),
        'norm1_g': 1.0 + nrm((DEPTH, D), 0.02),
        'norm2_g': 1.0 + nrm((DEPTH, D), 0.02),
        'attn_w_qkv': nrm((NA, D, qkv_cols), D ** -0.5),
        'attn_q_gain': 1.0 + nrm((NA, HEAD_DIM), 0.02),
        'attn_k_gain': 1.0 + nrm((NA, HEAD_DIM), 0.02),
        'attn_w_o': nrm((NA, N_Q_HEADS * HEAD_DIM, D), (N_Q_HEADS * HEAD_DIM) ** -0.5),
        'hy_w_in': nrm((NH, D, 3 * W), D ** -0.5),
        'hy_b_in': nrm((NH, 3 * W), 0.01),
        'hy_conv_w': nrm((NH, HY_SHORT, 3 * W), HY_SHORT ** -0.5),
        'hy_conv_b': nrm((NH, 3 * W), 0.01),
        'hy_filt_w1': nrm((NH, HY_EMB, HY_FFN), HY_EMB ** -0.5),
        'hy_filt_b1': nrm((NH, HY_FFN), 0.1),
        'hy_filt_w2': nrm((NH, HY_FFN, HY_FFN), HY_FFN ** -0.5),
        'hy_filt_b2': nrm((NH, HY_FFN), 0.1),
        'hy_filt_w3': nrm((NH, HY_FFN, 2 * W), HY_FFN ** -0.5),
        'hy_filt_b3': nrm((NH, 2 * W), 0.1),
        'hy_sin_freq': 1.0 + nrm((NH, HY_FFN), 0.02),
        'hy_d_bias': nrm((NH, W), 0.5),
        'hy_w_out': nrm((NH, W, D), W ** -0.5),
        'hy_b_out': nrm((NH, D), 0.01),
        'moe_router_w': nrm((DEPTH, D, E), D ** -0.5),
        'moe_router_b': nrm((DEPTH, E), 0.01),
        'moe_w_gu': nrm((DEPTH, E, D, 2 * F), D ** -0.5),
        'moe_b_gu': nrm((DEPTH, E, 2 * F), 0.01),
        'moe_w_down': nrm((DEPTH, E, F, D), F ** -0.5),
        'moe_b_down': nrm((DEPTH, E, D), 0.01),
    }


def reference(x, c, ctx, c_ctx, w_mod, b_mod, norm1_g, norm2_g, attn_w_qkv, attn_q_gain, attn_k_gain, attn_w_o, hy_w_in, hy_b_in, hy_conv_w, hy_conv_b, hy_filt_w1, hy_filt_b1, hy_filt_w2, hy_filt_b2, hy_filt_w3, hy_filt_b3, hy_sin_freq, hy_d_bias, hy_w_out, hy_b_out, moe_router_w, moe_router_b, moe_w_gu, moe_b_gu, moe_w_down, moe_b_down):
    B, L, D = x.shape
    ang_row, ang_col = axial_rope_angles(L)
    n_lat = B * L
    for i in range(DEPTH):
        j = i // N_MIXERS
        is_attn = i % N_MIXERS == 0
        ctx_live = any(m % N_MIXERS == 0 for m in range(i + 1, DEPTH))
        mod = jnp.split(jax.nn.silu(c) @ w_mod[i] + b_mod[i], 6, axis=-1)
        mod_c = jnp.split(jax.nn.silu(c_ctx)[None] @ w_mod[i] + b_mod[i], 6, axis=-1)
        h = modulate(rms_norm(x, norm1_g[i]), mod[0], mod[1])
        if is_attn or ctx_live:
            h_c = modulate(rms_norm(ctx, norm1_g[i]), mod_c[0], mod_c[1])
        if is_attn:
            o, o_c = attention_mixer(h, h_c, attn_w_qkv[j], attn_q_gain[j], attn_k_gain[j], attn_w_o[j], ang_row, ang_col, ctx_live)
        else:
            hy = (hy_w_in[j], hy_b_in[j], hy_conv_w[j], hy_conv_b[j], hy_filt_w1[j], hy_filt_b1[j], hy_filt_w2[j], hy_filt_b2[j], hy_filt_w3[j], hy_filt_b3[j], hy_sin_freq[j], hy_d_bias[j], hy_w_out[j], hy_b_out[j])
            o = hyena_mixer(h, *hy)
            o_c = hyena_mixer(h_c, *hy) if ctx_live else None
        x = x + mod[2][:, None, :] * o
        h = modulate(rms_norm(x, norm2_g[i]), mod[3], mod[4])
        moe = (moe_router_w[i], moe_router_b[i], moe_w_gu[i], moe_b_gu[i], moe_w_down[i], moe_b_down[i])
        if ctx_live:
            ctx = ctx + mod_c[2][:, None, :] * o_c
            h_c = modulate(rms_norm(ctx, norm2_g[i]), mod_c[3], mod_c[4])
            y = moe_ffn(jnp.concatenate([h.reshape(n_lat, D), h_c.reshape(-1, D)], axis=0), *moe)
            x = x + mod[5][:, None, :] * y[:n_lat].reshape(B, L, D)
            ctx = ctx + mod_c[5][:, None, :] * y[n_lat:].reshape(ctx.shape)
        else:
            y = moe_ffn(h.reshape(n_lat, D), *moe)
            x = x + mod[5][:, None, :] * y.reshape(B, L, D)
    return x
```

```python
import functools
import math

import jax
import jax.numpy as jnp
from jax import lax
from jax.experimental import pallas as pl
from jax.experimental.pallas import tpu as pltpu

F32 = jnp.float32
BF16 = jnp.bfloat16
I32 = jnp.int32

LANES = 128
SUBLANES = 8
VMEM_BYTES = 64 * 2**20
VMEM_CAP = VMEM_BYTES - 8 * 2**20

GRID_W = 64
EPS = 1e-6
HEAD_DIM = 64
N_KV_HEADS = 4
ROPE_THETA = 10000.0
ROPE_PAIRS = HEAD_DIM // 4
HY_SHORT = 3
HY_EMB = 33
HY_BANDS = (HY_EMB - 1) // 2
HY_FAST_DECAY = 0.3
HY_SLOW_DECAY = 1.5
HY_TARGET = 1e-2
N_EXPERTS = 32
TOP_K = 4
SWIGLU_LIMIT = 7.0
SWIGLU_ALPHA = 1.702
N_MIXERS = 2
NEG_BIG = -1e30

ROW_TILE = 512
ATT_Q_TILE = 128
EXPERT_TILE = 512
TOKEN_TILE = 256
CONV_COLS = 256
CONV_CHUNK = 512


def _cparams(n_axes, vmem_estimate):
    limit = int(min(max(vmem_estimate * 5 // 4, 32 * 2**20), VMEM_CAP))
    return pltpu.CompilerParams(dimension_semantics=("arbitrary",) * n_axes, vmem_limit_bytes=limit)


def _split(a):
    hi = a.astype(BF16)
    lo = (a - hi.astype(F32)).astype(BF16)
    return hi, lo


def _dot(a, b):
    return jnp.dot(a, b, preferred_element_type=F32)


def _dot3(a, b):
    a_hi, a_lo = _split(a)
    b_hi, b_lo = _split(b)
    return _dot(a_hi, b_hi) + _dot(a_lo, b_hi) + _dot(a_hi, b_lo)


def _norm_mod(x, g, shift, scale):
    y = x * lax.rsqrt(jnp.mean(x * x, axis=-1, keepdims=True) + EPS)
    return (y * g) * (1.0 + scale) + shift


def _pack(a, b):
    ua = lax.bitcast_convert_type(a.astype(BF16).astype(F32), I32)
    ub = lax.bitcast_convert_type(b.astype(BF16).astype(F32), I32)
    return (ua & jnp.int32(-65536)) | lax.shift_right_logical(ub, jnp.int32(16))


def _unpack(p):
    a = lax.bitcast_convert_type(p & jnp.int32(-65536), F32)
    b = lax.bitcast_convert_type(lax.shift_left(p, jnp.int32(16)), F32)
    return a, b


def _mod_kernel(c_ref, w_ref, b_ref, o_ref):
    c = c_ref[...]
    a = c * jax.nn.sigmoid(c)
    o_ref[0] = _dot3(a, w_ref[0]) + b_ref[0]


def _modulation(cc, w_mod, b_mod):
    depth, d, n = w_mod.shape
    r = cc.shape[0]
    tn = 1536
    return pl.pallas_call(
        _mod_kernel,
        grid=(depth, n // tn),
        in_specs=[
            pl.BlockSpec((r, d), lambda i, j: (0, 0)),
            pl.BlockSpec((1, d, tn), lambda i, j: (i, 0, j)),
            pl.BlockSpec((1, 1, tn), lambda i, j: (i, 0, j)),
        ],
        out_specs=pl.BlockSpec((1, r, tn), lambda i, j: (i, 0, j)),
        out_shape=jax.ShapeDtypeStruct((depth, r, n), F32),
        compiler_params=_cparams(2, 2 * d * tn * 4 + 3 * d * tn * 2),
        name="modulation",
    )(cc, w_mod, b_mod.reshape(depth, 1, n))


def _qkv_kernel(*refs, n_qk, rope, with_q):
    x_ref, sh_ref, sc_ref, g_ref, w_ref, gain_ref, e_ref, et_ref = refs[:8]
    pos = 8
    if rope:
        cos_ref, sa_ref, sb_ref = refs[pos:pos + 3]
        pos += 3
    outs = refs[pos:]
    h = _norm_mod(x_ref[0], g_ref[...], sh_ref[0], sc_ref[0]).astype(BF16)
    acc = _dot(h, w_ref[...])
    qk = acc[:, :n_qk]
    ssq = _dot((qk * qk).astype(BF16), e_ref[...])
    r_hi, r_lo = _split(lax.rsqrt(ssq * (1.0 / HEAD_DIM) + EPS))
    y = qk * (_dot(r_hi, et_ref[...]) + _dot(r_lo, et_ref[...])) * gain_ref[...]
    if rope:
        y = (y * cos_ref[...] + pltpu.roll(y, n_qk - ROPE_PAIRS, 1) * sa_ref[...]
             + pltpu.roll(y, ROPE_PAIRS, 1) * sb_ref[...])
    n_q = n_qk - N_KV_HEADS * HEAD_DIM
    if with_q:
        outs[0][0] = (y[:, :n_q] * (HEAD_DIM ** -0.5)).astype(BF16)
        outs = outs[1:]
    outs[0][0] = y[:, n_q:].astype(BF16)
    outs[1][0] = acc[:, n_qk:].astype(BF16)


def _qkv_project(x, shift, scale, g, w, q_gain, k_gain, rope_tabs, with_q):
    b, t, d = x.shape
    kv = N_KV_HEADS * HEAD_DIM
    nc = w.shape[1]
    n_qk = nc - kv
    n_heads = n_qk // HEAD_DIM
    tm = min(ROW_TILE, t)
    gains = ([q_gain] * (n_heads - N_KV_HEADS)) + [k_gain] * N_KV_HEADS
    gain = jnp.concatenate(gains).reshape(1, n_qk).astype(F32)
    head_of_col = jnp.arange(n_qk) // HEAD_DIM
    e = (head_of_col[:, None] == jnp.arange(LANES)[None, :]).astype(BF16)
    et = e.T
    bm = shift.shape[0]
    mod_map = (lambda i, j: (j, 0, 0)) if bm > 1 else (lambda i, j: (0, 0, 0))
    in_specs = [
        pl.BlockSpec((1, tm, d), lambda i, j: (j, i, 0)),
        pl.BlockSpec((1, 1, d), mod_map),
        pl.BlockSpec((1, 1, d), mod_map),
        pl.BlockSpec((1, d), lambda i, j: (0, 0)),
        pl.BlockSpec((d, nc), lambda i, j: (0, 0)),
        pl.BlockSpec((1, n_qk), lambda i, j: (0, 0)),
        pl.BlockSpec((n_qk, LANES), lambda i, j: (0, 0)),
        pl.BlockSpec((LANES, n_qk), lambda i, j: (0, 0)),
    ]
    args = [x, shift, scale, g.reshape(1, d), w, gain, e, et]
    rope = rope_tabs is not None
    if rope:
        in_specs += [pl.BlockSpec((tm, n_qk), lambda i, j: (i, 0))] * 3
        args += list(rope_tabs)
    out_shape, out_specs = [], []
    widths = ([n_qk - kv] if with_q else []) + [kv, kv]
    for wd in widths:
        out_shape.append(jax.ShapeDtypeStruct((b, t, wd), BF16))
        out_specs.append(pl.BlockSpec((1, tm, wd), lambda i, j: (j, i, 0)))
    est = 2 * tm * d * 4 + 2 * d * nc * 2 + 6 * tm * n_qk * 4 + 8 * tm * nc * 4
    return pl.pallas_call(
        functools.partial(_qkv_kernel, n_qk=n_qk, rope=rope, with_q=with_q),
        grid=(t // tm, b),
        in_specs=in_specs,
        out_specs=out_specs,
        out_shape=out_shape,
        compiler_params=_cparams(2, est),
        name="qkv_project",
    )(*args)


def _rope_tables(n_tokens, n_heads):
    n_rows = n_tokens // GRID_W
    row = jnp.repeat(jnp.arange(n_rows, dtype=F32), GRID_W)
    col = jnp.tile(jnp.arange(GRID_W, dtype=F32), n_rows)
    inv_freq = ROPE_THETA ** (-jnp.arange(ROPE_PAIRS, dtype=F32) / ROPE_PAIRS)
    ar, ac = row[:, None] * inv_freq, col[:, None] * inv_freq
    zero = jnp.zeros_like(ar)
    cos = jnp.concatenate([jnp.cos(ar), jnp.cos(ar), jnp.cos(ac), jnp.cos(ac)], axis=1)
    sin_a = jnp.concatenate([-jnp.sin(ar), zero, -jnp.sin(ac), zero], axis=1)
    sin_b = jnp.concatenate([zero, jnp.sin(ar), zero, jnp.sin(ac)], axis=1)
    return tuple(jnp.tile(tab, (1, n_heads)) for tab in (cos, sin_a, sin_b))


def _attn_kernel(*refs, n_seg, tq):
    q_ref = refs[0]
    kv_refs = refs[1:1 + 2 * n_seg]
    o_ref = refs[1 + 2 * n_seg]
    q_per_kv = q_ref.shape[2] // (N_KV_HEADS * HEAD_DIM)
    for g in range(N_KV_HEADS):
        heads = [g * q_per_kv + j for j in range(q_per_kv)]
        qs = jnp.concatenate([q_ref[0, :, h * HEAD_DIM:(h + 1) * HEAD_DIM] for h in heads], axis=0)
        ksl = slice(g * HEAD_DIM, (g + 1) * HEAD_DIM)
        scores = [lax.dot_general(qs, kv_refs[2 * s][0, :, ksl], (((1,), (1,)), ((), ())),
                                  preferred_element_type=F32) for s in range(n_seg)]
        m = functools.reduce(jnp.maximum, [jnp.max(sc, axis=-1, keepdims=True) for sc in scores])
        probs = [jnp.exp(sc - m) for sc in scores]
        denom = functools.reduce(jnp.add, [jnp.sum(p, axis=-1, keepdims=True) for p in probs])
        o = functools.reduce(jnp.add, [_dot(probs[s].astype(BF16), kv_refs[2 * s + 1][0, :, ksl])
                                       for s in range(n_seg)])
        o = o / denom
        for j, h in enumerate(heads):
            o_ref[0, :, h * HEAD_DIM:(h + 1) * HEAD_DIM] = o[j * tq:(j + 1) * tq].astype(BF16)


def _attention(q, segments):
    b, t, dq = q.shape
    tq = min(ATT_Q_TILE, t)
    in_specs = [pl.BlockSpec((1, tq, dq), lambda i, j: (i, j, 0))]
    args = [q]
    t_keys = 0
    for k, v in segments:
        ts, kv = k.shape[1], k.shape[2]
        in_specs += [pl.BlockSpec((1, ts, kv), lambda i, j: (i, 0, 0))] * 2
        args += [k, v]
        t_keys += ts
    q_rows = tq * dq // (N_KV_HEADS * HEAD_DIM)
    est = 4 * tq * dq * 2 + 8 * t_keys * N_KV_HEADS * HEAD_DIM * 2 + 4 * q_rows * t_keys * 4
    return pl.pallas_call(
        functools.partial(_attn_kernel, n_seg=len(segments), tq=tq),
        grid=(b, t // tq),
        in_specs=in_specs,
        out_specs=pl.BlockSpec((1, tq, dq), lambda i, j: (i, j, 0)),
        out_shape=jax.ShapeDtypeStruct((b, t, dq), BF16),
        compiler_params=_cparams(2, est),
        name="attention",
    )(*args)


def _post_kernel(a_ref, w_ref, bias_ref, x_ref, gate_ref, sh_ref, sc_ref, g_ref, rw_ref, rb_ref,
                 xo_ref, hp_ref, te_ref, tg_ref):
    d = x_ref.shape[2]
    o = _dot(a_ref[0], w_ref[...]) + bias_ref[...]
    x_new = x_ref[0] + gate_ref[0] * o
    xo_ref[0] = x_new
    h = _norm_mod(x_new, g_ref[...], sh_ref[0], sc_ref[0])
    hp_ref[0] = _pack(h[:, :d // 2], h[:, d // 2:])
    logits = _dot3(h, rw_ref[...]) + rb_ref[...]
    lane = lax.broadcasted_iota(I32, logits.shape, 1)
    e_out = jnp.zeros(logits.shape, I32)
    g_out = jnp.zeros(logits.shape, F32)
    top = None
    total = None
    exps = []
    for k in range(TOP_K):
        m = jnp.max(logits, axis=1, keepdims=True)
        idx = jnp.min(jnp.where(logits == m, lane, LANES), axis=1, keepdims=True)
        logits = jnp.where(lane == idx, -jnp.inf, logits)
        e_out = jnp.where(lane == k, idx, e_out)
        top = m if top is None else top
        ex = jnp.exp(m - top)
        exps.append(ex)
        total = ex if total is None else total + ex
    for k in range(TOP_K):
        g_out = jnp.where(lane == k, exps[k] / total, g_out)
    te_ref[0] = e_out
    tg_ref[0] = g_out


def _post_mixer(a, w, bias, x, gate, shift, scale, g, router_w, router_b):
    b, t, d = x.shape
    k_in = a.shape[2]
    tm = min(ROW_TILE, t)
    bm = gate.shape[0]
    mod_map = (lambda i, j: (i, 0, 0)) if bm > 1 else (lambda i, j: (0, 0, 0))
    rw = jnp.zeros((d, LANES), F32).at[:, :N_EXPERTS].set(router_w)
    rb = jnp.full((1, LANES), NEG_BIG, F32).at[0, :N_EXPERTS].set(router_b)
    row_spec = lambda wd: pl.BlockSpec((1, tm, wd), lambda i, j: (i, j, 0))
    const = lambda shape: pl.BlockSpec(shape, lambda i, j: (0,) * len(shape))
    est = 2 * tm * (k_in * 2 + d * 4 * 2 + d * 2 + 2 * LANES * 4) + 2 * k_in * d * 2 + 8 * tm * d * 4
    return pl.pallas_call(
        _post_kernel,
        grid=(b, t // tm),
        in_specs=[row_spec(k_in), const((k_in, d)), const((1, d)), row_spec(d),
                  pl.BlockSpec((1, 1, d), mod_map), pl.BlockSpec((1, 1, d), mod_map),
                  pl.BlockSpec((1, 1, d), mod_map), const((1, d)), const((d, LANES)), const((1, LANES))],
        out_specs=[row_spec(d), row_spec(d // 2), row_spec(LANES), row_spec(LANES)],
        out_shape=[jax.ShapeDtypeStruct((b, t, d), F32), jax.ShapeDtypeStruct((b, t, d // 2), I32),
                   jax.ShapeDtypeStruct((b, t, LANES), I32), jax.ShapeDtypeStruct((b, t, LANES), F32)],
        compiler_params=_cparams(2, est),
        name="post_mixer",
    )(a, w, bias.reshape(1, d), x, gate, shift, scale, g.reshape(1, d), rw, rb)


def _route(top_e, n_rows_max):
    flat = top_e.reshape(-1)
    onehot = (flat[:, None] == jnp.arange(N_EXPERTS, dtype=I32)[None, :]).astype(I32)
    csum = jnp.cumsum(onehot, axis=0)
    counts = csum[-1]
    rank = jnp.sum(onehot * csum, axis=1) - 1
    padded = (counts + EXPERT_TILE - 1) // EXPERT_TILE * EXPERT_TILE
    pad_end = jnp.cumsum(padded)
    pad_start = pad_end - padded
    dest = jnp.sum(onehot * pad_start[None, :], axis=1) + rank
    n_tiles_max = n_rows_max // EXPERT_TILE
    tile_e = jnp.minimum(jnp.searchsorted(pad_end, jnp.arange(n_tiles_max, dtype=I32) * EXPERT_TILE,
                                          side="right"), N_EXPERTS - 1).astype(I32)
    n_valid = (pad_end[-1] // EXPERT_TILE).astype(I32).reshape(1)
    return dest.astype(I32), tile_e, n_valid


def _row_copy(src_hbm, src_row, dst_hbm, dst_row, sem):
    return pltpu.make_async_copy(src_hbm.at[pl.ds(src_row, 1)], dst_hbm.at[pl.ds(dst_row, 1)], sem)


def _dispatch_kernel(dest_hbm, h_hbm, xs_in_hbm, xs_hbm, idx_smem, idx_sem, row_sem, *, tt):
    del xs_in_hbm
    i = pl.program_id(0)
    idx_copy = pltpu.make_async_copy(dest_hbm.at[i], idx_smem, idx_sem)
    idx_copy.start()
    idx_copy.wait()

    def start(r, carry):
        for k in range(TOP_K):
            _row_copy(h_hbm, i * tt + r, xs_hbm, idx_smem[TOP_K * r + k], row_sem).start()
        return carry

    lax.fori_loop(0, tt, start, 0)

    def wait(r, carry):
        for k in range(TOP_K):
            _row_copy(h_hbm, 0, xs_hbm, 0, row_sem).wait()
        return carry

    lax.fori_loop(0, tt, wait, 0)


def _dispatch(dest, h_rows, xs):
    n, w = h_rows.shape
    tt = min(TOKEN_TILE, n)
    return pl.pallas_call(
        functools.partial(_dispatch_kernel, tt=tt),
        grid=(n // tt,),
        in_specs=[pl.BlockSpec(memory_space=pl.ANY)] * 3,
        out_specs=pl.BlockSpec(memory_space=pl.ANY),
        out_shape=jax.ShapeDtypeStruct(xs.shape, xs.dtype),
        scratch_shapes=[pltpu.SMEM((tt * TOP_K,), I32), pltpu.SemaphoreType.DMA, pltpu.SemaphoreType.DMA],
        input_output_aliases={2: 0},
        compiler_params=pltpu.CompilerParams(dimension_semantics=("arbitrary",), has_side_effects=True),
        name="moe_dispatch",
    )(dest.reshape(n // tt, tt * TOP_K), h_rows, xs)


def _expert_kernel(te_ref, nv_ref, xs_ref, wg_ref, wu_ref, wd_ref, bg_ref, bu_ref, bd_ref, ys_ref):
    @pl.when(pl.program_id(0) < nv_ref[0])
    def _():
        lo, hi = _unpack(xs_ref[...])
        xb = jnp.concatenate([lo, hi], axis=1).astype(BF16)
        g = jnp.minimum(_dot(xb, wg_ref[0]) + bg_ref[0], SWIGLU_LIMIT)
        u = jnp.clip(_dot(xb, wu_ref[0]) + bu_ref[0], -SWIGLU_LIMIT, SWIGLU_LIMIT)
        act = (u + 1.0) * g * jax.nn.sigmoid(SWIGLU_ALPHA * g)
        y = _dot(act.astype(BF16), wd_ref[0]) + bd_ref[0]
        half = y.shape[1] // 2
        ys_ref[...] = _pack(y[:, :half], y[:, half:])

    @pl.when(pl.program_id(0) >= nv_ref[0])
    def _():
        ys_ref[...] = jnp.zeros(ys_ref.shape, ys_ref.dtype)


def _experts(xs, tile_e, n_valid, wg, wu, wd, bg, bu, bd):
    rows, wp = xs.shape
    d = 2 * wp
    f = wg.shape[2]
    tm = EXPERT_TILE
    tile_map = lambda i, te, nv: (jnp.minimum(i, nv[0] - 1), 0)
    w_map = lambda i, te, nv: (te[jnp.minimum(i, nv[0] - 1)], 0, 0)
    est = 4 * tm * wp * 4 + 2 * 3 * d * f * 2 + 8 * tm * f * 4
    return pl.pallas_call(
        _expert_kernel,
        grid_spec=pltpu.PrefetchScalarGridSpec(
            num_scalar_prefetch=2,
            grid=(rows // tm,),
            in_specs=[pl.BlockSpec((tm, wp), tile_map),
                      pl.BlockSpec((1, d, f), w_map), pl.BlockSpec((1, d, f), w_map),
                      pl.BlockSpec((1, f, d), w_map),
                      pl.BlockSpec((1, 1, f), w_map), pl.BlockSpec((1, 1, f), w_map),
                      pl.BlockSpec((1, 1, d), w_map)],
            out_specs=pl.BlockSpec((tm, wp), lambda i, te, nv: (i, 0)),
        ),
        out_shape=jax.ShapeDtypeStruct((rows, wp), I32),
        compiler_params=_cparams(1, est),
        name="moe_experts",
    )(tile_e, n_valid, xs, wg, wu, wd, bg, bu, bd)


def _combine_kernel(dest_hbm, ys_hbm, gates_ref, x_ref, gate_ref, o_ref, buf, idx_smem, idx_sem, row_sem,
                    *, tt):
    i = pl.program_id(0)
    idx_copy = pltpu.make_async_copy(dest_hbm.at[i], idx_smem, idx_sem)
    idx_copy.start()
    idx_copy.wait()

    def start(r, carry):
        for k in range(TOP_K):
            pltpu.make_async_copy(ys_hbm.at[pl.ds(idx_smem[TOP_K * r + k], 1)],
                                  buf.at[k, pl.ds(r, 1)], row_sem).start()
        return carry

    lax.fori_loop(0, tt, start, 0)

    def wait(r, carry):
        for k in range(TOP_K):
            pltpu.make_async_copy(ys_hbm.at[pl.ds(0, 1)], buf.at[k, pl.ds(0, 1)], row_sem).wait()
        return carry

    lax.fori_loop(0, tt, wait, 0)
    gates = gates_ref[...]
    y = None
    for k in range(TOP_K):
        lo, hi = _unpack(buf[k])
        yk = jnp.concatenate([lo, hi], axis=1) * gates[:, k:k + 1]
        y = yk if y is None else y + yk
    o_ref[...] = x_ref[...] + gate_ref[0] * y


def _combine(dest, ys, gates, x, gate):
    b, t, d = x.shape
    n = b * t
    tt = min(TOKEN_TILE, t)
    steps_per_batch = t // tt
    bm = gate.shape[0]
    gate_map = (lambda i: (i // steps_per_batch, 0, 0)) if bm > 1 else (lambda i: (0, 0, 0))
    est = TOP_K * tt * d * 2 + 4 * tt * d * 4 + 2 * tt * LANES * 4 + 6 * tt * d * 4
    out = pl.pallas_call(
        functools.partial(_combine_kernel, tt=tt),
        grid=(n // tt,),
        in_specs=[pl.BlockSpec(memory_space=pl.ANY), pl.BlockSpec(memory_space=pl.ANY),
                  pl.BlockSpec((tt, LANES), lambda i: (i, 0)),
                  pl.BlockSpec((tt, d), lambda i: (i, 0)),
                  pl.BlockSpec((1, 1, d), gate_map)],
        out_specs=pl.BlockSpec((tt, d), lambda i: (i, 0)),
        out_shape=jax.ShapeDtypeStruct((n, d), F32),
        scratch_shapes=[pltpu.VMEM((TOP_K, tt, d // 2), I32), pltpu.SMEM((tt * TOP_K,), I32),
                        pltpu.SemaphoreType.DMA, pltpu.SemaphoreType.DMA],
        compiler_params=_cparams(1, est),
        name="moe_combine",
    )(dest.reshape(n // tt, tt * TOP_K), ys, gates.reshape(n, LANES), x.reshape(n, d), gate)
    return out.reshape(b, t, d)


def _moe(streams, expert_w):
    wg, wu, wd, bg, bu, bd = expert_w
    n_pairs = sum(s[1].shape[0] * s[1].shape[1] for s in streams) * TOP_K
    n_tiles_max = (n_pairs + N_EXPERTS * (EXPERT_TILE - 1)) // EXPERT_TILE
    n_rows_max = n_tiles_max * EXPERT_TILE
    top_e = jnp.concatenate([s[2][..., :TOP_K].reshape(-1, TOP_K) for s in streams], axis=0)
    dest, tile_e, n_valid = _route(top_e, n_rows_max)
    wp = streams[0][1].shape[2]
    xs = jnp.zeros((n_rows_max, wp), I32)
    offs = 0
    dests = []
    for s in streams:
        n = s[1].shape[0] * s[1].shape[1]
        dests.append(dest[offs * TOP_K:(offs + n) * TOP_K])
        xs = _dispatch(dests[-1], s[1].reshape(n, wp), xs)
        offs += n
    ys = _experts(xs, tile_e, n_valid, wg, wu, wd, bg, bu, bd)
    return [_combine(dests[i], ys, s[3], s[0], s[4]) for i, s in enumerate(streams)]


def _hyena_in_kernel(xm_ref, xp_ref, xn_ref, sh_ref, sc_ref, g_ref, w_ref, b_ref, cw_ref, cb_ref,
                     x0_ref, z_ref, *, tm, width, chunk):
    i = pl.program_id(1)
    last = pl.num_programs(1) - 1
    xe = jnp.concatenate([xp_ref[0], xm_ref[0], xn_ref[0]], axis=0)
    h = _norm_mod(xe, g_ref[...], sh_ref[0], sc_ref[0]).astype(BF16)
    rows = lax.broadcasted_iota(I32, (tm + 2 * SUBLANES, 1), 0)
    inside = jnp.logical_and(jnp.logical_or(rows >= SUBLANES, i > 0),
                             jnp.logical_or(rows < tm + SUBLANES, i < last))
    n_ext = tm + 2 * SUBLANES

    def conv(col):
        u = _dot(h, w_ref[:, col:col + chunk]) + b_ref[:, col:col + chunk]
        u = jnp.where(inside, u, 0.0)
        prev = pltpu.roll(u, 1, 0)[SUBLANES:SUBLANES + tm]
        nxt = pltpu.roll(u, n_ext - 1, 0)[SUBLANES:SUBLANES + tm]
        cw = cw_ref[:, col:col + chunk]
        return (prev * cw[0:1] + u[SUBLANES:SUBLANES + tm] * cw[1:2] + nxt * cw[2:3]
                + cb_ref[:, col:col + chunk])

    for c in range(width // chunk):
        x0 = conv(c * chunk)
        x1 = conv(width + c * chunk)
        v = conv(2 * width + c * chunk)
        x0_ref[0, :, c * chunk:(c + 1) * chunk] = x0.astype(BF16)
        z_ref[0, :, c * chunk:(c + 1) * chunk] = (v * x1).astype(BF16)


def _hyena_in(x, shift, scale, g, w_in, b_in, conv_w, conv_b):
    b, t, d = x.shape
    n3 = w_in.shape[1]
    width = n3 // 3
    tm = min(ROW_TILE // 2, t)
    blocks8 = tm // SUBLANES
    last8 = t // SUBLANES - 1
    bm = shift.shape[0]
    mod_map = (lambda i, j: (i, 0, 0)) if bm > 1 else (lambda i, j: (0, 0, 0))
    const = lambda shape: pl.BlockSpec(shape, lambda i, j: (0,) * len(shape))
    chunk = 512
    est = 2 * (tm + 16) * d * 4 + 2 * d * n3 * 2 + 4 * tm * width * 2 + 10 * (tm + 16) * chunk * 4 + 4 * tm * d * 4
    return pl.pallas_call(
        functools.partial(_hyena_in_kernel, tm=tm, width=width, chunk=chunk),
        grid=(b, t // tm),
        in_specs=[pl.BlockSpec((1, tm, d), lambda i, j: (i, j, 0)),
                  pl.BlockSpec((1, SUBLANES, d), lambda i, j: (i, jnp.maximum(j * blocks8 - 1, 0), 0)),
                  pl.BlockSpec((1, SUBLANES, d), lambda i, j: (i, jnp.minimum((j + 1) * blocks8, last8), 0)),
                  pl.BlockSpec((1, 1, d), mod_map), pl.BlockSpec((1, 1, d), mod_map),
                  const((1, d)), const((d, n3)), const((1, n3)), const((HY_SHORT, n3)), const((1, n3))],
        out_specs=[pl.BlockSpec((1, tm, width), lambda i, j: (i, j, 0))] * 2,
        out_shape=[jax.ShapeDtypeStruct((b, t, width), BF16)] * 2,
        compiler_params=_cparams(2, est),
        name="hyena_in",
    )(x, x, x, shift, scale, g.reshape(1, d), w_in, b_in.reshape(1, n3), conv_w, conv_b.reshape(1, n3))


def _dft_matrix(length):
    n = 2 * length
    f = jnp.arange(length, dtype=I32)
    ang = ((f[:, None] * f[None, :]) % n).astype(F32) * (2.0 * math.pi / n)
    nyq = jnp.where(f % 2 == 0, 1.0, -1.0).astype(F32)
    msin = (-jnp.sin(ang)).at[0].set(nyq)
    return jnp.concatenate([jnp.cos(ang), msin], axis=0).astype(BF16)


def _filter_kernel(z_ref, w1_ref, b1_ref, fr_ref, w2_ref, b2_ref, w3f_ref, w3b_ref, b3f_ref, b3b_ref,
                   t_ref, dl_ref, wf_ref, a_ref, b_ref, c_ref, d_ref, *, length):
    freq = fr_ref[...]
    a1 = jnp.sin(freq * (_dot3(z_ref[...], w1_ref[...]) + b1_ref[...]))
    a2 = jnp.sin(freq * (_dot3(a1, w2_ref[...]) + b2_ref[...]))
    decay = jnp.exp(-t_ref[...] * dl_ref[...])
    rows = lax.broadcasted_iota(I32, (length, 1), 0)
    first = rows == 0
    hf = (_dot3(a2, w3f_ref[...]) + b3f_ref[...]) * decay
    hb = jnp.where(first, 0.0, (_dot3(a2, w3b_ref[...]) + b3b_ref[...]) * decay)
    inv = 1.0 / (jnp.sum(jnp.abs(hf), axis=0, keepdims=True) + jnp.sum(jnp.abs(hb), axis=0, keepdims=True) + EPS)
    hf = hf * inv
    hb = hb * inv
    p_hi, p_lo = _split(hf + hb)
    q_hi, q_lo = _split(hf - hb)
    wf = wf_ref[...]
    sp = _dot(wf, p_hi) + _dot(wf, p_lo)
    ki = _dot(wf[length:], q_hi) + _dot(wf[length:], q_lo)
    kr = sp[:length]
    knyq = sp[length:length + 1]
    n = 2.0 * length
    sc = jnp.where(first, 1.0 / n, 2.0 / n)
    a_ref[...] = kr * sc
    b_ref[...] = jnp.where(first, 0.0, -ki * sc)
    c_ref[...] = jnp.where(first, 0.0, ki * sc)
    d_ref[...] = jnp.where(first, knyq * (1.0 / n), kr * sc)


def _hyena_filter_spectrum(length, w1, b1, w2, b2, w3, b3, sin_freq, wf):
    width = w3.shape[1] // 2
    ffn = w1.shape[1]
    t = jnp.linspace(0.0, 1.0, length, dtype=F32)[:, None]
    w = 2.0 * math.pi * jnp.arange(length, dtype=F32)[:, None] / length
    f = jnp.linspace(1e-4, HY_BANDS - 1, HY_BANDS, dtype=F32)
    z = jnp.concatenate([t, jnp.cos(f * w), -jnp.sin(f * w)], axis=-1)
    zp = jnp.zeros((length, LANES), F32).at[:, :HY_EMB].set(z)
    pad2 = lambda m: jnp.zeros((LANES, m.shape[1] if m.shape[1] > LANES else LANES), F32).at[:m.shape[0], :m.shape[1]].set(m)
    padv = lambda v: jnp.zeros((1, LANES), F32).at[0, :v.shape[0]].set(v)
    max_decay = math.log(HY_TARGET) / HY_FAST_DECAY
    min_decay = math.log(HY_TARGET) / HY_SLOW_DECAY
    absdelta = jnp.abs(jnp.linspace(min_decay, max_decay, width, dtype=F32)).reshape(1, width)
    tc = LANES
    nct = width // tc
    const = lambda shape: pl.BlockSpec(shape, lambda j: (0,) * len(shape))
    w3p = pad2(w3)
    b3r = b3.reshape(1, 2 * width)
    plane = pl.BlockSpec((length, tc), lambda j: (0, j))
    est = 2 * length * length * 2 + 2 * 4 * length * tc * 4 + 24 * length * tc * 4
    return pl.pallas_call(
        functools.partial(_filter_kernel, length=length),
        grid=(nct,),
        in_specs=[const((length, LANES)), const((LANES, LANES)), const((1, LANES)), const((1, LANES)),
                  const((LANES, LANES)), const((1, LANES)),
                  pl.BlockSpec((LANES, tc), lambda j: (0, j)), pl.BlockSpec((LANES, tc), lambda j: (0, nct + j)),
                  pl.BlockSpec((1, tc), lambda j: (0, j)), pl.BlockSpec((1, tc), lambda j: (0, nct + j)),
                  const((length, 1)), pl.BlockSpec((1, tc), lambda j: (0, j)),
                  pl.BlockSpec((2 * length, length), lambda j: (0, 0), pipeline_mode=pl.Buffered(1))],
        out_specs=[plane] * 4,
        out_shape=[jax.ShapeDtypeStruct((length, width), F32)] * 4,
        compiler_params=_cparams(1, est),
        name="hyena_filter",
    )(zp, pad2(w1), padv(b1), padv(sin_freq), pad2(w2), padv(b2), w3p, w3p, b3r, b3r, t, absdelta, wf)


def _conv_fwd_kernel(z_ref, a_ref, b_ref, c_ref, d_ref, wf_ref, y_ref, *, length, chunk):
    z = z_ref[0]

    def body(s, carry):
        f0 = pl.multiple_of(s * chunk, chunk)
        vr = _dot(wf_ref[pl.ds(f0, chunk), :], z)
        vi = _dot(wf_ref[pl.ds(length + f0, chunk), :], z)
        sl = pl.ds(f0, chunk)
        y_ref[0, sl, :] = (vr * a_ref[sl, :] + vi * b_ref[sl, :]).astype(BF16)
        y_ref[0, pl.ds(length + f0, chunk), :] = (vr * c_ref[sl, :] + vi * d_ref[sl, :]).astype(BF16)
        return carry

    lax.fori_loop(0, length // chunk, body, 0)


def _conv_inv_kernel(y_ref, wi_ref, z_ref, x0_ref, db_ref, o_ref, *, length, chunk):
    yf = y_ref[0]

    def body(s, carry):
        sl = pl.ds(pl.multiple_of(s * chunk, chunk), chunk)
        y = _dot(wi_ref[sl, :], yf)
        o_ref[0, sl, :] = (x0_ref[0, sl, :].astype(F32)
                           * (y + z_ref[0, sl, :].astype(F32) * db_ref[...])).astype(BF16)
        return carry

    lax.fori_loop(0, length // chunk, body, 0)


def _long_conv(z, x0, planes, d_bias, wf, wi):
    b, length, width = z.shape
    tc = CONV_COLS
    chunk = min(CONV_CHUNK, length)
    single = pl.Buffered(1)
    plane = pl.BlockSpec((length, tc), lambda j, i: (0, j), pipeline_mode=single)
    est_f = 2 * length * length * 2 + 4 * length * tc * 4 + 2 * length * tc * 2 + 4 * length * tc * 2 + 8 * chunk * tc * 4
    spec = pl.pallas_call(
        functools.partial(_conv_fwd_kernel, length=length, chunk=chunk),
        grid=(width // tc, b),
        in_specs=[pl.BlockSpec((1, length, tc), lambda j, i: (i, 0, j))] + [plane] * 4
        + [pl.BlockSpec((2 * length, length), lambda j, i: (0, 0), pipeline_mode=single)],
        out_specs=pl.BlockSpec((1, 2 * length, tc), lambda j, i: (i, 0, j)),
        out_shape=jax.ShapeDtypeStruct((b, 2 * length, width), BF16),
        compiler_params=_cparams(2, est_f),
        name="hyena_conv_fwd",
    )(z, *planes, wf)
    col = lambda rows: pl.BlockSpec((1, rows, tc), lambda i, j: (i, 0, j))
    est_i = 2 * length * length * 2 + 4 * length * tc * 2 + 6 * length * tc * 2 + 8 * chunk * tc * 4
    return pl.pallas_call(
        functools.partial(_conv_inv_kernel, length=length, chunk=chunk),
        grid=(b, width // tc),
        in_specs=[col(2 * length),
                  pl.BlockSpec((length, 2 * length), lambda i, j: (0, 0), pipeline_mode=single),
                  col(length), col(length), pl.BlockSpec((1, tc), lambda i, j: (0, j))],
        out_specs=col(length),
        out_shape=jax.ShapeDtypeStruct((b, length, width), BF16),
        compiler_params=_cparams(2, est_i),
        name="hyena_conv_inv",
    )(spec, wi, z, x0, d_bias.reshape(1, width))


def _hyena_mixer(x, shift, scale, g, hy, wf, wi):
    w_in, b_in, conv_w, conv_b, w1, b1, w2, b2, w3, b3, sin_freq, d_bias = hy
    x0, z = _hyena_in(x, shift, scale, g, w_in, b_in, conv_w, conv_b)
    planes = _hyena_filter_spectrum(x.shape[1], w1, b1, w2, b2, w3, b3, sin_freq, wf)
    return _long_conv(z, x0, planes, d_bias, wf, wi)


def kernel(x, c, ctx, c_ctx, w_mod, b_mod, norm1_g, norm2_g, attn_w_qkv, attn_q_gain, attn_k_gain, attn_w_o, hy_w_in, hy_b_in, hy_conv_w, hy_conv_b, hy_filt_w1, hy_filt_b1, hy_filt_w2, hy_filt_b2, hy_filt_w3, hy_filt_b3, hy_sin_freq, hy_d_bias, hy_w_out, hy_b_out, moe_router_w, moe_router_b, moe_w_gu, moe_b_gu, moe_w_down, moe_b_down):
    b, length, d = x.shape
    n_ctx = ctx.shape[1]
    depth = w_mod.shape[0]
    n_q_heads = d // HEAD_DIM

    r_pad = -(-(b + 1) // SUBLANES) * SUBLANES
    cc = jnp.zeros((r_pad, d), F32).at[:b].set(c).at[b].set(c_ctx)
    mod_all = _modulation(cc, w_mod, b_mod).reshape(depth, r_pad, 6, d)

    rope_tabs = _rope_tables(length, n_q_heads + N_KV_HEADS)
    dft = {}

    def dft_pair(n):
        if n not in dft:
            wf = _dft_matrix(n)
            dft[n] = (wf, wf.T)
        return dft[n]

    for i in range(depth):
        j = i // N_MIXERS
        is_attn = i % N_MIXERS == 0
        ctx_live = any(m % N_MIXERS == 0 for m in range(i + 1, depth))
        mod = [mod_all[i, :b, s].reshape(b, 1, d) for s in range(6)]
        mod_c = [mod_all[i, b, s].reshape(1, 1, d) for s in range(6)]
        if is_attn:
            w_qkv = attn_w_qkv[j].astype(BF16)
            n_q = n_q_heads * HEAD_DIM
            q, k, v = _qkv_project(x, mod[0], mod[1], norm1_g[i], w_qkv, attn_q_gain[j], attn_k_gain[j],
                                   rope_tabs, True)
            w_c = w_qkv if ctx_live else w_qkv[:, n_q:]
            ctx_out = _qkv_project(ctx, mod_c[0], mod_c[1], norm1_g[i], w_c, attn_q_gain[j], attn_k_gain[j],
                                   None, ctx_live)
            k_c, v_c = ctx_out[-2], ctx_out[-1]
            a = _attention(q, [(k, v), (k_c, v_c)])
            a_c = _attention(ctx_out[0], [(k_c, v_c)]) if ctx_live else None
            w_out, b_out = attn_w_o[j].astype(BF16), jnp.zeros((d,), F32)
        else:
            hy = (hy_w_in[j].astype(BF16), hy_b_in[j], hy_conv_w[j], hy_conv_b[j], hy_filt_w1[j], hy_filt_b1[j],
                  hy_filt_w2[j], hy_filt_b2[j], hy_filt_w3[j], hy_filt_b3[j], hy_sin_freq[j], hy_d_bias[j])
            a = _hyena_mixer(x, mod[0], mod[1], norm1_g[i], hy, *dft_pair(length))
            a_c = _hyena_mixer(ctx, mod_c[0], mod_c[1], norm1_g[i], hy, *dft_pair(n_ctx)) if ctx_live else None
            w_out, b_out = hy_w_out[j].astype(BF16), hy_b_out[j]

        streams = [list(_post_mixer(a, w_out, b_out, x, mod[2], mod[3], mod[4], norm2_g[i],
                                    moe_router_w[i], moe_router_b[i])) + [mod[5]]]
        if ctx_live:
            streams.append(list(_post_mixer(a_c, w_out, b_out, ctx, mod_c[2], mod_c[3], mod_c[4], norm2_g[i],
                                            moe_router_w[i], moe_router_b[i])) + [mod_c[5]])
        w_gu = moe_w_gu[i]
        expert_w = (w_gu[:, :, 0::2].astype(BF16), w_gu[:, :, 1::2].astype(BF16), moe_w_down[i].astype(BF16),
                    moe_b_gu[i][:, None, 0::2], moe_b_gu[i][:, None, 1::2], moe_b_down[i][:, None, :])
        outs = _moe(streams, expert_w)
        x = outs[0]
        if ctx_live:
            ctx = outs[1]
    return x
```

```python
import functools
import math

import jax
import jax.numpy as jnp
from jax import lax
from jax.experimental import pallas as pl
from jax.experimental.pallas import tpu as pltpu

F32 = jnp.float32
BF16 = jnp.bfloat16
I32 = jnp.int32

LANES = 128
SUBLANES = 8
VMEM_BYTES = 64 * 2**20
VMEM_CAP = VMEM_BYTES - 8 * 2**20

GRID_W = 64
EPS = 1e-6
HEAD_DIM = 64
N_KV_HEADS = 4
ROPE_THETA = 10000.0
ROPE_PAIRS = HEAD_DIM // 4
HY_SHORT = 3
HY_EMB = 33
HY_BANDS = (HY_EMB - 1) // 2
HY_FAST_DECAY = 0.3
HY_SLOW_DECAY = 1.5
HY_TARGET = 1e-2
N_EXPERTS = 32
TOP_K = 4
SWIGLU_LIMIT = 7.0
SWIGLU_ALPHA = 1.702
N_MIXERS = 2
NEG_BIG = -1e30

ROW_TILE = 512
ATT_Q_TILE = 128
EXPERT_TILE = 512
TOKEN_TILE = 256
CONV_COLS = 256
CONV_CHUNK = 512


def _cparams(n_axes, vmem_estimate):
    limit = int(min(max(vmem_estimate * 5 // 4, 32 * 2**20), VMEM_CAP))
    return pltpu.CompilerParams(dimension_semantics=("arbitrary",) * n_axes, vmem_limit_bytes=limit)


def _split(a):
    hi = a.astype(BF16)
    lo = (a - hi.astype(F32)).astype(BF16)
    return hi, lo


def _dot(a, b):
    return jnp.dot(a, b, preferred_element_type=F32)


def _dot3(a, b):
    a_hi, a_lo = _split(a)
    b_hi, b_lo = _split(b)
    return _dot(a_hi, b_hi) + _dot(a_lo, b_hi) + _dot(a_hi, b_lo)


def _norm_mod(x, g, shift, scale):
    y = x * lax.rsqrt(jnp.mean(x * x, axis=-1, keepdims=True) + EPS)
    return (y * g) * (1.0 + scale) + shift


def _pack(a, b):
    ua = lax.bitcast_convert_type(a.astype(BF16).astype(F32), I32)
    ub = lax.bitcast_convert_type(b.astype(BF16).astype(F32), I32)
    return (ua & jnp.int32(-65536)) | lax.shift_right_logical(ub, jnp.int32(16))


def _unpack(p):
    a = lax.bitcast_convert_type(p & jnp.int32(-65536), F32)
    b = lax.bitcast_convert_type(lax.shift_left(p, jnp.int32(16)), F32)
    return a, b


def _row_chunks(d):
    return d // (2 * LANES)


def _store_token_rows(ref, lead, y):
    m, d = y.shape
    c = _row_chunks(d)
    for j in range(c):
        lo = y[:, LANES * j:LANES * (j + 1)]
        hi = y[:, d // 2 + LANES * j:d // 2 + LANES * (j + 1)]
        ref[lead + (pl.ds(j, m, stride=c), slice(None))] = _pack(lo, hi)


def _load_token_rows(ref, lead, m, c):
    halves = [_unpack(ref[lead + (pl.ds(j, m, stride=c), slice(None))]) for j in range(c)]
    return jnp.concatenate([h[0] for h in halves] + [h[1] for h in halves], axis=1)


def _mod_kernel(c_ref, w_ref, b_ref, o_ref):
    c = c_ref[...]
    a = c * jax.nn.sigmoid(c)
    o_ref[0] = _dot3(a, w_ref[0]) + b_ref[0]


def _modulation(cc, w_mod, b_mod):
    depth, d, n = w_mod.shape
    r = cc.shape[0]
    tn = 1536
    return pl.pallas_call(
        _mod_kernel,
        grid=(depth, n // tn),
        in_specs=[
            pl.BlockSpec((r, d), lambda i, j: (0, 0)),
            pl.BlockSpec((1, d, tn), lambda i, j: (i, 0, j)),
            pl.BlockSpec((1, 1, tn), lambda i, j: (i, 0, j)),
        ],
        out_specs=pl.BlockSpec((1, r, tn), lambda i, j: (i, 0, j)),
        out_shape=jax.ShapeDtypeStruct((depth, r, n), F32),
        compiler_params=_cparams(2, 2 * d * tn * 4 + 3 * d * tn * 2),
        name="modulation",
    )(cc, w_mod, b_mod.reshape(depth, 1, n))


def _qkv_kernel(*refs, n_qk, rope, with_q):
    x_ref, sh_ref, sc_ref, g_ref, w_ref, gain_ref, e_ref, et_ref = refs[:8]
    pos = 8
    if rope:
        cos_ref, sa_ref, sb_ref = refs[pos:pos + 3]
        pos += 3
    outs = refs[pos:]
    h = _norm_mod(x_ref[0], g_ref[...], sh_ref[0], sc_ref[0]).astype(BF16)
    acc = _dot(h, w_ref[...])
    qk = acc[:, :n_qk]
    ssq = _dot((qk * qk).astype(BF16), e_ref[...])
    r_hi, r_lo = _split(lax.rsqrt(ssq * (1.0 / HEAD_DIM) + EPS))
    y = qk * (_dot(r_hi, et_ref[...]) + _dot(r_lo, et_ref[...])) * gain_ref[...]
    if rope:
        y = (y * cos_ref[...] + pltpu.roll(y, n_qk - ROPE_PAIRS, 1) * sa_ref[...]
             + pltpu.roll(y, ROPE_PAIRS, 1) * sb_ref[...])
    n_q = n_qk - N_KV_HEADS * HEAD_DIM
    if with_q:
        outs[0][0] = (y[:, :n_q] * (HEAD_DIM ** -0.5)).astype(BF16)
        outs = outs[1:]
    outs[0][0] = y[:, n_q:].astype(BF16)
    outs[1][0] = acc[:, n_qk:].astype(BF16)


def _qkv_project(x, shift, scale, g, w, q_gain, k_gain, rope_tabs, with_q):
    b, t, d = x.shape
    kv = N_KV_HEADS * HEAD_DIM
    nc = w.shape[1]
    n_qk = nc - kv
    n_heads = n_qk // HEAD_DIM
    tm = min(ROW_TILE, t)
    gains = ([q_gain] * (n_heads - N_KV_HEADS)) + [k_gain] * N_KV_HEADS
    gain = jnp.concatenate(gains).reshape(1, n_qk).astype(F32)
    head_of_col = jnp.arange(n_qk) // HEAD_DIM
    e = (head_of_col[:, None] == jnp.arange(LANES)[None, :]).astype(BF16)
    et = e.T
    bm = shift.shape[0]
    mod_map = (lambda i, j: (j, 0, 0)) if bm > 1 else (lambda i, j: (0, 0, 0))
    in_specs = [
        pl.BlockSpec((1, tm, d), lambda i, j: (j, i, 0)),
        pl.BlockSpec((1, 1, d), mod_map),
        pl.BlockSpec((1, 1, d), mod_map),
        pl.BlockSpec((1, d), lambda i, j: (0, 0)),
        pl.BlockSpec((d, nc), lambda i, j: (0, 0)),
        pl.BlockSpec((1, n_qk), lambda i, j: (0, 0)),
        pl.BlockSpec((n_qk, LANES), lambda i, j: (0, 0)),
        pl.BlockSpec((LANES, n_qk), lambda i, j: (0, 0)),
    ]
    args = [x, shift, scale, g.reshape(1, d), w, gain, e, et]
    rope = rope_tabs is not None
    if rope:
        in_specs += [pl.BlockSpec((tm, n_qk), lambda i, j: (i, 0))] * 3
        args += list(rope_tabs)
    out_shape, out_specs = [], []
    widths = ([n_qk - kv] if with_q else []) + [kv, kv]
    for wd in widths:
        out_shape.append(jax.ShapeDtypeStruct((b, t, wd), BF16))
        out_specs.append(pl.BlockSpec((1, tm, wd), lambda i, j: (j, i, 0)))
    est = 2 * tm * d * 4 + 2 * d * nc * 2 + 6 * tm * n_qk * 4 + 8 * tm * nc * 4
    return pl.pallas_call(
        functools.partial(_qkv_kernel, n_qk=n_qk, rope=rope, with_q=with_q),
        grid=(t // tm, b),
        in_specs=in_specs,
        out_specs=out_specs,
        out_shape=out_shape,
        compiler_params=_cparams(2, est),
        name="qkv_project",
    )(*args)


def _rope_tables(n_tokens, n_heads):
    n_rows = n_tokens // GRID_W
    row = jnp.repeat(jnp.arange(n_rows, dtype=F32), GRID_W)
    col = jnp.tile(jnp.arange(GRID_W, dtype=F32), n_rows)
    inv_freq = ROPE_THETA ** (-jnp.arange(ROPE_PAIRS, dtype=F32) / ROPE_PAIRS)
    ar, ac = row[:, None] * inv_freq, col[:, None] * inv_freq
    zero = jnp.zeros_like(ar)
    cos = jnp.concatenate([jnp.cos(ar), jnp.cos(ar), jnp.cos(ac), jnp.cos(ac)], axis=1)
    sin_a = jnp.concatenate([-jnp.sin(ar), zero, -jnp.sin(ac), zero], axis=1)
    sin_b = jnp.concatenate([zero, jnp.sin(ar), zero, jnp.sin(ac)], axis=1)
    return tuple(jnp.tile(tab, (1, n_heads)) for tab in (cos, sin_a, sin_b))


def _attn_kernel(*refs, n_seg, tq):
    q_ref = refs[0]
    kv_refs = refs[1:1 + 2 * n_seg]
    o_ref = refs[1 + 2 * n_seg]
    q_per_kv = q_ref.shape[2] // (N_KV_HEADS * HEAD_DIM)
    for g in range(N_KV_HEADS):
        heads = [g * q_per_kv + j for j in range(q_per_kv)]
        qs = jnp.concatenate([q_ref[0, :, h * HEAD_DIM:(h + 1) * HEAD_DIM] for h in heads], axis=0)
        ksl = slice(g * HEAD_DIM, (g + 1) * HEAD_DIM)
        scores = [lax.dot_general(qs, kv_refs[2 * s][0, :, ksl], (((1,), (1,)), ((), ())),
                                  preferred_element_type=F32) for s in range(n_seg)]
        m = functools.reduce(jnp.maximum, [jnp.max(sc, axis=-1, keepdims=True) for sc in scores])
        probs = [jnp.exp(sc - m) for sc in scores]
        denom = functools.reduce(jnp.add, [jnp.sum(p, axis=-1, keepdims=True) for p in probs])
        o = functools.reduce(jnp.add, [_dot(probs[s].astype(BF16), kv_refs[2 * s + 1][0, :, ksl])
                                       for s in range(n_seg)])
        o = o / denom
        for j, h in enumerate(heads):
            o_ref[0, :, h * HEAD_DIM:(h + 1) * HEAD_DIM] = o[j * tq:(j + 1) * tq].astype(BF16)


def _attention(q, segments):
    b, t, dq = q.shape
    tq = min(ATT_Q_TILE, t)
    in_specs = [pl.BlockSpec((1, tq, dq), lambda i, j: (i, j, 0))]
    args = [q]
    t_keys = 0
    for k, v in segments:
        ts, kv = k.shape[1], k.shape[2]
        in_specs += [pl.BlockSpec((1, ts, kv), lambda i, j: (i, 0, 0))] * 2
        args += [k, v]
        t_keys += ts
    q_rows = tq * dq // (N_KV_HEADS * HEAD_DIM)
    est = 4 * tq * dq * 2 + 8 * t_keys * N_KV_HEADS * HEAD_DIM * 2 + 4 * q_rows * t_keys * 4
    return pl.pallas_call(
        functools.partial(_attn_kernel, n_seg=len(segments), tq=tq),
        grid=(b, t // tq),
        in_specs=in_specs,
        out_specs=pl.BlockSpec((1, tq, dq), lambda i, j: (i, j, 0)),
        out_shape=jax.ShapeDtypeStruct((b, t, dq), BF16),
        compiler_params=_cparams(2, est),
        name="attention",
    )(*args)


def _post_kernel(a_ref, w_ref, bias_ref, x_ref, gate_ref, sh_ref, sc_ref, g_ref, rw_ref, rb_ref,
                 xo_ref, hp_ref, te_ref, tg_ref):
    o = _dot(a_ref[0], w_ref[...]) + bias_ref[...]
    x_new = x_ref[0] + gate_ref[0] * o
    xo_ref[0] = x_new
    h = _norm_mod(x_new, g_ref[...], sh_ref[0], sc_ref[0])
    _store_token_rows(hp_ref, (0,), h)
    logits = _dot3(h, rw_ref[...]) + rb_ref[...]
    lane = lax.broadcasted_iota(I32, logits.shape, 1)
    e_out = jnp.zeros(logits.shape, I32)
    g_out = jnp.zeros(logits.shape, F32)
    top = None
    total = None
    exps = []
    for k in range(TOP_K):
        m = jnp.max(logits, axis=1, keepdims=True)
        idx = jnp.min(jnp.where(logits == m, lane, LANES), axis=1, keepdims=True)
        logits = jnp.where(lane == idx, -jnp.inf, logits)
        e_out = jnp.where(lane == k, idx, e_out)
        top = m if top is None else top
        ex = jnp.exp(m - top)
        exps.append(ex)
        total = ex if total is None else total + ex
    for k in range(TOP_K):
        g_out = jnp.where(lane == k, exps[k] / total, g_out)
    te_ref[0] = e_out
    tg_ref[0] = g_out


def _post_mixer(a, w, bias, x, gate, shift, scale, g, router_w, router_b):
    b, t, d = x.shape
    k_in = a.shape[2]
    c = _row_chunks(d)
    tm = min(ROW_TILE, t)
    bm = gate.shape[0]
    mod_map = (lambda i, j: (i, 0, 0)) if bm > 1 else (lambda i, j: (0, 0, 0))
    rw = jnp.zeros((d, LANES), F32).at[:, :N_EXPERTS].set(router_w)
    rb = jnp.full((1, LANES), NEG_BIG, F32).at[0, :N_EXPERTS].set(router_b)
    row_spec = lambda wd: pl.BlockSpec((1, tm, wd), lambda i, j: (i, j, 0))
    const = lambda shape: pl.BlockSpec(shape, lambda i, j: (0,) * len(shape))
    est = 2 * tm * (k_in * 2 + d * 4 * 2 + d * 2 + 2 * LANES * 4) + 2 * k_in * d * 2 + 8 * tm * d * 4
    return pl.pallas_call(
        _post_kernel,
        grid=(b, t // tm),
        in_specs=[row_spec(k_in), const((k_in, d)), const((1, d)), row_spec(d),
                  pl.BlockSpec((1, 1, d), mod_map), pl.BlockSpec((1, 1, d), mod_map),
                  pl.BlockSpec((1, 1, d), mod_map), const((1, d)), const((d, LANES)), const((1, LANES))],
        out_specs=[row_spec(d), pl.BlockSpec((1, tm * c, LANES), lambda i, j: (i, j, 0)),
                   row_spec(LANES), row_spec(LANES)],
        out_shape=[jax.ShapeDtypeStruct((b, t, d), F32), jax.ShapeDtypeStruct((b, t * c, LANES), I32),
                   jax.ShapeDtypeStruct((b, t, LANES), I32), jax.ShapeDtypeStruct((b, t, LANES), F32)],
        compiler_params=_cparams(2, est),
        name="post_mixer",
    )(a, w, bias.reshape(1, d), x, gate, shift, scale, g.reshape(1, d), rw, rb)


def _split_kernel(w_ref, s_ref, g_ref, u_ref):
    for blk in range(w_ref.shape[2] // (2 * LANES)):
        cols = w_ref[0, :, 2 * LANES * blk:2 * LANES * (blk + 1)].astype(BF16)
        r = _dot(cols, s_ref[...])
        g_ref[0, :, LANES * blk:LANES * (blk + 1)] = r[:, :LANES].astype(BF16)
        u_ref[0, :, LANES * blk:LANES * (blk + 1)] = r[:, LANES:].astype(BF16)


def _split_gate_up(w_gu):
    n, d, f2 = w_gu.shape
    tr = min(ROW_TILE, d)
    col = jnp.arange(2 * LANES)
    src = jnp.where(col < LANES, 2 * col, 2 * (col - LANES) + 1)
    sel = (col[:, None] == src[None, :]).astype(BF16)
    out = pl.BlockSpec((1, tr, f2 // 2), lambda i, j: (i, j, 0))
    return pl.pallas_call(
        _split_kernel,
        grid=(n, d // tr),
        in_specs=[pl.BlockSpec((1, tr, f2), lambda i, j: (i, j, 0)),
                  pl.BlockSpec((2 * LANES, 2 * LANES), lambda i, j: (0, 0))],
        out_specs=[out, out],
        out_shape=[jax.ShapeDtypeStruct((n, d, f2 // 2), BF16)] * 2,
        compiler_params=_cparams(2, 2 * tr * f2 * 4 + 4 * tr * f2 + 4 * tr * f2),
        name="moe_split_gate_up",
    )(w_gu, sel)


def _route(top_e, n_rows_max):
    flat = top_e.reshape(-1)
    onehot = (flat[:, None] == jnp.arange(N_EXPERTS, dtype=I32)[None, :]).astype(I32)
    csum = jnp.cumsum(onehot, axis=0)
    counts = csum[-1]
    rank = jnp.sum(onehot * csum, axis=1) - 1
    padded = (counts + EXPERT_TILE - 1) // EXPERT_TILE * EXPERT_TILE
    pad_end = jnp.cumsum(padded)
    pad_start = pad_end - padded
    dest = jnp.sum(onehot * pad_start[None, :], axis=1) + rank
    n_tiles_max = n_rows_max // EXPERT_TILE
    tile_e = jnp.minimum(jnp.searchsorted(pad_end, jnp.arange(n_tiles_max, dtype=I32) * EXPERT_TILE,
                                          side="right"), N_EXPERTS - 1).astype(I32)
    n_valid = (pad_end[-1] // EXPERT_TILE).astype(I32).reshape(1)
    return dest.astype(I32), tile_e, n_valid, pad_end.astype(I32), padded.astype(I32)


def _dispatch_kernel(pend_ref, padded_ref, dest_ref, h_ref, *rest, tt, c, first):
    if first:
        xs_hbm, zbuf, zsem, row_sem = rest
    else:
        _, xs_hbm, row_sem = rest
    i = pl.program_id(0)

    if first:
        @pl.when(i == 0)
        def _():
            zbuf[...] = jnp.zeros(zbuf.shape, zbuf.dtype)

            def fill(e, carry):
                @pl.when(padded_ref[e] > 0)
                def _():
                    start = pl.multiple_of((pend_ref[e] - EXPERT_TILE) * c, EXPERT_TILE * c)
                    cp = pltpu.make_async_copy(zbuf, xs_hbm.at[pl.ds(start, EXPERT_TILE * c)], zsem)
                    cp.start()
                    cp.wait()
                return carry

            lax.fori_loop(0, N_EXPERTS, fill, 0)

            def fill_tail(t, carry):
                cp = pltpu.make_async_copy(
                    zbuf, xs_hbm.at[pl.ds(pl.multiple_of(t * (EXPERT_TILE * c), EXPERT_TILE * c), EXPERT_TILE * c)],
                    zsem)
                cp.start()
                cp.wait()
                return carry

            lax.fori_loop(pend_ref[N_EXPERTS - 1] // EXPERT_TILE, xs_hbm.shape[0] // (EXPERT_TILE * c),
                          fill_tail, 0)

    def start(r, carry):
        for k in range(TOP_K):
            dst = pl.multiple_of(dest_ref[0, 0, TOP_K * r + k] * c, c)
            pltpu.make_async_copy(h_ref.at[pl.ds(pl.multiple_of(r * c, c), c)], xs_hbm.at[pl.ds(dst, c)],
                                  row_sem).start()
        return carry

    lax.fori_loop(0, tt, start, 0, unroll=8)

    def wait(r, carry):
        for k in range(TOP_K):
            pltpu.make_async_copy(h_ref.at[pl.ds(0, c)], xs_hbm.at[pl.ds(0, c)], row_sem).wait()
        return carry

    lax.fori_loop(0, tt, wait, 0, unroll=8)


def _dispatch(dest, h_rows, xs, pad_end, padded, n_rows, c):
    n = h_rows.shape[0] // c
    tt = min(TOKEN_TILE, n)
    first = xs is None
    in_specs = [pl.BlockSpec((1, 1, tt * TOP_K), lambda i, pe, pd: (i, 0, 0), memory_space=pltpu.SMEM),
                pl.BlockSpec((tt * c, LANES), lambda i, pe, pd: (i, 0))]
    args = [pad_end, padded, dest.reshape(n // tt, 1, tt * TOP_K), h_rows]
    scratch = [pltpu.SemaphoreType.DMA]
    if first:
        scratch = [pltpu.VMEM((EXPERT_TILE * c, LANES), I32), pltpu.SemaphoreType.DMA] + scratch
    else:
        in_specs.append(pl.BlockSpec(memory_space=pl.ANY))
        args.append(xs)
    return pl.pallas_call(
        functools.partial(_dispatch_kernel, tt=tt, c=c, first=first),
        grid_spec=pltpu.PrefetchScalarGridSpec(
            num_scalar_prefetch=2, grid=(n // tt,), in_specs=in_specs,
            out_specs=pl.BlockSpec(memory_space=pl.ANY), scratch_shapes=scratch),
        out_shape=jax.ShapeDtypeStruct((n_rows * c, LANES), I32),
        input_output_aliases={} if first else {4: 0},
        compiler_params=pltpu.CompilerParams(dimension_semantics=("arbitrary",), has_side_effects=True),
        name="moe_dispatch",
    )(*args)


def _expert_kernel(te_ref, nv_ref, xs_ref, wg_ref, wu_ref, wd_ref, bg_ref, bu_ref, bd_ref, ys_ref):
    @pl.when(pl.program_id(0) < nv_ref[0])
    def _():
        c = _row_chunks(wg_ref.shape[1])
        xb = _load_token_rows(xs_ref, (), xs_ref.shape[0] // c, c).astype(BF16)
        g = jnp.minimum(_dot(xb, wg_ref[0]) + bg_ref[0], SWIGLU_LIMIT)
        u = jnp.clip(_dot(xb, wu_ref[0]) + bu_ref[0], -SWIGLU_LIMIT, SWIGLU_LIMIT)
        act = (u + 1.0) * g * jax.nn.sigmoid(SWIGLU_ALPHA * g)
        y = _dot(act.astype(BF16), wd_ref[0]) + bd_ref[0]
        _store_token_rows(ys_ref, (), y)

    @pl.when(pl.program_id(0) >= nv_ref[0])
    def _():
        ys_ref[...] = jnp.zeros(ys_ref.shape, ys_ref.dtype)


def _experts(xs, tile_e, n_valid, wg, wu, wd, bg, bu, bd):
    d, f = wg.shape[1:]
    c = _row_chunks(d)
    rows = xs.shape[0] // c
    tm = EXPERT_TILE
    tile_map = lambda i, te, nv: (jnp.minimum(i, nv[0] - 1), 0)
    w_map = lambda i, te, nv: (te[jnp.minimum(i, nv[0] - 1)], 0, 0)
    est = 4 * tm * d * 2 + 2 * 3 * d * f * 2 + 8 * tm * f * 4
    return pl.pallas_call(
        _expert_kernel,
        grid_spec=pltpu.PrefetchScalarGridSpec(
            num_scalar_prefetch=2,
            grid=(rows // tm,),
            in_specs=[pl.BlockSpec((tm * c, LANES), tile_map),
                      pl.BlockSpec((1, d, f), w_map), pl.BlockSpec((1, d, f), w_map),
                      pl.BlockSpec((1, f, d), w_map),
                      pl.BlockSpec((1, 1, f), w_map), pl.BlockSpec((1, 1, f), w_map),
                      pl.BlockSpec((1, 1, d), w_map)],
            out_specs=pl.BlockSpec((tm * c, LANES), lambda i, te, nv: (i, 0)),
        ),
        out_shape=jax.ShapeDtypeStruct((rows * c, LANES), I32),
        compiler_params=_cparams(1, est),
        name="moe_experts",
    )(tile_e, n_valid, xs, wg, wu, wd, bg, bu, bd)


def _combine_kernel(dest_ref, dest_next_ref, ys_hbm, gates_ref, x_ref, gate_ref, o_ref, buf, sems, *, tt, c):
    i = pl.program_id(0)
    slot = i % 2

    def issue(idx_ref, s):
        def body(r, carry):
            for k in range(TOP_K):
                src = pl.multiple_of(idx_ref[0, 0, TOP_K * r + k] * c, c)
                pltpu.make_async_copy(ys_hbm.at[pl.ds(src, c)],
                                      buf.at[s, k, pl.ds(pl.multiple_of(r * c, c), c)], sems.at[s]).start()
            return carry

        lax.fori_loop(0, tt, body, 0, unroll=8)

    @pl.when(i == 0)
    def _():
        issue(dest_ref, 0)

    @pl.when(i + 1 < pl.num_programs(0))
    def _():
        issue(dest_next_ref, 1 - slot)

    pltpu.make_async_copy(buf.at[slot], buf.at[slot], sems.at[slot]).wait()
    gates = gates_ref[...]
    y = None
    for k in range(TOP_K):
        yk = _load_token_rows(buf, (slot, k), tt, c) * gates[:, k:k + 1]
        y = yk if y is None else y + yk
    o_ref[...] = x_ref[...] + gate_ref[0] * y


def _combine(dest, ys, gates, x, gate):
    b, t, d = x.shape
    n = b * t
    tt = min(TOKEN_TILE, t)
    steps = n // tt
    steps_per_batch = t // tt
    bm = gate.shape[0]
    gate_map = (lambda i: (i // steps_per_batch, 0, 0)) if bm > 1 else (lambda i: (0, 0, 0))
    est = 2 * TOP_K * tt * d * 2 + 4 * tt * d * 4 + 2 * tt * LANES * 4 + 6 * tt * d * 4
    dest2 = dest.reshape(steps, 1, tt * TOP_K)
    c = _row_chunks(d)
    out = pl.pallas_call(
        functools.partial(_combine_kernel, tt=tt, c=c),
        grid=(steps,),
        in_specs=[pl.BlockSpec((1, 1, tt * TOP_K), lambda i: (i, 0, 0), memory_space=pltpu.SMEM),
                  pl.BlockSpec((1, 1, tt * TOP_K), lambda i: (jnp.minimum(i + 1, steps - 1), 0, 0),
                               memory_space=pltpu.SMEM),
                  pl.BlockSpec(memory_space=pl.ANY),
                  pl.BlockSpec((tt, LANES), lambda i: (i, 0)),
                  pl.BlockSpec((tt, d), lambda i: (i, 0)),
                  pl.BlockSpec((1, 1, d), gate_map)],
        out_specs=pl.BlockSpec((tt, d), lambda i: (i, 0)),
        out_shape=jax.ShapeDtypeStruct((n, d), F32),
        scratch_shapes=[pltpu.VMEM((2, TOP_K, tt * c, LANES), I32), pltpu.SemaphoreType.DMA((2,))],
        compiler_params=_cparams(1, est),
        name="moe_combine",
    )(dest2, dest2, ys, gates.reshape(n, LANES), x.reshape(n, d), gate)
    return out.reshape(b, t, d)


def _moe(streams, expert_w):
    wg, wu, wd, bg, bu, bd = expert_w
    n_pairs = sum(s[0].shape[0] * s[0].shape[1] for s in streams) * TOP_K
    n_tiles_max = (n_pairs + N_EXPERTS * (EXPERT_TILE - 1)) // EXPERT_TILE
    n_rows_max = n_tiles_max * EXPERT_TILE
    top_e = jnp.concatenate([s[2][..., :TOP_K].reshape(-1, TOP_K) for s in streams], axis=0)
    dest, tile_e, n_valid, pad_end, padded = _route(top_e, n_rows_max)
    c = _row_chunks(streams[0][0].shape[2])
    xs = None
    offs = 0
    dests = []
    for s in streams:
        n = s[0].shape[0] * s[0].shape[1]
        dests.append(dest[offs * TOP_K:(offs + n) * TOP_K])
        xs = _dispatch(dests[-1], s[1].reshape(n * c, LANES), xs, pad_end, padded, n_rows_max, c)
        offs += n
    ys = _experts(xs, tile_e, n_valid, wg, wu, wd, bg, bu, bd)
    return [_combine(dests[i], ys, s[3], s[0], s[4]) for i, s in enumerate(streams)]


def _hyena_in_kernel(xm_ref, xp_ref, xn_ref, sh_ref, sc_ref, g_ref, w_ref, b_ref, cw_ref, cb_ref,
                     x0_ref, z_ref, *, tm, width, chunk):
    i = pl.program_id(1)
    last = pl.num_programs(1) - 1
    xe = jnp.concatenate([xp_ref[0], xm_ref[0], xn_ref[0]], axis=0)
    h = _norm_mod(xe, g_ref[...], sh_ref[0], sc_ref[0]).astype(BF16)
    rows = lax.broadcasted_iota(I32, (tm + 2 * SUBLANES, 1), 0)
    inside = jnp.logical_and(jnp.logical_or(rows >= SUBLANES, i > 0),
                             jnp.logical_or(rows < tm + SUBLANES, i < last))
    n_ext = tm + 2 * SUBLANES

    def conv(col):
        u = _dot(h, w_ref[:, col:col + chunk]) + b_ref[:, col:col + chunk]
        u = jnp.where(inside, u, 0.0)
        prev = pltpu.roll(u, 1, 0)[SUBLANES:SUBLANES + tm]
        nxt = pltpu.roll(u, n_ext - 1, 0)[SUBLANES:SUBLANES + tm]
        cw = cw_ref[:, col:col + chunk]
        return (prev * cw[0:1] + u[SUBLANES:SUBLANES + tm] * cw[1:2] + nxt * cw[2:3]
                + cb_ref[:, col:col + chunk])

    for c in range(width // chunk):
        x0 = conv(c * chunk)
        x1 = conv(width + c * chunk)
        v = conv(2 * width + c * chunk)
        x0_ref[0, :, c * chunk:(c + 1) * chunk] = x0.astype(BF16)
        z_ref[0, :, c * chunk:(c + 1) * chunk] = (v * x1).astype(BF16)


def _hyena_in(x, shift, scale, g, w_in, b_in, conv_w, conv_b):
    b, t, d = x.shape
    n3 = w_in.shape[1]
    width = n3 // 3
    tm = min(ROW_TILE // 2, t)
    blocks8 = tm // SUBLANES
    last8 = t // SUBLANES - 1
    bm = shift.shape[0]
    mod_map = (lambda i, j: (i, 0, 0)) if bm > 1 else (lambda i, j: (0, 0, 0))
    const = lambda shape: pl.BlockSpec(shape, lambda i, j: (0,) * len(shape))
    chunk = 512
    est = 2 * (tm + 16) * d * 4 + 2 * d * n3 * 2 + 4 * tm * width * 2 + 10 * (tm + 16) * chunk * 4 + 4 * tm * d * 4
    return pl.pallas_call(
        functools.partial(_hyena_in_kernel, tm=tm, width=width, chunk=chunk),
        grid=(b, t // tm),
        in_specs=[pl.BlockSpec((1, tm, d), lambda i, j: (i, j, 0)),
                  pl.BlockSpec((1, SUBLANES, d), lambda i, j: (i, jnp.maximum(j * blocks8 - 1, 0), 0)),
                  pl.BlockSpec((1, SUBLANES, d), lambda i, j: (i, jnp.minimum((j + 1) * blocks8, last8), 0)),
                  pl.BlockSpec((1, 1, d), mod_map), pl.BlockSpec((1, 1, d), mod_map),
                  const((1, d)), const((d, n3)), const((1, n3)), const((HY_SHORT, n3)), const((1, n3))],
        out_specs=[pl.BlockSpec((1, tm, width), lambda i, j: (i, j, 0))] * 2,
        out_shape=[jax.ShapeDtypeStruct((b, t, width), BF16)] * 2,
        compiler_params=_cparams(2, est),
        name="hyena_in",
    )(x, x, x, shift, scale, g.reshape(1, d), w_in, b_in.reshape(1, n3), conv_w, conv_b.reshape(1, n3))


def _dft_matrix(length):
    n = 2 * length
    f = jnp.arange(length, dtype=I32)
    ang = ((f[:, None] * f[None, :]) % n).astype(F32) * (2.0 * math.pi / n)
    nyq = jnp.where(f % 2 == 0, 1.0, -1.0).astype(F32)
    msin = (-jnp.sin(ang)).at[0].set(nyq)
    return jnp.concatenate([jnp.cos(ang), msin], axis=0).astype(BF16)


def _filter_kernel(z_ref, w1_ref, b1_ref, fr_ref, w2_ref, b2_ref, w3f_ref, w3b_ref, b3f_ref, b3b_ref,
                   t_ref, dl_ref, wf_ref, a_ref, b_ref, c_ref, d_ref, *, length):
    freq = fr_ref[...]
    a1 = jnp.sin(freq * (_dot3(z_ref[...], w1_ref[...]) + b1_ref[...]))
    a2 = jnp.sin(freq * (_dot3(a1, w2_ref[...]) + b2_ref[...]))
    decay = jnp.exp(-t_ref[...] * dl_ref[...])
    rows = lax.broadcasted_iota(I32, (length, 1), 0)
    first = rows == 0
    hf = (_dot3(a2, w3f_ref[...]) + b3f_ref[...]) * decay
    hb = jnp.where(first, 0.0, (_dot3(a2, w3b_ref[...]) + b3b_ref[...]) * decay)
    inv = 1.0 / (jnp.sum(jnp.abs(hf), axis=0, keepdims=True) + jnp.sum(jnp.abs(hb), axis=0, keepdims=True) + EPS)
    hf = hf * inv
    hb = hb * inv
    p_hi, p_lo = _split(hf + hb)
    q_hi, q_lo = _split(hf - hb)
    wf = wf_ref[...]
    sp = _dot(wf, p_hi) + _dot(wf, p_lo)
    ki = _dot(wf[length:], q_hi) + _dot(wf[length:], q_lo)
    kr = sp[:length]
    knyq = sp[length:length + 1]
    n = 2.0 * length
    sc = jnp.where(first, 1.0 / n, 2.0 / n)
    a_ref[...] = kr * sc
    b_ref[...] = jnp.where(first, 0.0, -ki * sc)
    c_ref[...] = jnp.where(first, 0.0, ki * sc)
    d_ref[...] = jnp.where(first, knyq * (1.0 / n), kr * sc)


def _hyena_filter_spectrum(length, w1, b1, w2, b2, w3, b3, sin_freq, wf):
    width = w3.shape[1] // 2
    ffn = w1.shape[1]
    t = jnp.linspace(0.0, 1.0, length, dtype=F32)[:, None]
    w = 2.0 * math.pi * jnp.arange(length, dtype=F32)[:, None] / length
    f = jnp.linspace(1e-4, HY_BANDS - 1, HY_BANDS, dtype=F32)
    z = jnp.concatenate([t, jnp.cos(f * w), -jnp.sin(f * w)], axis=-1)
    zp = jnp.zeros((length, LANES), F32).at[:, :HY_EMB].set(z)
    pad2 = lambda m: jnp.zeros((LANES, m.shape[1] if m.shape[1] > LANES else LANES), F32).at[:m.shape[0], :m.shape[1]].set(m)
    padv = lambda v: jnp.zeros((1, LANES), F32).at[0, :v.shape[0]].set(v)
    max_decay = math.log(HY_TARGET) / HY_FAST_DECAY
    min_decay = math.log(HY_TARGET) / HY_SLOW_DECAY
    absdelta = jnp.abs(jnp.linspace(min_decay, max_decay, width, dtype=F32)).reshape(1, width)
    tc = LANES
    nct = width // tc
    const = lambda shape: pl.BlockSpec(shape, lambda j: (0,) * len(shape))
    w3p = pad2(w3)
    b3r = b3.reshape(1, 2 * width)
    plane = pl.BlockSpec((length, tc), lambda j: (0, j))
    est = 2 * length * length * 2 + 2 * 4 * length * tc * 4 + 24 * length * tc * 4
    return pl.pallas_call(
        functools.partial(_filter_kernel, length=length),
        grid=(nct,),
        in_specs=[const((length, LANES)), const((LANES, LANES)), const((1, LANES)), const((1, LANES)),
                  const((LANES, LANES)), const((1, LANES)),
                  pl.BlockSpec((LANES, tc), lambda j: (0, j)), pl.BlockSpec((LANES, tc), lambda j: (0, nct + j)),
                  pl.BlockSpec((1, tc), lambda j: (0, j)), pl.BlockSpec((1, tc), lambda j: (0, nct + j)),
                  const((length, 1)), pl.BlockSpec((1, tc), lambda j: (0, j)),
                  pl.BlockSpec((2 * length, length), lambda j: (0, 0), pipeline_mode=pl.Buffered(1))],
        out_specs=[plane] * 4,
        out_shape=[jax.ShapeDtypeStruct((length, width), F32)] * 4,
        compiler_params=_cparams(1, est),
        name="hyena_filter",
    )(zp, pad2(w1), padv(b1), padv(sin_freq), pad2(w2), padv(b2), w3p, w3p, b3r, b3r, t, absdelta, wf)


def _conv_fwd_kernel(z_ref, a_ref, b_ref, c_ref, d_ref, wf_ref, y_ref, *, length, chunk):
    z = z_ref[0]

    def body(s, carry):
        f0 = pl.multiple_of(s * chunk, chunk)
        vr = _dot(wf_ref[pl.ds(f0, chunk), :], z)
        vi = _dot(wf_ref[pl.ds(length + f0, chunk), :], z)
        sl = pl.ds(f0, chunk)
        y_ref[0, sl, :] = (vr * a_ref[sl, :] + vi * b_ref[sl, :]).astype(BF16)
        y_ref[0, pl.ds(length + f0, chunk), :] = (vr * c_ref[sl, :] + vi * d_ref[sl, :]).astype(BF16)
        return carry

    lax.fori_loop(0, length // chunk, body, 0)


def _conv_inv_kernel(y_ref, wi_ref, z_ref, x0_ref, db_ref, o_ref, *, length, chunk):
    yf = y_ref[0]

    def body(s, carry):
        sl = pl.ds(pl.multiple_of(s * chunk, chunk), chunk)
        y = _dot(wi_ref[sl, :], yf)
        o_ref[0, sl, :] = (x0_ref[0, sl, :].astype(F32)
                           * (y + z_ref[0, sl, :].astype(F32) * db_ref[...])).astype(BF16)
        return carry

    lax.fori_loop(0, length // chunk, body, 0)


def _long_conv(z, x0, planes, d_bias, wf, wi):
    b, length, width = z.shape
    tc = CONV_COLS
    chunk = min(CONV_CHUNK, length)
    single = pl.Buffered(1)
    plane = pl.BlockSpec((length, tc), lambda j, i: (0, j), pipeline_mode=single)
    est_f = 2 * length * length * 2 + 4 * length * tc * 4 + 2 * length * tc * 2 + 4 * length * tc * 2 + 8 * chunk * tc * 4
    spec = pl.pallas_call(
        functools.partial(_conv_fwd_kernel, length=length, chunk=chunk),
        grid=(width // tc, b),
        in_specs=[pl.BlockSpec((1, length, tc), lambda j, i: (i, 0, j))] + [plane] * 4
        + [pl.BlockSpec((2 * length, length), lambda j, i: (0, 0), pipeline_mode=single)],
        out_specs=pl.BlockSpec((1, 2 * length, tc), lambda j, i: (i, 0, j)),
        out_shape=jax.ShapeDtypeStruct((b, 2 * length, width), BF16),
        compiler_params=_cparams(2, est_f),
        name="hyena_conv_fwd",
    )(z, *planes, wf)
    col = lambda rows: pl.BlockSpec((1, rows, tc), lambda i, j: (i, 0, j))
    est_i = 2 * length * length * 2 + 4 * length * tc * 2 + 6 * length * tc * 2 + 8 * chunk * tc * 4
    return pl.pallas_call(
        functools.partial(_conv_inv_kernel, length=length, chunk=chunk),
        grid=(b, width // tc),
        in_specs=[col(2 * length),
                  pl.BlockSpec((length, 2 * length), lambda i, j: (0, 0), pipeline_mode=single),
                  col(length), col(length), pl.BlockSpec((1, tc), lambda i, j: (0, j))],
        out_specs=col(length),
        out_shape=jax.ShapeDtypeStruct((b, length, width), BF16),
        compiler_params=_cparams(2, est_i),
        name="hyena_conv_inv",
    )(spec, wi, z, x0, d_bias.reshape(1, width))


def _hyena_mixer(x, shift, scale, g, hy, wf, wi):
    w_in, b_in, conv_w, conv_b, w1, b1, w2, b2, w3, b3, sin_freq, d_bias = hy
    x0, z = _hyena_in(x, shift, scale, g, w_in, b_in, conv_w, conv_b)
    planes = _hyena_filter_spectrum(x.shape[1], w1, b1, w2, b2, w3, b3, sin_freq, wf)
    return _long_conv(z, x0, planes, d_bias, wf, wi)


def kernel(x, c, ctx, c_ctx, w_mod, b_mod, norm1_g, norm2_g, attn_w_qkv, attn_q_gain, attn_k_gain, attn_w_o, hy_w_in, hy_b_in, hy_conv_w, hy_conv_b, hy_filt_w1, hy_filt_b1, hy_filt_w2, hy_filt_b2, hy_filt_w3, hy_filt_b3, hy_sin_freq, hy_d_bias, hy_w_out, hy_b_out, moe_router_w, moe_router_b, moe_w_gu, moe_b_gu, moe_w_down, moe_b_down):
    b, length, d = x.shape
    n_ctx = ctx.shape[1]
    depth = w_mod.shape[0]
    n_q_heads = d // HEAD_DIM

    r_pad = -(-(b + 1) // SUBLANES) * SUBLANES
    cc = jnp.zeros((r_pad, d), F32).at[:b].set(c).at[b].set(c_ctx)
    mod_all = _modulation(cc, w_mod, b_mod).reshape(depth, r_pad, 6, d)

    rope_tabs = _rope_tables(length, n_q_heads + N_KV_HEADS)
    n_exp, _, f2 = moe_w_gu.shape[1:]
    w_gate, w_up = (w.reshape(depth, n_exp, d, f2 // 2)
                    for w in _split_gate_up(moe_w_gu.reshape(depth * n_exp, d, f2)))
    w_down = moe_w_down.astype(BF16)
    dft = {}

    def dft_pair(n):
        if n not in dft:
            wf = _dft_matrix(n)
            dft[n] = (wf, wf.T)
        return dft[n]

    for i in range(depth):
        j = i // N_MIXERS
        is_attn = i % N_MIXERS == 0
        ctx_live = any(m % N_MIXERS == 0 for m in range(i + 1, depth))
        mod = [mod_all[i, :b, s].reshape(b, 1, d) for s in range(6)]
        mod_c = [mod_all[i, b, s].reshape(1, 1, d) for s in range(6)]
        if is_attn:
            w_qkv = attn_w_qkv[j].astype(BF16)
            n_q = n_q_heads * HEAD_DIM
            q, k, v = _qkv_project(x, mod[0], mod[1], norm1_g[i], w_qkv, attn_q_gain[j], attn_k_gain[j],
                                   rope_tabs, True)
            w_c = w_qkv if ctx_live else w_qkv[:, n_q:]
            ctx_out = _qkv_project(ctx, mod_c[0], mod_c[1], norm1_g[i], w_c, attn_q_gain[j], attn_k_gain[j],
                                   None, ctx_live)
            k_c, v_c = ctx_out[-2], ctx_out[-1]
            a = _attention(q, [(k, v), (k_c, v_c)])
            a_c = _attention(ctx_out[0], [(k_c, v_c)]) if ctx_live else None
            w_out, b_out = attn_w_o[j].astype(BF16), jnp.zeros((d,), F32)
        else:
            hy = (hy_w_in[j].astype(BF16), hy_b_in[j], hy_conv_w[j], hy_conv_b[j], hy_filt_w1[j], hy_filt_b1[j],
                  hy_filt_w2[j], hy_filt_b2[j], hy_filt_w3[j], hy_filt_b3[j], hy_sin_freq[j], hy_d_bias[j])
            a = _hyena_mixer(x, mod[0], mod[1], norm1_g[i], hy, *dft_pair(length))
            a_c = _hyena_mixer(ctx, mod_c[0], mod_c[1], norm1_g[i], hy, *dft_pair(n_ctx)) if ctx_live else None
            w_out, b_out = hy_w_out[j].astype(BF16), hy_b_out[j]

        streams = [list(_post_mixer(a, w_out, b_out, x, mod[2], mod[3], mod[4], norm2_g[i],
                                    moe_router_w[i], moe_router_b[i])) + [mod[5]]]
        if ctx_live:
            streams.append(list(_post_mixer(a_c, w_out, b_out, ctx, mod_c[2], mod_c[3], mod_c[4], norm2_g[i],
                                            moe_router_w[i], moe_router_b[i])) + [mod_c[5]])
        expert_w = (w_gate[i], w_up[i], w_down[i],
                    moe_b_gu[i][:, None, 0::2], moe_b_gu[i][:, None, 1::2], moe_b_down[i][:, None, :])
        outs = _moe(streams, expert_w)
        x = outs[0]
        if ctx_live:
            ctx = outs[1]
    return x
```

```python
import functools
import math

import jax
import jax.numpy as jnp
from jax import lax
from jax.experimental import pallas as pl
from jax.experimental.pallas import tpu as pltpu

F32 = jnp.float32
BF16 = jnp.bfloat16
I32 = jnp.int32

LANES = 128
SUBLANES = 8
VMEM_BYTES = 64 * 2**20
VMEM_CAP = VMEM_BYTES - 8 * 2**20

GRID_W = 64
EPS = 1e-6
HEAD_DIM = 64
N_KV_HEADS = 4
ROPE_THETA = 10000.0
ROPE_PAIRS = HEAD_DIM // 4
HY_SHORT = 3
HY_EMB = 33
HY_BANDS = (HY_EMB - 1) // 2
HY_FAST_DECAY = 0.3
HY_SLOW_DECAY = 1.5
HY_TARGET = 1e-2
N_EXPERTS = 32
TOP_K = 4
SWIGLU_LIMIT = 7.0
SWIGLU_ALPHA = 1.702
N_MIXERS = 2
NEG_BIG = -1e30

ROW_TILE = 512
ATT_Q_TILE = 128
EXPERT_TILE = 512
TOKEN_TILE = 256
CONV_COLS = 256
CONV_CHUNK = 512


def _cparams(n_axes, vmem_estimate):
    limit = int(min(max(vmem_estimate * 5 // 4, 32 * 2**20), VMEM_CAP))
    return pltpu.CompilerParams(dimension_semantics=("arbitrary",) * n_axes, vmem_limit_bytes=limit)


def _split(a):
    hi = a.astype(BF16)
    lo = (a - hi.astype(F32)).astype(BF16)
    return hi, lo


def _dot(a, b):
    return jnp.dot(a, b, preferred_element_type=F32)


def _dot3(a, b):
    a_hi, a_lo = _split(a)
    b_hi, b_lo = _split(b)
    return _dot(a_hi, b_hi) + _dot(a_lo, b_hi) + _dot(a_hi, b_lo)


def _norm_mod(x, g, shift, scale):
    y = x * lax.rsqrt(jnp.mean(x * x, axis=-1, keepdims=True) + EPS)
    return (y * g) * (1.0 + scale) + shift


def _pack(a, b):
    ua = lax.bitcast_convert_type(a.astype(BF16).astype(F32), I32)
    ub = lax.bitcast_convert_type(b.astype(BF16).astype(F32), I32)
    return (ua & jnp.int32(-65536)) | lax.shift_right_logical(ub, jnp.int32(16))


def _unpack(p):
    a = lax.bitcast_convert_type(p & jnp.int32(-65536), F32)
    b = lax.bitcast_convert_type(lax.shift_left(p, jnp.int32(16)), F32)
    return a, b


def _row_chunks(d):
    return d // (2 * LANES)


def _store_token_rows(ref, lead, y):
    m, d = y.shape
    c = _row_chunks(d)
    for j in range(c):
        lo = y[:, LANES * j:LANES * (j + 1)]
        hi = y[:, d // 2 + LANES * j:d // 2 + LANES * (j + 1)]
        ref[lead + (pl.ds(j, m, stride=c), slice(None))] = _pack(lo, hi)


def _load_token_rows(ref, lead, m, c):
    halves = [_unpack(ref[lead + (pl.ds(j, m, stride=c), slice(None))]) for j in range(c)]
    return jnp.concatenate([h[0] for h in halves] + [h[1] for h in halves], axis=1)


def _mod_kernel(c_ref, w_ref, b_ref, o_ref):
    c = c_ref[...]
    a = c * jax.nn.sigmoid(c)
    o_ref[0] = _dot3(a, w_ref[0]) + b_ref[0]


def _modulation(cc, w_mod, b_mod):
    depth, d, n = w_mod.shape
    r = cc.shape[0]
    tn = 1536
    return pl.pallas_call(
        _mod_kernel,
        grid=(depth, n // tn),
        in_specs=[
            pl.BlockSpec((r, d), lambda i, j: (0, 0)),
            pl.BlockSpec((1, d, tn), lambda i, j: (i, 0, j)),
            pl.BlockSpec((1, 1, tn), lambda i, j: (i, 0, j)),
        ],
        out_specs=pl.BlockSpec((1, r, tn), lambda i, j: (i, 0, j)),
        out_shape=jax.ShapeDtypeStruct((depth, r, n), F32),
        compiler_params=_cparams(2, 2 * d * tn * 4 + 3 * d * tn * 2),
        name="modulation",
    )(cc, w_mod, b_mod.reshape(depth, 1, n))


def _qkv_kernel(*refs, n_qk, rope, with_q):
    x_ref, sh_ref, sc_ref, g_ref, w_ref, gain_ref, e_ref, et_ref = refs[:8]
    pos = 8
    if rope:
        cos_ref, sa_ref, sb_ref = refs[pos:pos + 3]
        pos += 3
    outs = refs[pos:]
    n_q = n_qk - N_KV_HEADS * HEAD_DIM
    tm = x_ref.shape[1]
    n_sub = 2 if tm % (2 * SUBLANES) == 0 else 1
    for s in range(n_sub):
        rows = slice(s * (tm // n_sub), (s + 1) * (tm // n_sub))
        h = _norm_mod(x_ref[0, rows, :], g_ref[...], sh_ref[0], sc_ref[0]).astype(BF16)
        acc = _dot(h, w_ref[...])
        qk = acc[:, :n_qk]
        ssq = _dot((qk * qk).astype(BF16), e_ref[...])
        r_hi, r_lo = _split(lax.rsqrt(ssq * (1.0 / HEAD_DIM) + EPS))
        y = qk * (_dot(r_hi, et_ref[...]) + _dot(r_lo, et_ref[...])) * gain_ref[...]
        if rope:
            y = (y * cos_ref[rows, :] + pltpu.roll(y, n_qk - ROPE_PAIRS, 1) * sa_ref[rows, :]
                 + pltpu.roll(y, ROPE_PAIRS, 1) * sb_ref[rows, :])
        kv_outs = outs
        if with_q:
            outs[0][0, rows, :] = (y[:, :n_q] * (HEAD_DIM ** -0.5)).astype(BF16)
            kv_outs = outs[1:]
        kv_outs[0][0, rows, :] = y[:, n_q:].astype(BF16)
        kv_outs[1][0, rows, :] = acc[:, n_qk:].astype(BF16)


def _qkv_project(x, shift, scale, g, w, q_gain, k_gain, rope_tabs, with_q):
    b, t, d = x.shape
    kv = N_KV_HEADS * HEAD_DIM
    nc = w.shape[1]
    n_qk = nc - kv
    n_heads = n_qk // HEAD_DIM
    tm = min(ROW_TILE, t)
    gains = ([q_gain] * (n_heads - N_KV_HEADS)) + [k_gain] * N_KV_HEADS
    gain = jnp.concatenate(gains).reshape(1, n_qk).astype(F32)
    head_of_col = jnp.arange(n_qk) // HEAD_DIM
    e = (head_of_col[:, None] == jnp.arange(LANES)[None, :]).astype(BF16)
    et = e.T
    bm = shift.shape[0]
    mod_map = (lambda i, j: (j, 0, 0)) if bm > 1 else (lambda i, j: (0, 0, 0))
    in_specs = [
        pl.BlockSpec((1, tm, d), lambda i, j: (j, i, 0)),
        pl.BlockSpec((1, 1, d), mod_map),
        pl.BlockSpec((1, 1, d), mod_map),
        pl.BlockSpec((1, d), lambda i, j: (0, 0)),
        pl.BlockSpec((d, nc), lambda i, j: (0, 0)),
        pl.BlockSpec((1, n_qk), lambda i, j: (0, 0)),
        pl.BlockSpec((n_qk, LANES), lambda i, j: (0, 0)),
        pl.BlockSpec((LANES, n_qk), lambda i, j: (0, 0)),
    ]
    args = [x, shift, scale, g.reshape(1, d), w, gain, e, et]
    rope = rope_tabs is not None
    if rope:
        in_specs += [pl.BlockSpec((tm, n_qk), lambda i, j: (i, 0))] * 3
        args += list(rope_tabs)
    out_shape, out_specs = [], []
    widths = ([n_qk - kv] if with_q else []) + [kv, kv]
    for wd in widths:
        out_shape.append(jax.ShapeDtypeStruct((b, t, wd), BF16))
        out_specs.append(pl.BlockSpec((1, tm, wd), lambda i, j: (j, i, 0)))
    est = 2 * tm * d * 4 + 2 * d * nc * 2 + 6 * tm * n_qk * 4 + 8 * tm * nc * 4
    return pl.pallas_call(
        functools.partial(_qkv_kernel, n_qk=n_qk, rope=rope, with_q=with_q),
        grid=(t // tm, b),
        in_specs=in_specs,
        out_specs=out_specs,
        out_shape=out_shape,
        compiler_params=_cparams(2, est),
        name="qkv_project",
    )(*args)


def _rope_tables(n_tokens, n_heads):
    n_rows = n_tokens // GRID_W
    row = jnp.repeat(jnp.arange(n_rows, dtype=F32), GRID_W)
    col = jnp.tile(jnp.arange(GRID_W, dtype=F32), n_rows)
    inv_freq = ROPE_THETA ** (-jnp.arange(ROPE_PAIRS, dtype=F32) / ROPE_PAIRS)
    ar, ac = row[:, None] * inv_freq, col[:, None] * inv_freq
    zero = jnp.zeros_like(ar)
    cos = jnp.concatenate([jnp.cos(ar), jnp.cos(ar), jnp.cos(ac), jnp.cos(ac)], axis=1)
    sin_a = jnp.concatenate([-jnp.sin(ar), zero, -jnp.sin(ac), zero], axis=1)
    sin_b = jnp.concatenate([zero, jnp.sin(ar), zero, jnp.sin(ac)], axis=1)
    return tuple(jnp.tile(tab, (1, n_heads)) for tab in (cos, sin_a, sin_b))


def _attn_kernel(*refs, n_seg, tq):
    q_ref = refs[0]
    kv_refs = refs[1:1 + 2 * n_seg]
    o_ref = refs[1 + 2 * n_seg]
    q_per_kv = q_ref.shape[2] // (N_KV_HEADS * HEAD_DIM)
    for g in range(N_KV_HEADS):
        heads = [g * q_per_kv + j for j in range(q_per_kv)]
        qs = jnp.concatenate([q_ref[0, :, h * HEAD_DIM:(h + 1) * HEAD_DIM] for h in heads], axis=0)
        ksl = slice(g * HEAD_DIM, (g + 1) * HEAD_DIM)
        scores = [lax.dot_general(qs, kv_refs[2 * s][0, :, ksl], (((1,), (1,)), ((), ())),
                                  preferred_element_type=F32) for s in range(n_seg)]
        m = functools.reduce(jnp.maximum, [jnp.max(sc, axis=-1, keepdims=True) for sc in scores])
        probs = [jnp.exp(sc - m) for sc in scores]
        denom = functools.reduce(jnp.add, [jnp.sum(p, axis=-1, keepdims=True) for p in probs])
        o = functools.reduce(jnp.add, [_dot(probs[s].astype(BF16), kv_refs[2 * s + 1][0, :, ksl])
                                       for s in range(n_seg)])
        o = o / denom
        for j, h in enumerate(heads):
            o_ref[0, :, h * HEAD_DIM:(h + 1) * HEAD_DIM] = o[j * tq:(j + 1) * tq].astype(BF16)


def _attention(q, segments):
    b, t, dq = q.shape
    tq = min(ATT_Q_TILE, t)
    in_specs = [pl.BlockSpec((1, tq, dq), lambda i, j: (i, j, 0))]
    args = [q]
    t_keys = 0
    for k, v in segments:
        ts, kv = k.shape[1], k.shape[2]
        in_specs += [pl.BlockSpec((1, ts, kv), lambda i, j: (i, 0, 0))] * 2
        args += [k, v]
        t_keys += ts
    q_rows = tq * dq // (N_KV_HEADS * HEAD_DIM)
    est = 4 * tq * dq * 2 + 8 * t_keys * N_KV_HEADS * HEAD_DIM * 2 + 4 * q_rows * t_keys * 4
    return pl.pallas_call(
        functools.partial(_attn_kernel, n_seg=len(segments), tq=tq),
        grid=(b, t // tq),
        in_specs=in_specs,
        out_specs=pl.BlockSpec((1, tq, dq), lambda i, j: (i, j, 0)),
        out_shape=jax.ShapeDtypeStruct((b, t, dq), BF16),
        compiler_params=_cparams(2, est),
        name="attention",
    )(*args)


def _post_kernel(a_ref, w_ref, bias_ref, x_ref, gate_ref, sh_ref, sc_ref, g_ref, rw_ref, rb_ref, tri_ref, base_ref,
                 xo_ref, hp_ref, te_ref, tr_ref, tg_ref, cnt_ref, run_ref, *, n_sub):
    @pl.when(jnp.logical_and(pl.program_id(0) == 0, pl.program_id(1) == 0))
    def _():
        run_ref[...] = base_ref[...]

    tm = x_ref.shape[1]
    rs = tm // n_sub
    c = hp_ref.shape[1] // tm
    running = run_ref[...]
    for s in range(n_sub):
        rows = slice(s * rs, (s + 1) * rs)
        o = _dot(a_ref[0, rows, :], w_ref[...]) + bias_ref[...]
        x_new = x_ref[0, rows, :] + gate_ref[0] * o
        xo_ref[0, rows, :] = x_new
        h = _norm_mod(x_new, g_ref[...], sh_ref[0], sc_ref[0])
        d = h.shape[1]
        for j in range(c):
            hp_ref[0, pl.ds(s * rs * c + j, rs, stride=c), :] = _pack(
                h[:, LANES * j:LANES * (j + 1)], h[:, d // 2 + LANES * j:d // 2 + LANES * (j + 1)])
        h_hi, h_lo = _split(h)
        wide = _dot(h_hi, rw_ref[...])
        logits = (wide[:, :LANES] + wide[:, LANES:] + _dot(h_lo, rw_ref[:, :LANES])
                  + rb_ref[...])
        lane = lax.broadcasted_iota(I32, logits.shape, 1)
        e_out = jnp.zeros(logits.shape, I32)
        g_out = jnp.zeros(logits.shape, F32)
        picked = jnp.zeros(logits.shape, F32)
        top = None
        total = None
        exps = []
        for k in range(TOP_K):
            m = jnp.max(logits, axis=1, keepdims=True)
            idx = jnp.min(jnp.where(logits == m, lane, LANES), axis=1, keepdims=True)
            hit = lane == idx
            logits = jnp.where(hit, -jnp.inf, logits)
            e_out = jnp.where(lane == k, idx, e_out)
            picked = jnp.where(hit, 1.0, picked)
            top = m if top is None else top
            ex = jnp.exp(m - top)
            exps.append(ex)
            total = ex if total is None else total + ex
        before = running + _dot(tri_ref[...], picked.astype(BF16))
        for k in range(TOP_K):
            g_out = jnp.where(lane == k, exps[k] / total, g_out)
        running = running + jnp.sum(picked, axis=0, keepdims=True)
        te_ref[0, rows, :] = e_out
        tr_ref[0, rows, :] = jnp.where(picked > 0.0, before, 0.0).astype(I32)
        tg_ref[0, rows, :] = g_out
    run_ref[...] = running
    cnt_ref[...] = running


def _post_mixer(a, w, bias, x, gate, shift, scale, g, router_w, router_b, base):
    b, t, d = x.shape
    k_in = a.shape[2]
    c = _row_chunks(d)
    tm = min(ROW_TILE, t)
    n_sub = 1
    rs = tm // n_sub
    bm = gate.shape[0]
    mod_map = (lambda i, j: (i, 0, 0)) if bm > 1 else (lambda i, j: (0, 0, 0))
    rw = jnp.concatenate(_split(jnp.zeros((d, LANES), F32).at[:, :N_EXPERTS].set(router_w)), axis=1)
    rb = jnp.full((1, LANES), NEG_BIG, F32).at[0, :N_EXPERTS].set(router_b)
    tri = (jnp.arange(rs)[:, None] > jnp.arange(rs)[None, :]).astype(BF16)
    row_spec = lambda wd: pl.BlockSpec((1, tm, wd), lambda i, j: (i, j, 0))
    const = lambda shape: pl.BlockSpec(shape, lambda i, j: (0,) * len(shape))
    est = 2 * tm * (k_in * 2 + d * 4 * 2 + d * 2 + 3 * LANES * 4) + 2 * k_in * d * 2 + 8 * tm * d * 4
    return pl.pallas_call(
        functools.partial(_post_kernel, n_sub=n_sub),
        grid=(b, t // tm),
        in_specs=[row_spec(k_in), const((k_in, d)), const((1, d)), row_spec(d),
                  pl.BlockSpec((1, 1, d), mod_map), pl.BlockSpec((1, 1, d), mod_map),
                  pl.BlockSpec((1, 1, d), mod_map), const((1, d)), const((d, 2 * LANES)), const((1, LANES)),
                  const((rs, rs)), const((1, LANES))],
        out_specs=[row_spec(d), pl.BlockSpec((1, tm * c, LANES), lambda i, j: (i, j, 0)),
                   row_spec(LANES), row_spec(LANES), row_spec(LANES), const((1, LANES))],
        out_shape=[jax.ShapeDtypeStruct((b, t, d), F32), jax.ShapeDtypeStruct((b, t * c, LANES), I32),
                   jax.ShapeDtypeStruct((b, t, LANES), I32), jax.ShapeDtypeStruct((b, t, LANES), I32),
                   jax.ShapeDtypeStruct((b, t, LANES), F32), jax.ShapeDtypeStruct((1, LANES), F32)],
        scratch_shapes=[pltpu.VMEM((1, LANES), F32)],
        compiler_params=_cparams(2, est),
        name="post_mixer",
    )(a, w, bias.reshape(1, d), x, gate, shift, scale, g.reshape(1, d), rw, rb, tri, base)


def _split_kernel(w_ref, s_ref, g_ref, u_ref):
    for blk in range(w_ref.shape[2] // (2 * LANES)):
        cols = w_ref[0, :, 2 * LANES * blk:2 * LANES * (blk + 1)].astype(BF16)
        r = _dot(cols, s_ref[...])
        g_ref[0, :, LANES * blk:LANES * (blk + 1)] = r[:, :LANES].astype(BF16)
        u_ref[0, :, LANES * blk:LANES * (blk + 1)] = r[:, LANES:].astype(BF16)


def _split_gate_up(w_gu):
    n, d, f2 = w_gu.shape
    tr = min(ROW_TILE, d)
    col = jnp.arange(2 * LANES)
    src = jnp.where(col < LANES, 2 * col, 2 * (col - LANES) + 1)
    sel = (col[:, None] == src[None, :]).astype(BF16)
    out = pl.BlockSpec((1, tr, f2 // 2), lambda i, j: (i, j, 0))
    return pl.pallas_call(
        _split_kernel,
        grid=(n, d // tr),
        in_specs=[pl.BlockSpec((1, tr, f2), lambda i, j: (i, j, 0)),
                  pl.BlockSpec((2 * LANES, 2 * LANES), lambda i, j: (0, 0))],
        out_specs=[out, out],
        out_shape=[jax.ShapeDtypeStruct((n, d, f2 // 2), BF16)] * 2,
        compiler_params=_cparams(2, 2 * tr * f2 * 4 + 4 * tr * f2 + 4 * tr * f2),
        name="moe_split_gate_up",
    )(w_gu, sel)


def _route(counts, n_rows_max):
    counts = counts[0, :N_EXPERTS].astype(I32)
    padded = (counts + EXPERT_TILE - 1) // EXPERT_TILE * EXPERT_TILE
    pad_end = jnp.cumsum(padded)
    pad_start = pad_end - padded
    tile_start = jnp.arange(n_rows_max // EXPERT_TILE, dtype=I32) * EXPERT_TILE
    tile_e = jnp.minimum(jnp.sum((pad_end[None, :] <= tile_start[:, None]).astype(I32), axis=1), N_EXPERTS - 1)
    n_valid = (pad_end[-1] // EXPERT_TILE).astype(I32).reshape(1)
    return tile_e.astype(I32), n_valid, pad_start.astype(I32), pad_end.astype(I32), padded.astype(I32)


def _pair_dest(top_e, rank, pad_start):
    e = top_e[..., :TOP_K].reshape(-1, TOP_K)
    row = rank[..., :N_EXPERTS].reshape(-1, 1, N_EXPERTS) + pad_start
    onehot = e[..., None] == jnp.arange(N_EXPERTS, dtype=I32)
    return jnp.sum(jnp.where(onehot, row, 0), axis=-1).reshape(-1).astype(I32)


def _dispatch_kernel(pend_ref, padded_ref, dest_ref, h_ref, *rest, tt, c, first):
    if first:
        xs_hbm, zbuf, zsem, row_sem = rest
    else:
        _, xs_hbm, row_sem = rest
    i = pl.program_id(0)

    if first:
        @pl.when(i == 0)
        def _():
            zbuf[...] = jnp.zeros(zbuf.shape, zbuf.dtype)

            def fill(e, carry):
                @pl.when(padded_ref[e] > 0)
                def _():
                    start = pl.multiple_of((pend_ref[e] - EXPERT_TILE) * c, EXPERT_TILE * c)
                    cp = pltpu.make_async_copy(zbuf, xs_hbm.at[pl.ds(start, EXPERT_TILE * c)], zsem)
                    cp.start()
                    cp.wait()
                return carry

            lax.fori_loop(0, N_EXPERTS, fill, 0)

            def fill_tail(t, carry):
                cp = pltpu.make_async_copy(
                    zbuf, xs_hbm.at[pl.ds(pl.multiple_of(t * (EXPERT_TILE * c), EXPERT_TILE * c), EXPERT_TILE * c)],
                    zsem)
                cp.start()
                cp.wait()
                return carry

            lax.fori_loop(pend_ref[N_EXPERTS - 1] // EXPERT_TILE, xs_hbm.shape[0] // (EXPERT_TILE * c),
                          fill_tail, 0)

    def start(r, carry):
        for k in range(TOP_K):
            dst = pl.multiple_of(dest_ref[0, 0, TOP_K * r + k] * c, c)
            pltpu.make_async_copy(h_ref.at[pl.ds(pl.multiple_of(r * c, c), c)], xs_hbm.at[pl.ds(dst, c)],
                                  row_sem).start(priority=k % 2)
        return carry

    lax.fori_loop(0, tt, start, 0, unroll=8)

    def wait(r, carry):
        for k in range(TOP_K):
            pltpu.make_async_copy(h_ref.at[pl.ds(0, c)], xs_hbm.at[pl.ds(0, c)], row_sem).wait()
        return carry

    lax.fori_loop(0, tt, wait, 0, unroll=8)


def _dispatch(dest, h_rows, xs, pad_end, padded, n_rows, c):
    n = h_rows.shape[0] // c
    tt = min(TOKEN_TILE, n)
    first = xs is None
    in_specs = [pl.BlockSpec((1, 1, tt * TOP_K), lambda i, pe, pd: (i, 0, 0), memory_space=pltpu.SMEM),
                pl.BlockSpec((tt * c, LANES), lambda i, pe, pd: (i, 0))]
    args = [pad_end, padded, dest.reshape(n // tt, 1, tt * TOP_K), h_rows]
    scratch = [pltpu.SemaphoreType.DMA]
    if first:
        scratch = [pltpu.VMEM((EXPERT_TILE * c, LANES), I32), pltpu.SemaphoreType.DMA] + scratch
    else:
        in_specs.append(pl.BlockSpec(memory_space=pl.ANY))
        args.append(xs)
    return pl.pallas_call(
        functools.partial(_dispatch_kernel, tt=tt, c=c, first=first),
        grid_spec=pltpu.PrefetchScalarGridSpec(
            num_scalar_prefetch=2, grid=(n // tt,), in_specs=in_specs,
            out_specs=pl.BlockSpec(memory_space=pl.ANY), scratch_shapes=scratch),
        out_shape=jax.ShapeDtypeStruct((n_rows * c, LANES), I32),
        input_output_aliases={} if first else {4: 0},
        compiler_params=pltpu.CompilerParams(dimension_semantics=("arbitrary",), has_side_effects=True),
        name="moe_dispatch",
    )(*args)


def _expert_kernel(te_ref, nv_ref, xs_ref, wg_ref, wu_ref, wd_ref, bg_ref, bu_ref, bd_ref, ys_ref):
    @pl.when(pl.program_id(0) < nv_ref[0])
    def _():
        c = _row_chunks(wg_ref.shape[1])
        xb = _load_token_rows(xs_ref, (), xs_ref.shape[0] // c, c).astype(BF16)
        g = jnp.minimum(_dot(xb, wg_ref[0]) + bg_ref[0], SWIGLU_LIMIT)
        u = jnp.clip(_dot(xb, wu_ref[0]) + bu_ref[0], -SWIGLU_LIMIT, SWIGLU_LIMIT)
        act = (u + 1.0) * g * jax.nn.sigmoid(SWIGLU_ALPHA * g)
        y = _dot(act.astype(BF16), wd_ref[0]) + bd_ref[0]
        _store_token_rows(ys_ref, (), y)

    @pl.when(pl.program_id(0) >= nv_ref[0])
    def _():
        ys_ref[...] = jnp.zeros(ys_ref.shape, ys_ref.dtype)


def _experts(xs, tile_e, n_valid, wg, wu, wd, bg, bu, bd):
    d, f = wg.shape[1:]
    c = _row_chunks(d)
    rows = xs.shape[0] // c
    tm = EXPERT_TILE
    tile_map = lambda i, te, nv: (jnp.minimum(i, nv[0] - 1), 0)
    w_map = lambda i, te, nv: (te[jnp.minimum(i, nv[0] - 1)], 0, 0)
    est = 4 * tm * d * 2 + 2 * 3 * d * f * 2 + 8 * tm * f * 4
    return pl.pallas_call(
        _expert_kernel,
        grid_spec=pltpu.PrefetchScalarGridSpec(
            num_scalar_prefetch=2,
            grid=(rows // tm,),
            in_specs=[pl.BlockSpec((tm * c, LANES), tile_map),
                      pl.BlockSpec((1, d, f), w_map), pl.BlockSpec((1, d, f), w_map),
                      pl.BlockSpec((1, f, d), w_map),
                      pl.BlockSpec((1, 1, f), w_map), pl.BlockSpec((1, 1, f), w_map),
                      pl.BlockSpec((1, 1, d), w_map)],
            out_specs=pl.BlockSpec((tm * c, LANES), lambda i, te, nv: (i, 0)),
        ),
        out_shape=jax.ShapeDtypeStruct((rows * c, LANES), I32),
        compiler_params=_cparams(1, est),
        name="moe_experts",
    )(tile_e, n_valid, xs, wg, wu, wd, bg, bu, bd)


def _combine_kernel(dest_ref, dest_next_ref, ys_hbm, gates_ref, x_ref, gate_ref, o_ref, buf, sems, *, tt, c):
    i = pl.program_id(0)
    slot = i % 2

    def issue(idx_ref, s):
        def body(r, carry):
            for k in range(TOP_K):
                src = pl.multiple_of(idx_ref[0, 0, TOP_K * r + k] * c, c)
                pltpu.make_async_copy(ys_hbm.at[pl.ds(src, c)],
                                      buf.at[s, k, pl.ds(pl.multiple_of(r * c, c), c)],
                                      sems.at[s]).start(priority=k % 2)
            return carry

        lax.fori_loop(0, tt, body, 0, unroll=8)

    @pl.when(i == 0)
    def _():
        issue(dest_ref, 0)

    @pl.when(i + 1 < pl.num_programs(0))
    def _():
        issue(dest_next_ref, 1 - slot)

    pltpu.make_async_copy(buf.at[slot], buf.at[slot], sems.at[slot]).wait()
    gates = gates_ref[...]
    y = None
    for k in range(TOP_K):
        yk = _load_token_rows(buf, (slot, k), tt, c) * gates[:, k:k + 1]
        y = yk if y is None else y + yk
    o_ref[...] = x_ref[...] + gate_ref[0] * y


def _combine(dest, ys, gates, x, gate):
    b, t, d = x.shape
    n = b * t
    tt = min(TOKEN_TILE, t)
    steps = n // tt
    steps_per_batch = t // tt
    bm = gate.shape[0]
    gate_map = (lambda i: (i // steps_per_batch, 0, 0)) if bm > 1 else (lambda i: (0, 0, 0))
    est = 2 * TOP_K * tt * d * 2 + 4 * tt * d * 4 + 2 * tt * LANES * 4 + 6 * tt * d * 4
    dest2 = dest.reshape(steps, 1, tt * TOP_K)
    c = _row_chunks(d)
    out = pl.pallas_call(
        functools.partial(_combine_kernel, tt=tt, c=c),
        grid=(steps,),
        in_specs=[pl.BlockSpec((1, 1, tt * TOP_K), lambda i: (i, 0, 0), memory_space=pltpu.SMEM),
                  pl.BlockSpec((1, 1, tt * TOP_K), lambda i: (jnp.minimum(i + 1, steps - 1), 0, 0),
                               memory_space=pltpu.SMEM),
                  pl.BlockSpec(memory_space=pl.ANY),
                  pl.BlockSpec((tt, LANES), lambda i: (i, 0)),
                  pl.BlockSpec((tt, d), lambda i: (i, 0)),
                  pl.BlockSpec((1, 1, d), gate_map)],
        out_specs=pl.BlockSpec((tt, d), lambda i: (i, 0)),
        out_shape=jax.ShapeDtypeStruct((n, d), F32),
        scratch_shapes=[pltpu.VMEM((2, TOP_K, tt * c, LANES), I32), pltpu.SemaphoreType.DMA((2,))],
        compiler_params=_cparams(1, est),
        name="moe_combine",
    )(dest2, dest2, ys, gates.reshape(n, LANES), x.reshape(n, d), gate)
    return out.reshape(b, t, d)


def _moe(streams, counts, layer, expert_w):
    n_pairs = sum(s[0].shape[0] * s[0].shape[1] for s in streams) * TOP_K
    n_tiles_max = (n_pairs + N_EXPERTS * (EXPERT_TILE - 1)) // EXPERT_TILE
    n_rows_max = n_tiles_max * EXPERT_TILE
    tile_e, n_valid, pad_start, pad_end, padded = _route(counts, n_rows_max)
    c = _row_chunks(streams[0][0].shape[2])
    xs = None
    dests = []
    for x_new, rows, top_e, rank, _, _ in streams:
        n = x_new.shape[0] * x_new.shape[1]
        dests.append(_pair_dest(top_e, rank, pad_start))
        xs = _dispatch(dests[-1], rows.reshape(n * c, LANES), xs, pad_end, padded, n_rows_max, c)
    ys = _experts(xs, tile_e + layer * N_EXPERTS, n_valid, *expert_w)
    return [_combine(dests[i], ys, s[4], s[0], s[5]) for i, s in enumerate(streams)]


def _hyena_in_kernel(xm_ref, xp_ref, xn_ref, sh_ref, sc_ref, g_ref, w_ref, b_ref, cw_ref, cb_ref,
                     x0_ref, z_ref, *, tm, width, chunk):
    i = pl.program_id(1)
    last = pl.num_programs(1) - 1
    xe = jnp.concatenate([xp_ref[0], xm_ref[0], xn_ref[0]], axis=0)
    h = _norm_mod(xe, g_ref[...], sh_ref[0], sc_ref[0]).astype(BF16)
    rows = lax.broadcasted_iota(I32, (tm + 2 * SUBLANES, 1), 0)
    inside = jnp.logical_and(jnp.logical_or(rows >= SUBLANES, i > 0),
                             jnp.logical_or(rows < tm + SUBLANES, i < last))
    n_ext = tm + 2 * SUBLANES

    def conv(col):
        u = _dot(h, w_ref[:, col:col + chunk]) + b_ref[:, col:col + chunk]
        u = jnp.where(inside, u, 0.0)
        prev = pltpu.roll(u, 1, 0)[SUBLANES:SUBLANES + tm]
        nxt = pltpu.roll(u, n_ext - 1, 0)[SUBLANES:SUBLANES + tm]
        cw = cw_ref[:, col:col + chunk]
        return (prev * cw[0:1] + u[SUBLANES:SUBLANES + tm] * cw[1:2] + nxt * cw[2:3]
                + cb_ref[:, col:col + chunk])

    for c in range(width // chunk):
        x0 = conv(c * chunk)
        x1 = conv(width + c * chunk)
        v = conv(2 * width + c * chunk)
        x0_ref[0, :, c * chunk:(c + 1) * chunk] = x0.astype(BF16)
        z_ref[0, :, c * chunk:(c + 1) * chunk] = (v * x1).astype(BF16)


def _hyena_in(x, shift, scale, g, w_in, b_in, conv_w, conv_b):
    b, t, d = x.shape
    n3 = w_in.shape[1]
    width = n3 // 3
    tm = min(ROW_TILE // 2, t)
    blocks8 = tm // SUBLANES
    last8 = t // SUBLANES - 1
    bm = shift.shape[0]
    mod_map = (lambda i, j: (i, 0, 0)) if bm > 1 else (lambda i, j: (0, 0, 0))
    const = lambda shape: pl.BlockSpec(shape, lambda i, j: (0,) * len(shape))
    chunk = 512
    est = 2 * (tm + 16) * d * 4 + 2 * d * n3 * 2 + 4 * tm * width * 2 + 10 * (tm + 16) * chunk * 4 + 4 * tm * d * 4
    return pl.pallas_call(
        functools.partial(_hyena_in_kernel, tm=tm, width=width, chunk=chunk),
        grid=(b, t // tm),
        in_specs=[pl.BlockSpec((1, tm, d), lambda i, j: (i, j, 0)),
                  pl.BlockSpec((1, SUBLANES, d), lambda i, j: (i, jnp.maximum(j * blocks8 - 1, 0), 0)),
                  pl.BlockSpec((1, SUBLANES, d), lambda i, j: (i, jnp.minimum((j + 1) * blocks8, last8), 0)),
                  pl.BlockSpec((1, 1, d), mod_map), pl.BlockSpec((1, 1, d), mod_map),
                  const((1, d)), const((d, n3)), const((1, n3)), const((HY_SHORT, n3)), const((1, n3))],
        out_specs=[pl.BlockSpec((1, tm, width), lambda i, j: (i, j, 0))] * 2,
        out_shape=[jax.ShapeDtypeStruct((b, t, width), BF16)] * 2,
        compiler_params=_cparams(2, est),
        name="hyena_in",
    )(x, x, x, shift, scale, g.reshape(1, d), w_in, b_in.reshape(1, n3), conv_w, conv_b.reshape(1, n3))


def _dft_matrix(length):
    n = 2 * length
    f = jnp.arange(length, dtype=I32)
    ang = ((f[:, None] * f[None, :]) % n).astype(F32) * (2.0 * math.pi / n)
    nyq = jnp.where(f % 2 == 0, 1.0, -1.0).astype(F32)
    msin = (-jnp.sin(ang)).at[0].set(nyq)
    return jnp.concatenate([jnp.cos(ang), msin], axis=0).astype(BF16)


def _filter_kernel(z_ref, w1_ref, b1_ref, fr_ref, w2_ref, b2_ref, w3f_ref, w3b_ref, b3f_ref, b3b_ref,
                   t_ref, dl_ref, wf_ref, a_ref, b_ref, c_ref, d_ref, *, length):
    freq = fr_ref[...]
    a1 = jnp.sin(freq * (_dot3(z_ref[...], w1_ref[...]) + b1_ref[...]))
    a2 = jnp.sin(freq * (_dot3(a1, w2_ref[...]) + b2_ref[...]))
    decay = jnp.exp(-t_ref[...] * dl_ref[...])
    rows = lax.broadcasted_iota(I32, (length, 1), 0)
    first = rows == 0
    hf = (_dot3(a2, w3f_ref[...]) + b3f_ref[...]) * decay
    hb = jnp.where(first, 0.0, (_dot3(a2, w3b_ref[...]) + b3b_ref[...]) * decay)
    inv = 1.0 / (jnp.sum(jnp.abs(hf), axis=0, keepdims=True) + jnp.sum(jnp.abs(hb), axis=0, keepdims=True) + EPS)
    hf = hf * inv
    hb = hb * inv
    p_hi, p_lo = _split(hf + hb)
    q_hi, q_lo = _split(hf - hb)
    wf = wf_ref[...]
    sp = _dot(wf, p_hi) + _dot(wf, p_lo)
    ki = _dot(wf[length:], q_hi) + _dot(wf[length:], q_lo)
    kr = sp[:length]
    knyq = sp[length:length + 1]
    n = 2.0 * length
    sc = jnp.where(first, 1.0 / n, 2.0 / n)
    a_ref[...] = kr * sc
    b_ref[...] = jnp.where(first, 0.0, -ki * sc)
    c_ref[...] = jnp.where(first, 0.0, ki * sc)
    d_ref[...] = jnp.where(first, knyq * (1.0 / n), kr * sc)


def _hyena_filter_spectrum(length, w1, b1, w2, b2, w3, b3, sin_freq, wf):
    width = w3.shape[1] // 2
    ffn = w1.shape[1]
    t = jnp.linspace(0.0, 1.0, length, dtype=F32)[:, None]
    w = 2.0 * math.pi * jnp.arange(length, dtype=F32)[:, None] / length
    f = jnp.linspace(1e-4, HY_BANDS - 1, HY_BANDS, dtype=F32)
    z = jnp.concatenate([t, jnp.cos(f * w), -jnp.sin(f * w)], axis=-1)
    zp = jnp.zeros((length, LANES), F32).at[:, :HY_EMB].set(z)
    pad2 = lambda m: jnp.zeros((LANES, m.shape[1] if m.shape[1] > LANES else LANES), F32).at[:m.shape[0], :m.shape[1]].set(m)
    padv = lambda v: jnp.zeros((1, LANES), F32).at[0, :v.shape[0]].set(v)
    max_decay = math.log(HY_TARGET) / HY_FAST_DECAY
    min_decay = math.log(HY_TARGET) / HY_SLOW_DECAY
    absdelta = jnp.abs(jnp.linspace(min_decay, max_decay, width, dtype=F32)).reshape(1, width)
    tc = LANES
    nct = width // tc
    const = lambda shape: pl.BlockSpec(shape, lambda j: (0,) * len(shape))
    w3p = pad2(w3)
    b3r = b3.reshape(1, 2 * width)
    plane = pl.BlockSpec((length, tc), lambda j: (0, j))
    est = 2 * length * length * 2 + 2 * 4 * length * tc * 4 + 24 * length * tc * 4
    return pl.pallas_call(
        functools.partial(_filter_kernel, length=length),
        grid=(nct,),
        in_specs=[const((length, LANES)), const((LANES, LANES)), const((1, LANES)), const((1, LANES)),
                  const((LANES, LANES)), const((1, LANES)),
                  pl.BlockSpec((LANES, tc), lambda j: (0, j)), pl.BlockSpec((LANES, tc), lambda j: (0, nct + j)),
                  pl.BlockSpec((1, tc), lambda j: (0, j)), pl.BlockSpec((1, tc), lambda j: (0, nct + j)),
                  const((length, 1)), pl.BlockSpec((1, tc), lambda j: (0, j)),
                  pl.BlockSpec((2 * length, length), lambda j: (0, 0), pipeline_mode=pl.Buffered(1))],
        out_specs=[plane] * 4,
        out_shape=[jax.ShapeDtypeStruct((length, width), F32)] * 4,
        compiler_params=_cparams(1, est),
        name="hyena_filter",
    )(zp, pad2(w1), padv(b1), padv(sin_freq), pad2(w2), padv(b2), w3p, w3p, b3r, b3r, t, absdelta, wf)


def _conv_fwd_kernel(z_ref, a_ref, b_ref, c_ref, d_ref, wf_ref, y_ref, *, length, chunk):
    z = z_ref[0]

    def body(s, carry):
        f0 = pl.multiple_of(s * chunk, chunk)
        vr = _dot(wf_ref[pl.ds(f0, chunk), :], z)
        vi = _dot(wf_ref[pl.ds(length + f0, chunk), :], z)
        sl = pl.ds(f0, chunk)
        y_ref[0, sl, :] = (vr * a_ref[sl, :] + vi * b_ref[sl, :]).astype(BF16)
        y_ref[0, pl.ds(length + f0, chunk), :] = (vr * c_ref[sl, :] + vi * d_ref[sl, :]).astype(BF16)
        return carry

    lax.fori_loop(0, length // chunk, body, 0)


def _conv_inv_kernel(y_ref, wi_ref, z_ref, x0_ref, db_ref, o_ref, *, length, chunk):
    yf = y_ref[0]

    def body(s, carry):
        sl = pl.ds(pl.multiple_of(s * chunk, chunk), chunk)
        y = _dot(wi_ref[sl, :], yf)
        o_ref[0, sl, :] = (x0_ref[0, sl, :].astype(F32)
                           * (y + z_ref[0, sl, :].astype(F32) * db_ref[...])).astype(BF16)
        return carry

    lax.fori_loop(0, length // chunk, body, 0)


def _long_conv(z, x0, planes, d_bias, wf, wi):
    b, length, width = z.shape
    tc = CONV_COLS
    chunk = min(CONV_CHUNK, length)
    single = pl.Buffered(1)
    plane = pl.BlockSpec((length, tc), lambda j, i: (0, j), pipeline_mode=single)
    est_f = 2 * length * length * 2 + 4 * length * tc * 4 + 2 * length * tc * 2 + 4 * length * tc * 2 + 8 * chunk * tc * 4
    spec = pl.pallas_call(
        functools.partial(_conv_fwd_kernel, length=length, chunk=chunk),
        grid=(width // tc, b),
        in_specs=[pl.BlockSpec((1, length, tc), lambda j, i: (i, 0, j))] + [plane] * 4
        + [pl.BlockSpec((2 * length, length), lambda j, i: (0, 0), pipeline_mode=single)],
        out_specs=pl.BlockSpec((1, 2 * length, tc), lambda j, i: (i, 0, j)),
        out_shape=jax.ShapeDtypeStruct((b, 2 * length, width), BF16),
        compiler_params=_cparams(2, est_f),
        name="hyena_conv_fwd",
    )(z, *planes, wf)
    col = lambda rows: pl.BlockSpec((1, rows, tc), lambda i, j: (i, 0, j))
    est_i = 2 * length * length * 2 + 4 * length * tc * 2 + 6 * length * tc * 2 + 8 * chunk * tc * 4
    return pl.pallas_call(
        functools.partial(_conv_inv_kernel, length=length, chunk=chunk),
        grid=(b, width // tc),
        in_specs=[col(2 * length),
                  pl.BlockSpec((length, 2 * length), lambda i, j: (0, 0), pipeline_mode=single),
                  col(length), col(length), pl.BlockSpec((1, tc), lambda i, j: (0, j))],
        out_specs=col(length),
        out_shape=jax.ShapeDtypeStruct((b, length, width), BF16),
        compiler_params=_cparams(2, est_i),
        name="hyena_conv_inv",
    )(spec, wi, z, x0, d_bias.reshape(1, width))


def _hyena_mixer(x, shift, scale, g, hy, wf, wi):
    w_in, b_in, conv_w, conv_b, w1, b1, w2, b2, w3, b3, sin_freq, d_bias = hy
    x0, z = _hyena_in(x, shift, scale, g, w_in, b_in, conv_w, conv_b)
    planes = _hyena_filter_spectrum(x.shape[1], w1, b1, w2, b2, w3, b3, sin_freq, wf)
    return _long_conv(z, x0, planes, d_bias, wf, wi)


def kernel(x, c, ctx, c_ctx, w_mod, b_mod, norm1_g, norm2_g, attn_w_qkv, attn_q_gain, attn_k_gain, attn_w_o, hy_w_in, hy_b_in, hy_conv_w, hy_conv_b, hy_filt_w1, hy_filt_b1, hy_filt_w2, hy_filt_b2, hy_filt_w3, hy_filt_b3, hy_sin_freq, hy_d_bias, hy_w_out, hy_b_out, moe_router_w, moe_router_b, moe_w_gu, moe_b_gu, moe_w_down, moe_b_down):
    b, length, d = x.shape
    n_ctx = ctx.shape[1]
    depth = w_mod.shape[0]
    n_q_heads = d // HEAD_DIM

    r_pad = -(-(b + 1) // SUBLANES) * SUBLANES
    cc = jnp.zeros((r_pad, d), F32).at[:b].set(c).at[b].set(c_ctx)
    mod_all = _modulation(cc, w_mod, b_mod).reshape(depth, r_pad, 6, d)

    rope_tabs = _rope_tables(length, n_q_heads + N_KV_HEADS)
    n_exp, _, f2 = moe_w_gu.shape[1:]
    w_gate, w_up = _split_gate_up(moe_w_gu.reshape(depth * n_exp, d, f2))
    b_gu = moe_b_gu.reshape(depth * n_exp, 1, f2)
    expert_w = (w_gate, w_up, moe_w_down.astype(BF16).reshape(depth * n_exp, f2 // 2, d),
                b_gu[..., 0::2], b_gu[..., 1::2], moe_b_down.reshape(depth * n_exp, 1, d))
    dft = {}

    def dft_pair(n):
        if n not in dft:
            wf = _dft_matrix(n)
            dft[n] = (wf, wf.T)
        return dft[n]

    for i in range(depth):
        j = i // N_MIXERS
        is_attn = i % N_MIXERS == 0
        ctx_live = any(m % N_MIXERS == 0 for m in range(i + 1, depth))
        mod = [mod_all[i, :b, s].reshape(b, 1, d) for s in range(6)]
        mod_c = [mod_all[i, b, s].reshape(1, 1, d) for s in range(6)]
        if is_attn:
            w_qkv = attn_w_qkv[j].astype(BF16)
            n_q = n_q_heads * HEAD_DIM
            q, k, v = _qkv_project(x, mod[0], mod[1], norm1_g[i], w_qkv, attn_q_gain[j], attn_k_gain[j],
                                   rope_tabs, True)
            w_c = w_qkv if ctx_live else w_qkv[:, n_q:]
            ctx_out = _qkv_project(ctx, mod_c[0], mod_c[1], norm1_g[i], w_c, attn_q_gain[j], attn_k_gain[j],
                                   None, ctx_live)
            k_c, v_c = ctx_out[-2], ctx_out[-1]
            a = _attention(q, [(k, v), (k_c, v_c)])
            a_c = _attention(ctx_out[0], [(k_c, v_c)]) if ctx_live else None
            w_out, b_out = attn_w_o[j].astype(BF16), jnp.zeros((d,), F32)
        else:
            hy = (hy_w_in[j].astype(BF16), hy_b_in[j], hy_conv_w[j], hy_conv_b[j], hy_filt_w1[j], hy_filt_b1[j],
                  hy_filt_w2[j], hy_filt_b2[j], hy_filt_w3[j], hy_filt_b3[j], hy_sin_freq[j], hy_d_bias[j])
            a = _hyena_mixer(x, mod[0], mod[1], norm1_g[i], hy, *dft_pair(length))
            a_c = _hyena_mixer(ctx, mod_c[0], mod_c[1], norm1_g[i], hy, *dft_pair(n_ctx)) if ctx_live else None
            w_out, b_out = hy_w_out[j].astype(BF16), hy_b_out[j]

        post = _post_mixer(a, w_out, b_out, x, mod[2], mod[3], mod[4], norm2_g[i],
                           moe_router_w[i], moe_router_b[i], jnp.zeros((1, LANES), F32))
        streams = [tuple(post[:5]) + (mod[5],)]
        counts = post[5]
        if ctx_live:
            post = _post_mixer(a_c, w_out, b_out, ctx, mod_c[2], mod_c[3], mod_c[4], norm2_g[i],
                               moe_router_w[i], moe_router_b[i], counts)
            streams.append(tuple(post[:5]) + (mod_c[5],))
            counts = post[5]
        outs = _moe(streams, counts, i, expert_w)
        x = outs[0]
        if ctx_live:
            ctx = outs[1]
    return x
```

```python
import functools
import math

import jax
import jax.numpy as jnp
from jax import lax
from jax.experimental import pallas as pl
from jax.experimental.pallas import tpu as pltpu

F32 = jnp.float32
BF16 = jnp.bfloat16
I32 = jnp.int32

LANES = 128
SUBLANES = 8
VMEM_BYTES = 64 * 2**20
VMEM_CAP = VMEM_BYTES - 8 * 2**20

GRID_W = 64
EPS = 1e-6
HEAD_DIM = 64
N_KV_HEADS = 4
ROPE_THETA = 10000.0
ROPE_PAIRS = HEAD_DIM // 4
HY_SHORT = 3
HY_EMB = 33
HY_BANDS = (HY_EMB - 1) // 2
HY_FAST_DECAY = 0.3
HY_SLOW_DECAY = 1.5
HY_TARGET = 1e-2
N_EXPERTS = 32
TOP_K = 4
SWIGLU_LIMIT = 7.0
SWIGLU_ALPHA = 1.702
N_MIXERS = 2
NEG_BIG = -1e30

ROW_TILE = 512
ATT_Q_TILE = 128
EXPERT_TILE = 512
TOKEN_TILE = 256
CONV_COLS = 256
CONV_CHUNK = 512


def _cparams(n_axes, vmem_estimate):
    limit = int(min(max(vmem_estimate * 5 // 4, 32 * 2**20), VMEM_CAP))
    return pltpu.CompilerParams(dimension_semantics=("arbitrary",) * n_axes, vmem_limit_bytes=limit)


def _split(a):
    hi = a.astype(BF16)
    lo = (a - hi.astype(F32)).astype(BF16)
    return hi, lo


def _dot(a, b):
    return jnp.dot(a, b, preferred_element_type=F32)


def _dot3(a, b):
    a_hi, a_lo = _split(a)
    b_hi, b_lo = _split(b)
    return _dot(a_hi, b_hi) + _dot(a_lo, b_hi) + _dot(a_hi, b_lo)


def _norm_mod(x, g, shift, scale):
    y = x * lax.rsqrt(jnp.mean(x * x, axis=-1, keepdims=True) + EPS)
    return (y * g) * (1.0 + scale) + shift


def _pack(a, b):
    ua = lax.bitcast_convert_type(a.astype(BF16).astype(F32), I32)
    ub = lax.bitcast_convert_type(b.astype(BF16).astype(F32), I32)
    return (ua & jnp.int32(-65536)) | lax.shift_right_logical(ub, jnp.int32(16))


def _unpack(p):
    a = lax.bitcast_convert_type(p & jnp.int32(-65536), F32)
    b = lax.bitcast_convert_type(lax.shift_left(p, jnp.int32(16)), F32)
    return a, b


def _row_chunks(d):
    return d // (2 * LANES)


def _store_token_rows(ref, lead, y):
    m, d = y.shape
    c = _row_chunks(d)
    for j in range(c):
        lo = y[:, LANES * j:LANES * (j + 1)]
        hi = y[:, d // 2 + LANES * j:d // 2 + LANES * (j + 1)]
        ref[lead + (pl.ds(j, m, stride=c), slice(None))] = _pack(lo, hi)


def _load_token_rows(ref, lead, m, c):
    halves = [_unpack(ref[lead + (pl.ds(j, m, stride=c), slice(None))]) for j in range(c)]
    return jnp.concatenate([h[0] for h in halves] + [h[1] for h in halves], axis=1)


def _mod_kernel(c_ref, w_ref, b_ref, o_ref):
    c = c_ref[...]
    a = c * jax.nn.sigmoid(c)
    o_ref[0] = _dot3(a, w_ref[0]) + b_ref[0]


def _modulation(cc, w_mod, b_mod):
    depth, d, n = w_mod.shape
    r = cc.shape[0]
    tn = 1536
    return pl.pallas_call(
        _mod_kernel,
        grid=(depth, n // tn),
        in_specs=[
            pl.BlockSpec((r, d), lambda i, j: (0, 0)),
            pl.BlockSpec((1, d, tn), lambda i, j: (i, 0, j)),
            pl.BlockSpec((1, 1, tn), lambda i, j: (i, 0, j)),
        ],
        out_specs=pl.BlockSpec((1, r, tn), lambda i, j: (i, 0, j)),
        out_shape=jax.ShapeDtypeStruct((depth, r, n), F32),
        compiler_params=_cparams(2, 2 * d * tn * 4 + 3 * d * tn * 2),
        name="modulation",
    )(cc, w_mod, b_mod.reshape(depth, 1, n))


def _qkv_kernel(*refs, n_qk, rope, with_q):
    x_ref, sh_ref, sc_ref, g_ref, w_ref, gain_ref, e_ref, et_ref = refs[:8]
    pos = 8
    if rope:
        cos_ref, sa_ref, sb_ref = refs[pos:pos + 3]
        pos += 3
    outs = refs[pos:]
    n_q = n_qk - N_KV_HEADS * HEAD_DIM
    tm = x_ref.shape[1]
    n_sub = 2 if tm % (2 * SUBLANES) == 0 else 1
    for s in range(n_sub):
        rows = slice(s * (tm // n_sub), (s + 1) * (tm // n_sub))
        h = _norm_mod(x_ref[0, rows, :], g_ref[...], sh_ref[0], sc_ref[0]).astype(BF16)
        acc = _dot(h, w_ref[...])
        qk = acc[:, :n_qk]
        ssq = _dot((qk * qk).astype(BF16), e_ref[...])
        r_hi, r_lo = _split(lax.rsqrt(ssq * (1.0 / HEAD_DIM) + EPS))
        y = qk * (_dot(r_hi, et_ref[...]) + _dot(r_lo, et_ref[...])) * gain_ref[...]
        if rope:
            y = (y * cos_ref[rows, :] + pltpu.roll(y, n_qk - ROPE_PAIRS, 1) * sa_ref[rows, :]
                 + pltpu.roll(y, ROPE_PAIRS, 1) * sb_ref[rows, :])
        kv_outs = outs
        if with_q:
            outs[0][0, rows, :] = (y[:, :n_q] * (HEAD_DIM ** -0.5)).astype(BF16)
            kv_outs = outs[1:]
        kv_outs[0][0, rows, :] = y[:, n_q:].astype(BF16)
        kv_outs[1][0, rows, :] = acc[:, n_qk:].astype(BF16)


def _qkv_project(x, shift, scale, g, w, q_gain, k_gain, rope_tabs, with_q):
    b, t, d = x.shape
    kv = N_KV_HEADS * HEAD_DIM
    nc = w.shape[1]
    n_qk = nc - kv
    n_heads = n_qk // HEAD_DIM
    tm = min(ROW_TILE, t)
    gains = ([q_gain] * (n_heads - N_KV_HEADS)) + [k_gain] * N_KV_HEADS
    gain = jnp.concatenate(gains).reshape(1, n_qk).astype(F32)
    head_of_col = jnp.arange(n_qk) // HEAD_DIM
    e = (head_of_col[:, None] == jnp.arange(LANES)[None, :]).astype(BF16)
    et = e.T
    bm = shift.shape[0]
    mod_map = (lambda i, j: (j, 0, 0)) if bm > 1 else (lambda i, j: (0, 0, 0))
    in_specs = [
        pl.BlockSpec((1, tm, d), lambda i, j: (j, i, 0)),
        pl.BlockSpec((1, 1, d), mod_map),
        pl.BlockSpec((1, 1, d), mod_map),
        pl.BlockSpec((1, d), lambda i, j: (0, 0)),
        pl.BlockSpec((d, nc), lambda i, j: (0, 0)),
        pl.BlockSpec((1, n_qk), lambda i, j: (0, 0)),
        pl.BlockSpec((n_qk, LANES), lambda i, j: (0, 0)),
        pl.BlockSpec((LANES, n_qk), lambda i, j: (0, 0)),
    ]
    args = [x, shift, scale, g.reshape(1, d), w, gain, e, et]
    rope = rope_tabs is not None
    if rope:
        in_specs += [pl.BlockSpec((tm, n_qk), lambda i, j: (i, 0))] * 3
        args += list(rope_tabs)
    out_shape, out_specs = [], []
    widths = ([n_qk - kv] if with_q else []) + [kv, kv]
    for wd in widths:
        out_shape.append(jax.ShapeDtypeStruct((b, t, wd), BF16))
        out_specs.append(pl.BlockSpec((1, tm, wd), lambda i, j: (j, i, 0)))
    est = 2 * tm * d * 4 + 2 * d * nc * 2 + 6 * tm * n_qk * 4 + 8 * tm * nc * 4
    return pl.pallas_call(
        functools.partial(_qkv_kernel, n_qk=n_qk, rope=rope, with_q=with_q),
        grid=(t // tm, b),
        in_specs=in_specs,
        out_specs=out_specs,
        out_shape=out_shape,
        compiler_params=_cparams(2, est),
        name="qkv_project",
    )(*args)


def _rope_tables(n_tokens, n_heads):
    n_rows = n_tokens // GRID_W
    row = jnp.repeat(jnp.arange(n_rows, dtype=F32), GRID_W)
    col = jnp.tile(jnp.arange(GRID_W, dtype=F32), n_rows)
    inv_freq = ROPE_THETA ** (-jnp.arange(ROPE_PAIRS, dtype=F32) / ROPE_PAIRS)
    ar, ac = row[:, None] * inv_freq, col[:, None] * inv_freq
    zero = jnp.zeros_like(ar)
    cos = jnp.concatenate([jnp.cos(ar), jnp.cos(ar), jnp.cos(ac), jnp.cos(ac)], axis=1)
    sin_a = jnp.concatenate([-jnp.sin(ar), zero, -jnp.sin(ac), zero], axis=1)
    sin_b = jnp.concatenate([zero, jnp.sin(ar), zero, jnp.sin(ac)], axis=1)
    return tuple(jnp.tile(tab, (1, n_heads)) for tab in (cos, sin_a, sin_b))


def _attn_kernel(*refs, n_seg, tq):
    q_ref = refs[0]
    kv_refs = refs[1:1 + 2 * n_seg]
    o_ref = refs[1 + 2 * n_seg]
    q_per_kv = q_ref.shape[2] // (N_KV_HEADS * HEAD_DIM)
    for g in range(N_KV_HEADS):
        heads = [g * q_per_kv + j for j in range(q_per_kv)]
        qs = jnp.concatenate([q_ref[0, :, h * HEAD_DIM:(h + 1) * HEAD_DIM] for h in heads], axis=0)
        ksl = slice(g * HEAD_DIM, (g + 1) * HEAD_DIM)
        scores = [lax.dot_general(qs, kv_refs[2 * s][0, :, ksl], (((1,), (1,)), ((), ())),
                                  preferred_element_type=F32) for s in range(n_seg)]
        m = functools.reduce(jnp.maximum, [jnp.max(sc, axis=-1, keepdims=True) for sc in scores])
        probs = [jnp.exp(sc - m) for sc in scores]
        denom = functools.reduce(jnp.add, [jnp.sum(p, axis=-1, keepdims=True) for p in probs])
        o = functools.reduce(jnp.add, [_dot(probs[s].astype(BF16), kv_refs[2 * s + 1][0, :, ksl])
                                       for s in range(n_seg)])
        o = o / denom
        for j, h in enumerate(heads):
            o_ref[0, :, h * HEAD_DIM:(h + 1) * HEAD_DIM] = o[j * tq:(j + 1) * tq].astype(BF16)


def _attention(q, segments):
    b, t, dq = q.shape
    tq = min(ATT_Q_TILE, t)
    in_specs = [pl.BlockSpec((1, tq, dq), lambda i, j: (i, j, 0))]
    args = [q]
    t_keys = 0
    for k, v in segments:
        ts, kv = k.shape[1], k.shape[2]
        in_specs += [pl.BlockSpec((1, ts, kv), lambda i, j: (i, 0, 0))] * 2
        args += [k, v]
        t_keys += ts
    q_rows = tq * dq // (N_KV_HEADS * HEAD_DIM)
    est = 4 * tq * dq * 2 + 8 * t_keys * N_KV_HEADS * HEAD_DIM * 2 + 4 * q_rows * t_keys * 4
    return pl.pallas_call(
        functools.partial(_attn_kernel, n_seg=len(segments), tq=tq),
        grid=(b, t // tq),
        in_specs=in_specs,
        out_specs=pl.BlockSpec((1, tq, dq), lambda i, j: (i, j, 0)),
        out_shape=jax.ShapeDtypeStruct((b, t, dq), BF16),
        compiler_params=_cparams(2, est),
        name="attention",
    )(*args)


def _post_kernel(a_ref, w_ref, bias_ref, x_ref, gate_ref, sh_ref, sc_ref, g_ref, rw_ref, rb_ref, tri_ref, base_ref,
                 xo_ref, hp_ref, te_ref, tr_ref, tg_ref, cnt_ref, run_ref, *, n_sub):
    @pl.when(jnp.logical_and(pl.program_id(0) == 0, pl.program_id(1) == 0))
    def _():
        run_ref[...] = base_ref[...]

    tm = x_ref.shape[1]
    rs = tm // n_sub
    c = hp_ref.shape[1] // tm
    running = run_ref[...]
    for s in range(n_sub):
        rows = slice(s * rs, (s + 1) * rs)
        o = _dot(a_ref[0, rows, :], w_ref[...]) + bias_ref[...]
        x_new = x_ref[0, rows, :] + gate_ref[0] * o
        xo_ref[0, rows, :] = x_new
        h = _norm_mod(x_new, g_ref[...], sh_ref[0], sc_ref[0])
        d = h.shape[1]
        for j in range(c):
            hp_ref[0, pl.ds(s * rs * c + j, rs, stride=c), :] = _pack(
                h[:, LANES * j:LANES * (j + 1)], h[:, d // 2 + LANES * j:d // 2 + LANES * (j + 1)])
        h_hi, h_lo = _split(h)
        wide = _dot(h_hi, rw_ref[...])
        logits = (wide[:, :LANES] + wide[:, LANES:] + _dot(h_lo, rw_ref[:, :LANES])
                  + rb_ref[...])
        lane = lax.broadcasted_iota(I32, logits.shape, 1)
        e_out = jnp.zeros(logits.shape, I32)
        g_out = jnp.zeros(logits.shape, F32)
        picked = jnp.zeros(logits.shape, F32)
        top = None
        total = None
        exps = []
        for k in range(TOP_K):
            m = jnp.max(logits, axis=1, keepdims=True)
            idx = jnp.min(jnp.where(logits == m, lane, LANES), axis=1, keepdims=True)
            hit = lane == idx
            logits = jnp.where(hit, -jnp.inf, logits)
            e_out = jnp.where(lane == k, idx, e_out)
            picked = jnp.where(hit, 1.0, picked)
            top = m if top is None else top
            ex = jnp.exp(m - top)
            exps.append(ex)
            total = ex if total is None else total + ex
        before = running + _dot(tri_ref[...], picked.astype(BF16))
        for k in range(TOP_K):
            g_out = jnp.where(lane == k, exps[k] / total, g_out)
        running = running + jnp.sum(picked, axis=0, keepdims=True)
        te_ref[0, rows, :] = e_out
        tr_ref[0, rows, :] = jnp.where(picked > 0.0, before, 0.0).astype(I32)
        tg_ref[0, rows, :] = g_out
    run_ref[...] = running
    cnt_ref[...] = running


def _post_mixer(a, w, bias, x, gate, shift, scale, g, router_w, router_b, base):
    b, t, d = x.shape
    k_in = a.shape[2]
    c = _row_chunks(d)
    tm = min(ROW_TILE, t)
    n_sub = 1
    rs = tm // n_sub
    bm = gate.shape[0]
    mod_map = (lambda i, j: (i, 0, 0)) if bm > 1 else (lambda i, j: (0, 0, 0))
    rw = jnp.concatenate(_split(jnp.zeros((d, LANES), F32).at[:, :N_EXPERTS].set(router_w)), axis=1)
    rb = jnp.full((1, LANES), NEG_BIG, F32).at[0, :N_EXPERTS].set(router_b)
    tri = (jnp.arange(rs)[:, None] > jnp.arange(rs)[None, :]).astype(BF16)
    row_spec = lambda wd: pl.BlockSpec((1, tm, wd), lambda i, j: (i, j, 0))
    const = lambda shape: pl.BlockSpec(shape, lambda i, j: (0,) * len(shape))
    est = 2 * tm * (k_in * 2 + d * 4 * 2 + d * 2 + 3 * LANES * 4) + 2 * k_in * d * 2 + 8 * tm * d * 4
    return pl.pallas_call(
        functools.partial(_post_kernel, n_sub=n_sub),
        grid=(b, t // tm),
        in_specs=[row_spec(k_in), const((k_in, d)), const((1, d)), row_spec(d),
                  pl.BlockSpec((1, 1, d), mod_map), pl.BlockSpec((1, 1, d), mod_map),
                  pl.BlockSpec((1, 1, d), mod_map), const((1, d)), const((d, 2 * LANES)), const((1, LANES)),
                  const((rs, rs)), const((1, LANES))],
        out_specs=[row_spec(d), pl.BlockSpec((1, tm * c, LANES), lambda i, j: (i, j, 0)),
                   row_spec(LANES), row_spec(LANES), row_spec(LANES), const((1, LANES))],
        out_shape=[jax.ShapeDtypeStruct((b, t, d), F32), jax.ShapeDtypeStruct((b, t * c, LANES), I32),
                   jax.ShapeDtypeStruct((b, t, LANES), I32), jax.ShapeDtypeStruct((b, t, LANES), I32),
                   jax.ShapeDtypeStruct((b, t, LANES), F32), jax.ShapeDtypeStruct((1, LANES), F32)],
        scratch_shapes=[pltpu.VMEM((1, LANES), F32)],
        compiler_params=_cparams(2, est),
        name="post_mixer",
    )(a, w, bias.reshape(1, d), x, gate, shift, scale, g.reshape(1, d), rw, rb, tri, base)


def _route(counts, n_rows_max):
    counts = counts[0, :N_EXPERTS].astype(I32)
    padded = (counts + EXPERT_TILE - 1) // EXPERT_TILE * EXPERT_TILE
    pad_end = jnp.cumsum(padded)
    pad_start = pad_end - padded
    tile_start = jnp.arange(n_rows_max // EXPERT_TILE, dtype=I32) * EXPERT_TILE
    tile_e = jnp.minimum(jnp.sum((pad_end[None, :] <= tile_start[:, None]).astype(I32), axis=1), N_EXPERTS - 1)
    n_valid = (pad_end[-1] // EXPERT_TILE).astype(I32).reshape(1)
    return tile_e.astype(I32), n_valid, pad_start.astype(I32), pad_end.astype(I32), padded.astype(I32)


def _pair_dest(top_e, rank, pad_start):
    e = top_e[..., :TOP_K].reshape(-1, TOP_K)
    row = rank[..., :N_EXPERTS].reshape(-1, 1, N_EXPERTS) + pad_start
    onehot = e[..., None] == jnp.arange(N_EXPERTS, dtype=I32)
    return jnp.sum(jnp.where(onehot, row, 0), axis=-1).reshape(-1).astype(I32)


def _dispatch_kernel(pend_ref, padded_ref, dest_ref, h_ref, *rest, tt, c, first):
    if first:
        xs_hbm, zbuf, zsem, stage, row_sems = rest
    else:
        _, xs_hbm, stage, row_sems = rest
    i = pl.program_id(0)

    if first:
        @pl.when(i == 0)
        def _():
            zbuf[...] = jnp.zeros(zbuf.shape, zbuf.dtype)

            def fill(e, carry):
                @pl.when(padded_ref[e] > 0)
                def _():
                    start = pl.multiple_of((pend_ref[e] - EXPERT_TILE) * c, EXPERT_TILE * c)
                    cp = pltpu.make_async_copy(zbuf, xs_hbm.at[pl.ds(start, EXPERT_TILE * c)], zsem)
                    cp.start()
                    cp.wait()
                return carry

            lax.fori_loop(0, N_EXPERTS, fill, 0)

            def fill_tail(t, carry):
                cp = pltpu.make_async_copy(
                    zbuf, xs_hbm.at[pl.ds(pl.multiple_of(t * (EXPERT_TILE * c), EXPERT_TILE * c), EXPERT_TILE * c)],
                    zsem)
                cp.start()
                cp.wait()
                return carry

            lax.fori_loop(pend_ref[N_EXPERTS - 1] // EXPERT_TILE, xs_hbm.shape[0] // (EXPERT_TILE * c),
                          fill_tail, 0)

    slot = i % 2
    stage[slot] = h_ref[...]

    def start(r, carry):
        for k in range(TOP_K):
            dst = pl.multiple_of(dest_ref[0, 0, TOP_K * r + k] * c, c)
            pltpu.make_async_copy(stage.at[slot, pl.ds(pl.multiple_of(r * c, c), c)], xs_hbm.at[pl.ds(dst, c)],
                                  row_sems.at[slot]).start(priority=k % 2)
        return carry

    lax.fori_loop(0, tt, start, 0, unroll=8)

    def wait_all(s):
        for _ in range(TOP_K):
            pltpu.make_async_copy(stage.at[s], stage.at[s], row_sems.at[s]).wait()

    @pl.when(i > 0)
    def _():
        wait_all(1 - slot)

    @pl.when(i == pl.num_programs(0) - 1)
    def _():
        wait_all(slot)


def _dispatch(dest, h_rows, xs, pad_end, padded, n_rows, c):
    n = h_rows.shape[0] // c
    tt = min(TOKEN_TILE, n)
    first = xs is None
    in_specs = [pl.BlockSpec((1, 1, tt * TOP_K), lambda i, pe, pd: (i, 0, 0), memory_space=pltpu.SMEM),
                pl.BlockSpec((tt * c, LANES), lambda i, pe, pd: (i, 0))]
    args = [pad_end, padded, dest.reshape(n // tt, 1, tt * TOP_K), h_rows]
    scratch = [pltpu.VMEM((2, tt * c, LANES), I32), pltpu.SemaphoreType.DMA((2,))]
    if first:
        scratch = [pltpu.VMEM((EXPERT_TILE * c, LANES), I32), pltpu.SemaphoreType.DMA] + scratch
    else:
        in_specs.append(pl.BlockSpec(memory_space=pl.ANY))
        args.append(xs)
    return pl.pallas_call(
        functools.partial(_dispatch_kernel, tt=tt, c=c, first=first),
        grid_spec=pltpu.PrefetchScalarGridSpec(
            num_scalar_prefetch=2, grid=(n // tt,), in_specs=in_specs,
            out_specs=pl.BlockSpec(memory_space=pl.ANY), scratch_shapes=scratch),
        out_shape=jax.ShapeDtypeStruct((n_rows * c, LANES), I32),
        input_output_aliases={} if first else {4: 0},
        compiler_params=pltpu.CompilerParams(dimension_semantics=("arbitrary",), has_side_effects=True),
        name="moe_dispatch",
    )(*args)


def _expert_kernel(te_ref, nv_ref, xs_ref, wgu_ref, wd_ref, bg_ref, bu_ref, bd_ref, sel_ref, ys_ref,
                   wg_s, wu_s, wd_s):
    i = pl.program_id(0)
    valid = i < nv_ref[0]
    new_expert = jnp.logical_or(i == 0, te_ref[i] != te_ref[jnp.maximum(i - 1, 0)])

    @pl.when(jnp.logical_and(valid, new_expert))
    def _():
        for blk in range(wgu_ref.shape[2] // (2 * LANES)):
            cols = wgu_ref[0, :, 2 * LANES * blk:2 * LANES * (blk + 1)].astype(BF16)
            r = _dot(cols, sel_ref[...])
            wg_s[:, LANES * blk:LANES * (blk + 1)] = r[:, :LANES].astype(BF16)
            wu_s[:, LANES * blk:LANES * (blk + 1)] = r[:, LANES:].astype(BF16)
        wd_s[...] = wd_ref[0].astype(BF16)

    @pl.when(valid)
    def _():
        c = _row_chunks(wg_s.shape[0])
        xb = _load_token_rows(xs_ref, (), xs_ref.shape[0] // c, c).astype(BF16)
        g = jnp.minimum(_dot(xb, wg_s[...]) + bg_ref[0], SWIGLU_LIMIT)
        u = jnp.clip(_dot(xb, wu_s[...]) + bu_ref[0], -SWIGLU_LIMIT, SWIGLU_LIMIT)
        act = (u + 1.0) * g * jax.nn.sigmoid(SWIGLU_ALPHA * g)
        y = _dot(act.astype(BF16), wd_s[...]) + bd_ref[0]
        _store_token_rows(ys_ref, (), y)

    @pl.when(jnp.logical_not(valid))
    def _():
        ys_ref[...] = jnp.zeros(ys_ref.shape, ys_ref.dtype)


def _gate_up_selector():
    col = jnp.arange(2 * LANES)
    src = jnp.where(col < LANES, 2 * col, 2 * (col - LANES) + 1)
    return (col[:, None] == src[None, :]).astype(BF16)


def _experts(xs, tile_e, n_valid, w_gu, w_down, bg, bu, bd):
    d, f2 = w_gu.shape[1:]
    f = f2 // 2
    c = _row_chunks(d)
    rows = xs.shape[0] // c
    tm = EXPERT_TILE
    tile_map = lambda i, te, nv: (jnp.minimum(i, nv[0] - 1), 0)
    w_map = lambda i, te, nv: (te[jnp.minimum(i, nv[0] - 1)], 0, 0)
    est = 4 * tm * d * 2 + 2 * 3 * d * f * 4 + 3 * d * f * 2 + 8 * tm * f * 4
    return pl.pallas_call(
        _expert_kernel,
        grid_spec=pltpu.PrefetchScalarGridSpec(
            num_scalar_prefetch=2,
            grid=(rows // tm,),
            in_specs=[pl.BlockSpec((tm * c, LANES), tile_map),
                      pl.BlockSpec((1, d, f2), w_map), pl.BlockSpec((1, f, d), w_map),
                      pl.BlockSpec((1, 1, f), w_map), pl.BlockSpec((1, 1, f), w_map),
                      pl.BlockSpec((1, 1, d), w_map),
                      pl.BlockSpec((2 * LANES, 2 * LANES), lambda i, te, nv: (0, 0))],
            out_specs=pl.BlockSpec((tm * c, LANES), lambda i, te, nv: (i, 0)),
            scratch_shapes=[pltpu.VMEM((d, f), BF16), pltpu.VMEM((d, f), BF16), pltpu.VMEM((f, d), BF16)],
        ),
        out_shape=jax.ShapeDtypeStruct((rows * c, LANES), I32),
        compiler_params=_cparams(1, est),
        name="moe_experts",
    )(tile_e, n_valid, xs, w_gu, w_down, bg, bu, bd, _gate_up_selector())


def _combine_kernel(dest_ref, dest_next_ref, ys_hbm, gates_ref, x_ref, gate_ref, o_ref, buf, sems, *, tt, c):
    i = pl.program_id(0)
    slot = i % 2

    def issue(idx_ref, s):
        def body(r, carry):
            for k in range(TOP_K):
                src = pl.multiple_of(idx_ref[0, 0, TOP_K * r + k] * c, c)
                pltpu.make_async_copy(ys_hbm.at[pl.ds(src, c)],
                                      buf.at[s, k, pl.ds(pl.multiple_of(r * c, c), c)],
                                      sems.at[s]).start(priority=k % 2)
            return carry

        lax.fori_loop(0, tt, body, 0, unroll=8)

    @pl.when(i == 0)
    def _():
        issue(dest_ref, 0)

    @pl.when(i + 1 < pl.num_programs(0))
    def _():
        issue(dest_next_ref, 1 - slot)

    pltpu.make_async_copy(buf.at[slot], buf.at[slot], sems.at[slot]).wait()
    gates = gates_ref[...]
    y = None
    for k in range(TOP_K):
        yk = _load_token_rows(buf, (slot, k), tt, c) * gates[:, k:k + 1]
        y = yk if y is None else y + yk
    o_ref[...] = x_ref[...] + gate_ref[0] * y


def _combine(dest, ys, gates, x, gate):
    b, t, d = x.shape
    n = b * t
    tt = min(TOKEN_TILE, t)
    steps = n // tt
    steps_per_batch = t // tt
    bm = gate.shape[0]
    gate_map = (lambda i: (i // steps_per_batch, 0, 0)) if bm > 1 else (lambda i: (0, 0, 0))
    est = 2 * TOP_K * tt * d * 2 + 4 * tt * d * 4 + 2 * tt * LANES * 4 + 6 * tt * d * 4
    dest2 = dest.reshape(steps, 1, tt * TOP_K)
    c = _row_chunks(d)
    out = pl.pallas_call(
        functools.partial(_combine_kernel, tt=tt, c=c),
        grid=(steps,),
        in_specs=[pl.BlockSpec((1, 1, tt * TOP_K), lambda i: (i, 0, 0), memory_space=pltpu.SMEM),
                  pl.BlockSpec((1, 1, tt * TOP_K), lambda i: (jnp.minimum(i + 1, steps - 1), 0, 0),
                               memory_space=pltpu.SMEM),
                  pl.BlockSpec(memory_space=pl.ANY),
                  pl.BlockSpec((tt, LANES), lambda i: (i, 0)),
                  pl.BlockSpec((tt, d), lambda i: (i, 0)),
                  pl.BlockSpec((1, 1, d), gate_map)],
        out_specs=pl.BlockSpec((tt, d), lambda i: (i, 0)),
        out_shape=jax.ShapeDtypeStruct((n, d), F32),
        scratch_shapes=[pltpu.VMEM((2, TOP_K, tt * c, LANES), I32), pltpu.SemaphoreType.DMA((2,))],
        compiler_params=_cparams(1, est),
        name="moe_combine",
    )(dest2, dest2, ys, gates.reshape(n, LANES), x.reshape(n, d), gate)
    return out.reshape(b, t, d)


def _moe(streams, counts, layer, expert_w):
    n_pairs = sum(s[0].shape[0] * s[0].shape[1] for s in streams) * TOP_K
    n_tiles_max = (n_pairs + N_EXPERTS * (EXPERT_TILE - 1)) // EXPERT_TILE
    n_rows_max = n_tiles_max * EXPERT_TILE
    tile_e, n_valid, pad_start, pad_end, padded = _route(counts, n_rows_max)
    c = _row_chunks(streams[0][0].shape[2])
    xs = None
    dests = []
    for x_new, rows, top_e, rank, _, _ in streams:
        n = x_new.shape[0] * x_new.shape[1]
        dests.append(_pair_dest(top_e, rank, pad_start))
        xs = _dispatch(dests[-1], rows.reshape(n * c, LANES), xs, pad_end, padded, n_rows_max, c)
    ys = _experts(xs, tile_e + layer * N_EXPERTS, n_valid, *expert_w)
    return [_combine(dests[i], ys, s[4], s[0], s[5]) for i, s in enumerate(streams)]


def _hyena_in_kernel(xm_ref, xp_ref, xn_ref, sh_ref, sc_ref, g_ref, w_ref, b_ref, cw_ref, cb_ref,
                     x0_ref, z_ref, *, tm, width, chunk):
    i = pl.program_id(1)
    last = pl.num_programs(1) - 1
    xe = jnp.concatenate([xp_ref[0], xm_ref[0], xn_ref[0]], axis=0)
    h = _norm_mod(xe, g_ref[...], sh_ref[0], sc_ref[0]).astype(BF16)
    rows = lax.broadcasted_iota(I32, (tm + 2 * SUBLANES, 1), 0)
    inside = jnp.logical_and(jnp.logical_or(rows >= SUBLANES, i > 0),
                             jnp.logical_or(rows < tm + SUBLANES, i < last))
    n_ext = tm + 2 * SUBLANES

    def conv(col):
        u = _dot(h, w_ref[:, col:col + chunk]) + b_ref[:, col:col + chunk]
        u = jnp.where(inside, u, 0.0)
        prev = pltpu.roll(u, 1, 0)[SUBLANES:SUBLANES + tm]
        nxt = pltpu.roll(u, n_ext - 1, 0)[SUBLANES:SUBLANES + tm]
        cw = cw_ref[:, col:col + chunk]
        return (prev * cw[0:1] + u[SUBLANES:SUBLANES + tm] * cw[1:2] + nxt * cw[2:3]
                + cb_ref[:, col:col + chunk])

    for c in range(width // chunk):
        x0 = conv(c * chunk)
        x1 = conv(width + c * chunk)
        v = conv(2 * width + c * chunk)
        x0_ref[0, :, c * chunk:(c + 1) * chunk] = x0.astype(BF16)
        z_ref[0, :, c * chunk:(c + 1) * chunk] = (v * x1).astype(BF16)


def _hyena_in(x, shift, scale, g, w_in, b_in, conv_w, conv_b):
    b, t, d = x.shape
    n3 = w_in.shape[1]
    width = n3 // 3
    tm = min(ROW_TILE, t)
    blocks8 = tm // SUBLANES
    last8 = t // SUBLANES - 1
    bm = shift.shape[0]
    mod_map = (lambda i, j: (i, 0, 0)) if bm > 1 else (lambda i, j: (0, 0, 0))
    const = lambda shape: pl.BlockSpec(shape, lambda i, j: (0,) * len(shape))
    chunk = 512
    est = 2 * (tm + 16) * d * 4 + 2 * d * n3 * 2 + 4 * tm * width * 2 + 10 * (tm + 16) * chunk * 4 + 4 * tm * d * 4
    return pl.pallas_call(
        functools.partial(_hyena_in_kernel, tm=tm, width=width, chunk=chunk),
        grid=(b, t // tm),
        in_specs=[pl.BlockSpec((1, tm, d), lambda i, j: (i, j, 0)),
                  pl.BlockSpec((1, SUBLANES, d), lambda i, j: (i, jnp.maximum(j * blocks8 - 1, 0), 0)),
                  pl.BlockSpec((1, SUBLANES, d), lambda i, j: (i, jnp.minimum((j + 1) * blocks8, last8), 0)),
                  pl.BlockSpec((1, 1, d), mod_map), pl.BlockSpec((1, 1, d), mod_map),
                  const((1, d)), const((d, n3)), const((1, n3)), const((HY_SHORT, n3)), const((1, n3))],
        out_specs=[pl.BlockSpec((1, tm, width), lambda i, j: (i, j, 0))] * 2,
        out_shape=[jax.ShapeDtypeStruct((b, t, width), BF16)] * 2,
        compiler_params=_cparams(2, est),
        name="hyena_in",
    )(x, x, x, shift, scale, g.reshape(1, d), w_in, b_in.reshape(1, n3), conv_w, conv_b.reshape(1, n3))


def _dft_matrix(length):
    n = 2 * length
    f = jnp.arange(length, dtype=I32)
    ang = ((f[:, None] * f[None, :]) % n).astype(F32) * (2.0 * math.pi / n)
    nyq = jnp.where(f % 2 == 0, 1.0, -1.0).astype(F32)
    msin = (-jnp.sin(ang)).at[0].set(nyq)
    return jnp.concatenate([jnp.cos(ang), msin], axis=0).astype(BF16)


def _filter_kernel(z_ref, w1_ref, b1_ref, fr_ref, w2_ref, b2_ref, w3f_ref, w3b_ref, b3f_ref, b3b_ref,
                   t_ref, dl_ref, wf_ref, a_ref, b_ref, c_ref, d_ref, *, length):
    freq = fr_ref[...]
    a1 = jnp.sin(freq * (_dot3(z_ref[...], w1_ref[...]) + b1_ref[...]))
    a2 = jnp.sin(freq * (_dot3(a1, w2_ref[...]) + b2_ref[...]))
    decay = jnp.exp(-t_ref[...] * dl_ref[...])
    rows = lax.broadcasted_iota(I32, (length, 1), 0)
    first = rows == 0
    hf = (_dot3(a2, w3f_ref[...]) + b3f_ref[...]) * decay
    hb = jnp.where(first, 0.0, (_dot3(a2, w3b_ref[...]) + b3b_ref[...]) * decay)
    inv = 1.0 / (jnp.sum(jnp.abs(hf), axis=0, keepdims=True) + jnp.sum(jnp.abs(hb), axis=0, keepdims=True) + EPS)
    hf = hf * inv
    hb = hb * inv
    p_hi, p_lo = _split(hf + hb)
    q_hi, q_lo = _split(hf - hb)
    wf = wf_ref[...]
    sp = _dot(wf, p_hi) + _dot(wf, p_lo)
    ki = _dot(wf[length:], q_hi) + _dot(wf[length:], q_lo)
    kr = sp[:length]
    knyq = sp[length:length + 1]
    n = 2.0 * length
    sc = jnp.where(first, 1.0 / n, 2.0 / n)
    a_ref[...] = kr * sc
    b_ref[...] = jnp.where(first, 0.0, -ki * sc)
    c_ref[...] = jnp.where(first, 0.0, ki * sc)
    d_ref[...] = jnp.where(first, knyq * (1.0 / n), kr * sc)


def _hyena_filter_spectrum(length, w1, b1, w2, b2, w3, b3, sin_freq, wf):
    width = w3.shape[1] // 2
    ffn = w1.shape[1]
    t = jnp.linspace(0.0, 1.0, length, dtype=F32)[:, None]
    w = 2.0 * math.pi * jnp.arange(length, dtype=F32)[:, None] / length
    f = jnp.linspace(1e-4, HY_BANDS - 1, HY_BANDS, dtype=F32)
    z = jnp.concatenate([t, jnp.cos(f * w), -jnp.sin(f * w)], axis=-1)
    zp = jnp.zeros((length, LANES), F32).at[:, :HY_EMB].set(z)
    pad2 = lambda m: jnp.zeros((LANES, m.shape[1] if m.shape[1] > LANES else LANES), F32).at[:m.shape[0], :m.shape[1]].set(m)
    padv = lambda v: jnp.zeros((1, LANES), F32).at[0, :v.shape[0]].set(v)
    max_decay = math.log(HY_TARGET) / HY_FAST_DECAY
    min_decay = math.log(HY_TARGET) / HY_SLOW_DECAY
    absdelta = jnp.abs(jnp.linspace(min_decay, max_decay, width, dtype=F32)).reshape(1, width)
    tc = LANES
    nct = width // tc
    const = lambda shape: pl.BlockSpec(shape, lambda j: (0,) * len(shape))
    w3p = pad2(w3)
    b3r = b3.reshape(1, 2 * width)
    plane = pl.BlockSpec((length, tc), lambda j: (0, j))
    est = 2 * length * length * 2 + 2 * 4 * length * tc * 4 + 24 * length * tc * 4
    return pl.pallas_call(
        functools.partial(_filter_kernel, length=length),
        grid=(nct,),
        in_specs=[const((length, LANES)), const((LANES, LANES)), const((1, LANES)), const((1, LANES)),
                  const((LANES, LANES)), const((1, LANES)),
                  pl.BlockSpec((LANES, tc), lambda j: (0, j)), pl.BlockSpec((LANES, tc), lambda j: (0, nct + j)),
                  pl.BlockSpec((1, tc), lambda j: (0, j)), pl.BlockSpec((1, tc), lambda j: (0, nct + j)),
                  const((length, 1)), pl.BlockSpec((1, tc), lambda j: (0, j)),
                  pl.BlockSpec((2 * length, length), lambda j: (0, 0), pipeline_mode=pl.Buffered(1))],
        out_specs=[plane] * 4,
        out_shape=[jax.ShapeDtypeStruct((length, width), F32)] * 4,
        compiler_params=_cparams(1, est),
        name="hyena_filter",
    )(zp, pad2(w1), padv(b1), padv(sin_freq), pad2(w2), padv(b2), w3p, w3p, b3r, b3r, t, absdelta, wf)


def _conv_fwd_kernel(z_ref, a_ref, b_ref, c_ref, d_ref, wf_ref, y_ref, *, length, chunk):
    z = z_ref[0]

    def body(s, carry):
        f0 = pl.multiple_of(s * chunk, chunk)
        vr = _dot(wf_ref[pl.ds(f0, chunk), :], z)
        vi = _dot(wf_ref[pl.ds(length + f0, chunk), :], z)
        sl = pl.ds(f0, chunk)
        y_ref[0, sl, :] = (vr * a_ref[sl, :] + vi * b_ref[sl, :]).astype(BF16)
        y_ref[0, pl.ds(length + f0, chunk), :] = (vr * c_ref[sl, :] + vi * d_ref[sl, :]).astype(BF16)
        return carry

    lax.fori_loop(0, length // chunk, body, 0)


def _conv_inv_kernel(y_ref, wi_ref, z_ref, x0_ref, db_ref, o_ref, *, length, chunk):
    yf = y_ref[0]

    def body(s, carry):
        sl = pl.ds(pl.multiple_of(s * chunk, chunk), chunk)
        y = _dot(wi_ref[sl, :], yf)
        o_ref[0, sl, :] = (x0_ref[0, sl, :].astype(F32)
                           * (y + z_ref[0, sl, :].astype(F32) * db_ref[...])).astype(BF16)
        return carry

    lax.fori_loop(0, length // chunk, body, 0)


def _long_conv(z, x0, planes, d_bias, wf, wi):
    b, length, width = z.shape
    tc = CONV_COLS
    chunk = min(CONV_CHUNK, length)
    single = pl.Buffered(1)
    plane = pl.BlockSpec((length, tc), lambda j, i: (0, j), pipeline_mode=single)
    est_f = 2 * length * length * 2 + 4 * length * tc * 4 + 2 * length * tc * 2 + 4 * length * tc * 2 + 8 * chunk * tc * 4
    spec = pl.pallas_call(
        functools.partial(_conv_fwd_kernel, length=length, chunk=chunk),
        grid=(width // tc, b),
        in_specs=[pl.BlockSpec((1, length, tc), lambda j, i: (i, 0, j))] + [plane] * 4
        + [pl.BlockSpec((2 * length, length), lambda j, i: (0, 0), pipeline_mode=single)],
        out_specs=pl.BlockSpec((1, 2 * length, tc), lambda j, i: (i, 0, j)),
        out_shape=jax.ShapeDtypeStruct((b, 2 * length, width), BF16),
        compiler_params=_cparams(2, est_f),
        name="hyena_conv_fwd",
    )(z, *planes, wf)
    col = lambda rows: pl.BlockSpec((1, rows, tc), lambda i, j: (i, 0, j))
    est_i = 2 * length * length * 2 + 4 * length * tc * 2 + 6 * length * tc * 2 + 8 * chunk * tc * 4
    return pl.pallas_call(
        functools.partial(_conv_inv_kernel, length=length, chunk=chunk),
        grid=(b, width // tc),
        in_specs=[col(2 * length),
                  pl.BlockSpec((length, 2 * length), lambda i, j: (0, 0), pipeline_mode=single),
                  col(length), col(length), pl.BlockSpec((1, tc), lambda i, j: (0, j))],
        out_specs=col(length),
        out_shape=jax.ShapeDtypeStruct((b, length, width), BF16),
        compiler_params=_cparams(2, est_i),
        name="hyena_conv_inv",
    )(spec, wi, z, x0, d_bias.reshape(1, width))


def _hyena_mixer(x, shift, scale, g, hy, wf, wi):
    w_in, b_in, conv_w, conv_b, w1, b1, w2, b2, w3, b3, sin_freq, d_bias = hy
    x0, z = _hyena_in(x, shift, scale, g, w_in, b_in, conv_w, conv_b)
    planes = _hyena_filter_spectrum(x.shape[1], w1, b1, w2, b2, w3, b3, sin_freq, wf)
    return _long_conv(z, x0, planes, d_bias, wf, wi)


def kernel(x, c, ctx, c_ctx, w_mod, b_mod, norm1_g, norm2_g, attn_w_qkv, attn_q_gain, attn_k_gain, attn_w_o, hy_w_in, hy_b_in, hy_conv_w, hy_conv_b, hy_filt_w1, hy_filt_b1, hy_filt_w2, hy_filt_b2, hy_filt_w3, hy_filt_b3, hy_sin_freq, hy_d_bias, hy_w_out, hy_b_out, moe_router_w, moe_router_b, moe_w_gu, moe_b_gu, moe_w_down, moe_b_down):
    b, length, d = x.shape
    n_ctx = ctx.shape[1]
    depth = w_mod.shape[0]
    n_q_heads = d // HEAD_DIM

    r_pad = -(-(b + 1) // SUBLANES) * SUBLANES
    cc = jnp.zeros((r_pad, d), F32).at[:b].set(c).at[b].set(c_ctx)
    mod_all = _modulation(cc, w_mod, b_mod).reshape(depth, r_pad, 6, d)

    rope_tabs = _rope_tables(length, n_q_heads + N_KV_HEADS)
    n_exp, _, f2 = moe_w_gu.shape[1:]
    b_gu = moe_b_gu.reshape(depth * n_exp, 1, f2)
    expert_w = (moe_w_gu.reshape(depth * n_exp, d, f2), moe_w_down.reshape(depth * n_exp, f2 // 2, d),
                b_gu[..., 0::2], b_gu[..., 1::2], moe_b_down.reshape(depth * n_exp, 1, d))
    dft = {}

    def dft_pair(n):
        if n not in dft:
            wf = _dft_matrix(n)
            dft[n] = (wf, wf.T)
        return dft[n]

    for i in range(depth):
        j = i // N_MIXERS
        is_attn = i % N_MIXERS == 0
        ctx_live = any(m % N_MIXERS == 0 for m in range(i + 1, depth))
        mod = [mod_all[i, :b, s].reshape(b, 1, d) for s in range(6)]
        mod_c = [mod_all[i, b, s].reshape(1, 1, d) for s in range(6)]
        if is_attn:
            w_qkv = attn_w_qkv[j].astype(BF16)
            n_q = n_q_heads * HEAD_DIM
            q, k, v = _qkv_project(x, mod[0], mod[1], norm1_g[i], w_qkv, attn_q_gain[j], attn_k_gain[j],
                                   rope_tabs, True)
            w_c = w_qkv if ctx_live else w_qkv[:, n_q:]
            ctx_out = _qkv_project(ctx, mod_c[0], mod_c[1], norm1_g[i], w_c, attn_q_gain[j], attn_k_gain[j],
                                   None, ctx_live)
            k_c, v_c = ctx_out[-2], ctx_out[-1]
            a = _attention(q, [(k, v), (k_c, v_c)])
            a_c = _attention(ctx_out[0], [(k_c, v_c)]) if ctx_live else None
            w_out, b_out = attn_w_o[j].astype(BF16), jnp.zeros((d,), F32)
        else:
            hy = (hy_w_in[j].astype(BF16), hy_b_in[j], hy_conv_w[j], hy_conv_b[j], hy_filt_w1[j], hy_filt_b1[j],
                  hy_filt_w2[j], hy_filt_b2[j], hy_filt_w3[j], hy_filt_b3[j], hy_sin_freq[j], hy_d_bias[j])
            a = _hyena_mixer(x, mod[0], mod[1], norm1_g[i], hy, *dft_pair(length))
            a_c = _hyena_mixer(ctx, mod_c[0], mod_c[1], norm1_g[i], hy, *dft_pair(n_ctx)) if ctx_live else None
            w_out, b_out = hy_w_out[j].astype(BF16), hy_b_out[j]

        post = _post_mixer(a, w_out, b_out, x, mod[2], mod[3], mod[4], norm2_g[i],
                           moe_router_w[i], moe_router_b[i], jnp.zeros((1, LANES), F32))
        streams = [tuple(post[:5]) + (mod[5],)]
        counts = post[5]
        if ctx_live:
            post = _post_mixer(a_c, w_out, b_out, ctx, mod_c[2], mod_c[3], mod_c[4], norm2_g[i],
                               moe_router_w[i], moe_router_b[i], counts)
            streams.append(tuple(post[:5]) + (mod_c[5],))
            counts = post[5]
        outs = _moe(streams, counts, i, expert_w)
        x = outs[0]
        if ctx_live:
            ctx = outs[1]
    return x
```

```python
import functools
import math

import jax
import jax.numpy as jnp
from jax import lax
from jax.experimental import pallas as pl
from jax.experimental.pallas import tpu as pltpu

F32 = jnp.float32
BF16 = jnp.bfloat16
I32 = jnp.int32

LANES = 128
SUBLANES = 8
VMEM_BYTES = 64 * 2**20
VMEM_CAP = VMEM_BYTES - 8 * 2**20

GRID_W = 64
EPS = 1e-6
HEAD_DIM = 64
N_KV_HEADS = 4
ROPE_THETA = 10000.0
ROPE_PAIRS = HEAD_DIM // 4
HY_SHORT = 3
HY_EMB = 33
HY_BANDS = (HY_EMB - 1) // 2
HY_FAST_DECAY = 0.3
HY_SLOW_DECAY = 1.5
HY_TARGET = 1e-2
N_EXPERTS = 32
TOP_K = 4
SWIGLU_LIMIT = 7.0
SWIGLU_ALPHA = 1.702
N_MIXERS = 2
NEG_BIG = -1e30

ROW_TILE = 512
ATT_Q_TILE = 256
EXPERT_TILE = 512
TOKEN_TILE = 256
CONV_COLS = 256
CONV_CHUNK = 512


def _cparams(n_axes, vmem_estimate):
    limit = int(min(max(vmem_estimate * 5 // 4, 32 * 2**20), VMEM_CAP))
    return pltpu.CompilerParams(dimension_semantics=("arbitrary",) * n_axes, vmem_limit_bytes=limit)


def _split(a):
    hi = a.astype(BF16)
    lo = (a - hi.astype(F32)).astype(BF16)
    return hi, lo


def _dot(a, b):
    return jnp.dot(a, b, preferred_element_type=F32)


def _dot3(a, b):
    a_hi, a_lo = _split(a)
    b_hi, b_lo = _split(b)
    return _dot(a_hi, b_hi) + _dot(a_lo, b_hi) + _dot(a_hi, b_lo)


def _norm_mod(x, g, shift, scale):
    y = x * lax.rsqrt(jnp.mean(x * x, axis=-1, keepdims=True) + EPS)
    return (y * g) * (1.0 + scale) + shift


def _pack(a, b):
    ua = lax.bitcast_convert_type(a.astype(BF16).astype(F32), I32)
    ub = lax.bitcast_convert_type(b.astype(BF16).astype(F32), I32)
    return (ua & jnp.int32(-65536)) | lax.shift_right_logical(ub, jnp.int32(16))


def _unpack(p):
    a = lax.bitcast_convert_type(p & jnp.int32(-65536), F32)
    b = lax.bitcast_convert_type(lax.shift_left(p, jnp.int32(16)), F32)
    return a, b


def _row_chunks(d):
    return d // (2 * LANES)


def _store_token_rows(ref, lead, y):
    m, d = y.shape
    c = _row_chunks(d)
    for j in range(c):
        lo = y[:, LANES * j:LANES * (j + 1)]
        hi = y[:, d // 2 + LANES * j:d // 2 + LANES * (j + 1)]
        ref[lead + (pl.ds(j, m, stride=c), slice(None))] = _pack(lo, hi)


def _load_token_rows(ref, lead, m, c):
    halves = [_unpack(ref[lead + (pl.ds(j, m, stride=c), slice(None))]) for j in range(c)]
    return jnp.concatenate([h[0] for h in halves] + [h[1] for h in halves], axis=1)


def _mod_kernel(c_ref, w_ref, b_ref, o_ref):
    c = c_ref[...]
    a = c * jax.nn.sigmoid(c)
    o_ref[0] = _dot3(a, w_ref[0]) + b_ref[0]


def _modulation(cc, w_mod, b_mod):
    depth, d, n = w_mod.shape
    r = cc.shape[0]
    tn = 1536
    return pl.pallas_call(
        _mod_kernel,
        grid=(depth, n // tn),
        in_specs=[
            pl.BlockSpec((r, d), lambda i, j: (0, 0)),
            pl.BlockSpec((1, d, tn), lambda i, j: (i, 0, j)),
            pl.BlockSpec((1, 1, tn), lambda i, j: (i, 0, j)),
        ],
        out_specs=pl.BlockSpec((1, r, tn), lambda i, j: (i, 0, j)),
        out_shape=jax.ShapeDtypeStruct((depth, r, n), F32),
        compiler_params=_cparams(2, 2 * d * tn * 4 + 3 * d * tn * 2),
        name="modulation",
    )(cc, w_mod, b_mod.reshape(depth, 1, n))


def _qkv_kernel(*refs, n_qk, rope, with_q):
    x_ref, sh_ref, sc_ref, g_ref, w_ref, gain_ref, e_ref, et_ref = refs[:8]
    pos = 8
    if rope:
        cos_ref, sa_ref, sb_ref = refs[pos:pos + 3]
        pos += 3
    outs = refs[pos:]
    n_q = n_qk - N_KV_HEADS * HEAD_DIM
    tm = x_ref.shape[1]
    n_sub = 2 if tm % (2 * SUBLANES) == 0 else 1
    for s in range(n_sub):
        rows = slice(s * (tm // n_sub), (s + 1) * (tm // n_sub))
        h = _norm_mod(x_ref[0, rows, :], g_ref[...], sh_ref[0], sc_ref[0]).astype(BF16)
        acc = _dot(h, w_ref[...])
        qk = acc[:, :n_qk]
        ssq = _dot((qk * qk).astype(BF16), e_ref[...])
        r_hi, r_lo = _split(lax.rsqrt(ssq * (1.0 / HEAD_DIM) + EPS))
        y = qk * (_dot(r_hi, et_ref[...]) + _dot(r_lo, et_ref[...])) * gain_ref[...]
        if rope:
            y = (y * cos_ref[rows, :] + pltpu.roll(y, n_qk - ROPE_PAIRS, 1) * sa_ref[rows, :]
                 + pltpu.roll(y, ROPE_PAIRS, 1) * sb_ref[rows, :])
        kv_outs = outs
        if with_q:
            outs[0][0, rows, :] = (y[:, :n_q] * (HEAD_DIM ** -0.5)).astype(BF16)
            kv_outs = outs[1:]
        kv_outs[0][0, rows, :] = y[:, n_q:].astype(BF16)
        kv_outs[1][0, rows, :] = acc[:, n_qk:].astype(BF16)


def _qkv_project(x, shift, scale, g, w, q_gain, k_gain, rope_tabs, with_q):
    b, t, d = x.shape
    kv = N_KV_HEADS * HEAD_DIM
    nc = w.shape[1]
    n_qk = nc - kv
    n_heads = n_qk // HEAD_DIM
    tm = min(ROW_TILE, t)
    gains = ([q_gain] * (n_heads - N_KV_HEADS)) + [k_gain] * N_KV_HEADS
    gain = jnp.concatenate(gains).reshape(1, n_qk).astype(F32)
    head_of_col = jnp.arange(n_qk) // HEAD_DIM
    e = (head_of_col[:, None] == jnp.arange(LANES)[None, :]).astype(BF16)
    et = e.T
    bm = shift.shape[0]
    mod_map = (lambda i, j: (j, 0, 0)) if bm > 1 else (lambda i, j: (0, 0, 0))
    in_specs = [
        pl.BlockSpec((1, tm, d), lambda i, j: (j, i, 0)),
        pl.BlockSpec((1, 1, d), mod_map),
        pl.BlockSpec((1, 1, d), mod_map),
        pl.BlockSpec((1, d), lambda i, j: (0, 0)),
        pl.BlockSpec((d, nc), lambda i, j: (0, 0)),
        pl.BlockSpec((1, n_qk), lambda i, j: (0, 0)),
        pl.BlockSpec((n_qk, LANES), lambda i, j: (0, 0)),
        pl.BlockSpec((LANES, n_qk), lambda i, j: (0, 0)),
    ]
    args = [x, shift, scale, g.reshape(1, d), w, gain, e, et]
    rope = rope_tabs is not None
    if rope:
        in_specs += [pl.BlockSpec((tm, n_qk), lambda i, j: (i, 0))] * 3
        args += list(rope_tabs)
    out_shape, out_specs = [], []
    widths = ([n_qk - kv] if with_q else []) + [kv, kv]
    for wd in widths:
        out_shape.append(jax.ShapeDtypeStruct((b, t, wd), BF16))
        out_specs.append(pl.BlockSpec((1, tm, wd), lambda i, j: (j, i, 0)))
    est = 2 * tm * d * 4 + 2 * d * nc * 2 + 6 * tm * n_qk * 4 + 8 * tm * nc * 4
    return pl.pallas_call(
        functools.partial(_qkv_kernel, n_qk=n_qk, rope=rope, with_q=with_q),
        grid=(t // tm, b),
        in_specs=in_specs,
        out_specs=out_specs,
        out_shape=out_shape,
        compiler_params=_cparams(2, est),
        name="qkv_project",
    )(*args)


def _rope_tables(n_tokens, n_heads):
    n_rows = n_tokens // GRID_W
    row = jnp.repeat(jnp.arange(n_rows, dtype=F32), GRID_W)
    col = jnp.tile(jnp.arange(GRID_W, dtype=F32), n_rows)
    inv_freq = ROPE_THETA ** (-jnp.arange(ROPE_PAIRS, dtype=F32) / ROPE_PAIRS)
    ar, ac = row[:, None] * inv_freq, col[:, None] * inv_freq
    zero = jnp.zeros_like(ar)
    cos = jnp.concatenate([jnp.cos(ar), jnp.cos(ar), jnp.cos(ac), jnp.cos(ac)], axis=1)
    sin_a = jnp.concatenate([-jnp.sin(ar), zero, -jnp.sin(ac), zero], axis=1)
    sin_b = jnp.concatenate([zero, jnp.sin(ar), zero, jnp.sin(ac)], axis=1)
    return tuple(jnp.tile(tab, (1, n_heads)) for tab in (cos, sin_a, sin_b))


def _attn_kernel(*refs, n_seg, tq):
    q_ref = refs[0]
    kv_refs = refs[1:1 + 2 * n_seg]
    o_ref = refs[1 + 2 * n_seg]
    q_per_kv = q_ref.shape[2] // (N_KV_HEADS * HEAD_DIM)
    for g in range(N_KV_HEADS):
        heads = [g * q_per_kv + j for j in range(q_per_kv)]
        qs = jnp.concatenate([q_ref[0, :, h * HEAD_DIM:(h + 1) * HEAD_DIM] for h in heads], axis=0)
        ksl = slice(g * HEAD_DIM, (g + 1) * HEAD_DIM)
        scores = [lax.dot_general(qs, kv_refs[2 * s][0, :, ksl], (((1,), (1,)), ((), ())),
                                  preferred_element_type=F32) for s in range(n_seg)]
        m = functools.reduce(jnp.maximum, [jnp.max(sc, axis=-1, keepdims=True) for sc in scores])
        probs = [jnp.exp(sc - m) for sc in scores]
        denom = functools.reduce(jnp.add, [jnp.sum(p, axis=-1, keepdims=True) for p in probs])
        o = functools.reduce(jnp.add, [_dot(probs[s].astype(BF16), kv_refs[2 * s + 1][0, :, ksl])
                                       for s in range(n_seg)])
        o = o / denom
        for j, h in enumerate(heads):
            o_ref[0, :, h * HEAD_DIM:(h + 1) * HEAD_DIM] = o[j * tq:(j + 1) * tq].astype(BF16)


def _attention(q, segments):
    b, t, dq = q.shape
    tq = min(ATT_Q_TILE, t)
    in_specs = [pl.BlockSpec((1, tq, dq), lambda i, j: (i, j, 0))]
    args = [q]
    t_keys = 0
    for k, v in segments:
        ts, kv = k.shape[1], k.shape[2]
        in_specs += [pl.BlockSpec((1, ts, kv), lambda i, j: (i, 0, 0))] * 2
        args += [k, v]
        t_keys += ts
    q_rows = tq * dq // (N_KV_HEADS * HEAD_DIM)
    est = 4 * tq * dq * 2 + 8 * t_keys * N_KV_HEADS * HEAD_DIM * 2 + 4 * q_rows * t_keys * 4
    return pl.pallas_call(
        functools.partial(_attn_kernel, n_seg=len(segments), tq=tq),
        grid=(b, t // tq),
        in_specs=in_specs,
        out_specs=pl.BlockSpec((1, tq, dq), lambda i, j: (i, j, 0)),
        out_shape=jax.ShapeDtypeStruct((b, t, dq), BF16),
        compiler_params=_cparams(2, est),
        name="attention",
    )(*args)


def _post_kernel(a_ref, w_ref, bias_ref, x_ref, gate_ref, sh_ref, sc_ref, g_ref, rw_ref, rb_ref, tri_ref, base_ref,
                 xo_ref, hp_ref, te_ref, tr_ref, tg_ref, cnt_ref, run_ref, *, n_sub):
    @pl.when(jnp.logical_and(pl.program_id(0) == 0, pl.program_id(1) == 0))
    def _():
        run_ref[...] = base_ref[...]

    tm = x_ref.shape[1]
    rs = tm // n_sub
    c = hp_ref.shape[1] // tm
    running = run_ref[...]
    for s in range(n_sub):
        rows = slice(s * rs, (s + 1) * rs)
        o = _dot(a_ref[0, rows, :], w_ref[...]) + bias_ref[...]
        x_new = x_ref[0, rows, :] + gate_ref[0] * o
        xo_ref[0, rows, :] = x_new
        h = _norm_mod(x_new, g_ref[...], sh_ref[0], sc_ref[0])
        d = h.shape[1]
        for j in range(c):
            hp_ref[0, pl.ds(s * rs * c + j, rs, stride=c), :] = _pack(
                h[:, LANES * j:LANES * (j + 1)], h[:, d // 2 + LANES * j:d // 2 + LANES * (j + 1)])
        h_hi, h_lo = _split(h)
        wide = _dot(h_hi, rw_ref[...])
        logits = (wide[:, :LANES] + wide[:, LANES:] + _dot(h_lo, rw_ref[:, :LANES])
                  + rb_ref[...])
        lane = lax.broadcasted_iota(I32, logits.shape, 1)
        e_out = jnp.zeros(logits.shape, I32)
        g_out = jnp.zeros(logits.shape, F32)
        picked = jnp.zeros(logits.shape, F32)
        top = None
        total = None
        exps = []
        for k in range(TOP_K):
            m = jnp.max(logits, axis=1, keepdims=True)
            idx = jnp.min(jnp.where(logits == m, lane, LANES), axis=1, keepdims=True)
            hit = lane == idx
            logits = jnp.where(hit, -jnp.inf, logits)
            e_out = jnp.where(lane == k, idx, e_out)
            picked = jnp.where(hit, 1.0, picked)
            top = m if top is None else top
            ex = jnp.exp(m - top)
            exps.append(ex)
            total = ex if total is None else total + ex
        before = running + _dot(tri_ref[...], picked.astype(BF16))
        for k in range(TOP_K):
            g_out = jnp.where(lane == k, exps[k] / total, g_out)
        running = running + jnp.sum(picked, axis=0, keepdims=True)
        te_ref[0, rows, :] = e_out
        tr_ref[0, rows, :] = jnp.where(picked > 0.0, before, 0.0).astype(I32)
        tg_ref[0, rows, :] = g_out
    run_ref[...] = running
    cnt_ref[...] = running


def _post_mixer(a, w, bias, x, gate, shift, scale, g, router_w, router_b, base):
    b, t, d = x.shape
    k_in = a.shape[2]
    c = _row_chunks(d)
    tm = min(ROW_TILE, t)
    n_sub = 1
    rs = tm // n_sub
    bm = gate.shape[0]
    mod_map = (lambda i, j: (i, 0, 0)) if bm > 1 else (lambda i, j: (0, 0, 0))
    rw = jnp.concatenate(_split(jnp.zeros((d, LANES), F32).at[:, :N_EXPERTS].set(router_w)), axis=1)
    rb = jnp.full((1, LANES), NEG_BIG, F32).at[0, :N_EXPERTS].set(router_b)
    tri = (jnp.arange(rs)[:, None] > jnp.arange(rs)[None, :]).astype(BF16)
    row_spec = lambda wd: pl.BlockSpec((1, tm, wd), lambda i, j: (i, j, 0))
    const = lambda shape: pl.BlockSpec(shape, lambda i, j: (0,) * len(shape))
    est = 2 * tm * (k_in * 2 + d * 4 * 2 + d * 2 + 3 * LANES * 4) + 2 * k_in * d * 2 + 8 * tm * d * 4
    return pl.pallas_call(
        functools.partial(_post_kernel, n_sub=n_sub),
        grid=(b, t // tm),
        in_specs=[row_spec(k_in), const((k_in, d)), const((1, d)), row_spec(d),
                  pl.BlockSpec((1, 1, d), mod_map), pl.BlockSpec((1, 1, d), mod_map),
                  pl.BlockSpec((1, 1, d), mod_map), const((1, d)), const((d, 2 * LANES)), const((1, LANES)),
                  const((rs, rs)), const((1, LANES))],
        out_specs=[row_spec(d), pl.BlockSpec((1, tm * c, LANES), lambda i, j: (i, j, 0)),
                   row_spec(LANES), row_spec(LANES), row_spec(LANES), const((1, LANES))],
        out_shape=[jax.ShapeDtypeStruct((b, t, d), F32), jax.ShapeDtypeStruct((b, t * c, LANES), I32),
                   jax.ShapeDtypeStruct((b, t, LANES), I32), jax.ShapeDtypeStruct((b, t, LANES), I32),
                   jax.ShapeDtypeStruct((b, t, LANES), F32), jax.ShapeDtypeStruct((1, LANES), F32)],
        scratch_shapes=[pltpu.VMEM((1, LANES), F32)],
        compiler_params=_cparams(2, est),
        name="post_mixer",
    )(a, w, bias.reshape(1, d), x, gate, shift, scale, g.reshape(1, d), rw, rb, tri, base)


def _route(counts, n_rows_max):
    counts = counts[0, :N_EXPERTS].astype(I32)
    padded = (counts + EXPERT_TILE - 1) // EXPERT_TILE * EXPERT_TILE
    pad_end = jnp.cumsum(padded)
    pad_start = pad_end - padded
    tile_start = jnp.arange(n_rows_max // EXPERT_TILE, dtype=I32) * EXPERT_TILE
    tile_e = jnp.minimum(jnp.sum((pad_end[None, :] <= tile_start[:, None]).astype(I32), axis=1), N_EXPERTS - 1)
    n_valid = (pad_end[-1] // EXPERT_TILE).astype(I32).reshape(1)
    return tile_e.astype(I32), n_valid, pad_start.astype(I32), pad_end.astype(I32), padded.astype(I32)


def _pair_dest(top_e, rank, pad_start):
    e = top_e[..., :TOP_K].reshape(-1, TOP_K)
    row = rank[..., :N_EXPERTS].reshape(-1, 1, N_EXPERTS) + pad_start
    onehot = e[..., None] == jnp.arange(N_EXPERTS, dtype=I32)
    return jnp.sum(jnp.where(onehot, row, 0), axis=-1).reshape(-1).astype(I32)


def _dispatch_kernel(pend_ref, padded_ref, dest_ref, h_ref, *rest, tt, c, first):
    if first:
        xs_hbm, zbuf, zsem, stage, row_sems = rest
    else:
        _, xs_hbm, stage, row_sems = rest
    i = pl.program_id(0)

    if first:
        @pl.when(i == 0)
        def _():
            zbuf[...] = jnp.zeros(zbuf.shape, zbuf.dtype)

            def fill(e, carry):
                @pl.when(padded_ref[e] > 0)
                def _():
                    start = pl.multiple_of((pend_ref[e] - EXPERT_TILE) * c, EXPERT_TILE * c)
                    cp = pltpu.make_async_copy(zbuf, xs_hbm.at[pl.ds(start, EXPERT_TILE * c)], zsem)
                    cp.start()
                    cp.wait()
                return carry

            lax.fori_loop(0, N_EXPERTS, fill, 0)

            def fill_tail(t, carry):
                cp = pltpu.make_async_copy(
                    zbuf, xs_hbm.at[pl.ds(pl.multiple_of(t * (EXPERT_TILE * c), EXPERT_TILE * c), EXPERT_TILE * c)],
                    zsem)
                cp.start()
                cp.wait()
                return carry

            lax.fori_loop(pend_ref[N_EXPERTS - 1] // EXPERT_TILE, xs_hbm.shape[0] // (EXPERT_TILE * c),
                          fill_tail, 0)

    slot = i % 2
    stage[slot] = h_ref[...]

    def start(r, carry):
        for k in range(TOP_K):
            dst = pl.multiple_of(dest_ref[0, 0, TOP_K * r + k] * c, c)
            pltpu.make_async_copy(stage.at[slot, pl.ds(pl.multiple_of(r * c, c), c)], xs_hbm.at[pl.ds(dst, c)],
                                  row_sems.at[slot]).start(priority=k % 2)
        return carry

    lax.fori_loop(0, tt, start, 0, unroll=8)

    def wait_all(s):
        for _ in range(TOP_K):
            pltpu.make_async_copy(stage.at[s], stage.at[s], row_sems.at[s]).wait()

    @pl.when(i > 0)
    def _():
        wait_all(1 - slot)

    @pl.when(i == pl.num_programs(0) - 1)
    def _():
        wait_all(slot)


def _dispatch(dest, h_rows, xs, pad_end, padded, n_rows, c):
    n = h_rows.shape[0] // c
    tt = min(TOKEN_TILE, n)
    first = xs is None
    in_specs = [pl.BlockSpec((1, 1, tt * TOP_K), lambda i, pe, pd: (i, 0, 0), memory_space=pltpu.SMEM),
                pl.BlockSpec((tt * c, LANES), lambda i, pe, pd: (i, 0))]
    args = [pad_end, padded, dest.reshape(n // tt, 1, tt * TOP_K), h_rows]
    scratch = [pltpu.VMEM((2, tt * c, LANES), I32), pltpu.SemaphoreType.DMA((2,))]
    if first:
        scratch = [pltpu.VMEM((EXPERT_TILE * c, LANES), I32), pltpu.SemaphoreType.DMA] + scratch
    else:
        in_specs.append(pl.BlockSpec(memory_space=pl.ANY))
        args.append(xs)
    return pl.pallas_call(
        functools.partial(_dispatch_kernel, tt=tt, c=c, first=first),
        grid_spec=pltpu.PrefetchScalarGridSpec(
            num_scalar_prefetch=2, grid=(n // tt,), in_specs=in_specs,
            out_specs=pl.BlockSpec(memory_space=pl.ANY), scratch_shapes=scratch),
        out_shape=jax.ShapeDtypeStruct((n_rows * c, LANES), I32),
        input_output_aliases={} if first else {4: 0},
        compiler_params=pltpu.CompilerParams(dimension_semantics=("arbitrary",), has_side_effects=True),
        name="moe_dispatch",
    )(*args)


def _expert_kernel(te_ref, nv_ref, xs_ref, wgu_ref, wd_ref, bg_ref, bu_ref, bd_ref, sel_ref, ys_ref,
                   wg_s, wu_s, wd_s):
    i = pl.program_id(0)
    valid = i < nv_ref[0]
    new_expert = jnp.logical_or(i == 0, te_ref[i] != te_ref[jnp.maximum(i - 1, 0)])

    @pl.when(jnp.logical_and(valid, new_expert))
    def _():
        for blk in range(wgu_ref.shape[2] // (2 * LANES)):
            cols = wgu_ref[0, :, 2 * LANES * blk:2 * LANES * (blk + 1)].astype(BF16)
            r = _dot(cols, sel_ref[...])
            wg_s[:, LANES * blk:LANES * (blk + 1)] = r[:, :LANES].astype(BF16)
            wu_s[:, LANES * blk:LANES * (blk + 1)] = r[:, LANES:].astype(BF16)
        wd_s[...] = wd_ref[0].astype(BF16)

    @pl.when(valid)
    def _():
        c = _row_chunks(wg_s.shape[0])
        xb = _load_token_rows(xs_ref, (), xs_ref.shape[0] // c, c).astype(BF16)
        g = jnp.minimum(_dot(xb, wg_s[...]) + bg_ref[0], SWIGLU_LIMIT)
        u = jnp.clip(_dot(xb, wu_s[...]) + bu_ref[0], -SWIGLU_LIMIT, SWIGLU_LIMIT)
        act = (u + 1.0) * g * jax.nn.sigmoid(SWIGLU_ALPHA * g)
        y = _dot(act.astype(BF16), wd_s[...]) + bd_ref[0]
        _store_token_rows(ys_ref, (), y)

    @pl.when(jnp.logical_not(valid))
    def _():
        ys_ref[...] = jnp.zeros(ys_ref.shape, ys_ref.dtype)


def _gate_up_selector():
    col = jnp.arange(2 * LANES)
    src = jnp.where(col < LANES, 2 * col, 2 * (col - LANES) + 1)
    return (col[:, None] == src[None, :]).astype(BF16)


def _experts(xs, tile_e, n_valid, w_gu, w_down, bg, bu, bd):
    d, f2 = w_gu.shape[1:]
    f = f2 // 2
    c = _row_chunks(d)
    rows = xs.shape[0] // c
    tm = EXPERT_TILE
    tile_map = lambda i, te, nv: (jnp.minimum(i, nv[0] - 1), 0)
    w_map = lambda i, te, nv: (te[jnp.minimum(i, nv[0] - 1)], 0, 0)
    est = 4 * tm * d * 2 + 2 * 3 * d * f * 4 + 3 * d * f * 2 + 8 * tm * f * 4
    return pl.pallas_call(
        _expert_kernel,
        grid_spec=pltpu.PrefetchScalarGridSpec(
            num_scalar_prefetch=2,
            grid=(rows // tm,),
            in_specs=[pl.BlockSpec((tm * c, LANES), tile_map),
                      pl.BlockSpec((1, d, f2), w_map), pl.BlockSpec((1, f, d), w_map),
                      pl.BlockSpec((1, 1, f), w_map), pl.BlockSpec((1, 1, f), w_map),
                      pl.BlockSpec((1, 1, d), w_map),
                      pl.BlockSpec((2 * LANES, 2 * LANES), lambda i, te, nv: (0, 0))],
            out_specs=pl.BlockSpec((tm * c, LANES), lambda i, te, nv: (i, 0)),
            scratch_shapes=[pltpu.VMEM((d, f), BF16), pltpu.VMEM((d, f), BF16), pltpu.VMEM((f, d), BF16)],
        ),
        out_shape=jax.ShapeDtypeStruct((rows * c, LANES), I32),
        compiler_params=_cparams(1, est),
        name="moe_experts",
    )(tile_e, n_valid, xs, w_gu, w_down, bg, bu, bd, _gate_up_selector())


def _combine_kernel(dest_ref, dest_next_ref, ys_hbm, gates_ref, x_ref, gate_ref, o_ref, buf, sems, *, tt, c):
    i = pl.program_id(0)
    slot = i % 2

    def issue(idx_ref, s):
        def body(r, carry):
            for k in range(TOP_K):
                src = pl.multiple_of(idx_ref[0, 0, TOP_K * r + k] * c, c)
                pltpu.make_async_copy(ys_hbm.at[pl.ds(src, c)],
                                      buf.at[s, k, pl.ds(pl.multiple_of(r * c, c), c)],
                                      sems.at[s]).start(priority=k % 2)
            return carry

        lax.fori_loop(0, tt, body, 0, unroll=8)

    @pl.when(i == 0)
    def _():
        issue(dest_ref, 0)

    @pl.when(i + 1 < pl.num_programs(0))
    def _():
        issue(dest_next_ref, 1 - slot)

    pltpu.make_async_copy(buf.at[slot], buf.at[slot], sems.at[slot]).wait()
    gates = gates_ref[...]
    y = None
    for k in range(TOP_K):
        yk = _load_token_rows(buf, (slot, k), tt, c) * gates[:, k:k + 1]
        y = yk if y is None else y + yk
    o_ref[...] = x_ref[...] + gate_ref[0] * y


def _combine(dest, ys, gates, x, gate):
    b, t, d = x.shape
    n = b * t
    tt = min(TOKEN_TILE, t)
    steps = n // tt
    steps_per_batch = t // tt
    bm = gate.shape[0]
    gate_map = (lambda i: (i // steps_per_batch, 0, 0)) if bm > 1 else (lambda i: (0, 0, 0))
    est = 2 * TOP_K * tt * d * 2 + 4 * tt * d * 4 + 2 * tt * LANES * 4 + 6 * tt * d * 4
    dest2 = dest.reshape(steps, 1, tt * TOP_K)
    c = _row_chunks(d)
    out = pl.pallas_call(
        functools.partial(_combine_kernel, tt=tt, c=c),
        grid=(steps,),
        in_specs=[pl.BlockSpec((1, 1, tt * TOP_K), lambda i: (i, 0, 0), memory_space=pltpu.SMEM),
                  pl.BlockSpec((1, 1, tt * TOP_K), lambda i: (jnp.minimum(i + 1, steps - 1), 0, 0),
                               memory_space=pltpu.SMEM),
                  pl.BlockSpec(memory_space=pl.ANY),
                  pl.BlockSpec((tt, LANES), lambda i: (i, 0)),
                  pl.BlockSpec((tt, d), lambda i: (i, 0)),
                  pl.BlockSpec((1, 1, d), gate_map)],
        out_specs=pl.BlockSpec((tt, d), lambda i: (i, 0)),
        out_shape=jax.ShapeDtypeStruct((n, d), F32),
        scratch_shapes=[pltpu.VMEM((2, TOP_K, tt * c, LANES), I32), pltpu.SemaphoreType.DMA((2,))],
        compiler_params=_cparams(1, est),
        name="moe_combine",
    )(dest2, dest2, ys, gates.reshape(n, LANES), x.reshape(n, d), gate)
    return out.reshape(b, t, d)


def _moe(streams, counts, layer, expert_w):
    n_pairs = sum(s[0].shape[0] * s[0].shape[1] for s in streams) * TOP_K
    n_tiles_max = (n_pairs + N_EXPERTS * (EXPERT_TILE - 1)) // EXPERT_TILE
    n_rows_max = n_tiles_max * EXPERT_TILE
    tile_e, n_valid, pad_start, pad_end, padded = _route(counts, n_rows_max)
    c = _row_chunks(streams[0][0].shape[2])
    xs = None
    dests = []
    for x_new, rows, top_e, rank, _, _ in streams:
        n = x_new.shape[0] * x_new.shape[1]
        dests.append(_pair_dest(top_e, rank, pad_start))
        xs = _dispatch(dests[-1], rows.reshape(n * c, LANES), xs, pad_end, padded, n_rows_max, c)
    ys = _experts(xs, tile_e + layer * N_EXPERTS, n_valid, *expert_w)
    return [_combine(dests[i], ys, s[4], s[0], s[5]) for i, s in enumerate(streams)]


def _hyena_in_kernel(xm_ref, xp_ref, xn_ref, sh_ref, sc_ref, g_ref, w_ref, b_ref, cw_ref, cb_ref,
                     x0_ref, z_ref, *, tm, width, chunk):
    i = pl.program_id(1)
    last = pl.num_programs(1) - 1
    xe = jnp.concatenate([xp_ref[0], xm_ref[0], xn_ref[0]], axis=0)
    h = _norm_mod(xe, g_ref[...], sh_ref[0], sc_ref[0]).astype(BF16)
    rows = lax.broadcasted_iota(I32, (tm + 2 * SUBLANES, 1), 0)
    inside = jnp.logical_and(jnp.logical_or(rows >= SUBLANES, i > 0),
                             jnp.logical_or(rows < tm + SUBLANES, i < last))
    n_ext = tm + 2 * SUBLANES

    def conv(col):
        u = _dot(h, w_ref[:, col:col + chunk]) + b_ref[:, col:col + chunk]
        u = jnp.where(inside, u, 0.0)
        prev = pltpu.roll(u, 1, 0)[SUBLANES:SUBLANES + tm]
        nxt = pltpu.roll(u, n_ext - 1, 0)[SUBLANES:SUBLANES + tm]
        cw = cw_ref[:, col:col + chunk]
        return (prev * cw[0:1] + u[SUBLANES:SUBLANES + tm] * cw[1:2] + nxt * cw[2:3]
                + cb_ref[:, col:col + chunk])

    for c in range(width // chunk):
        x0 = conv(c * chunk)
        x1 = conv(width + c * chunk)
        v = conv(2 * width + c * chunk)
        x0_ref[0, :, c * chunk:(c + 1) * chunk] = x0.astype(BF16)
        z_ref[0, :, c * chunk:(c + 1) * chunk] = (v * x1).astype(BF16)


def _hyena_in(x, shift, scale, g, w_in, b_in, conv_w, conv_b):
    b, t, d = x.shape
    n3 = w_in.shape[1]
    width = n3 // 3
    tm = min(ROW_TILE, t)
    blocks8 = tm // SUBLANES
    last8 = t // SUBLANES - 1
    bm = shift.shape[0]
    mod_map = (lambda i, j: (i, 0, 0)) if bm > 1 else (lambda i, j: (0, 0, 0))
    const = lambda shape: pl.BlockSpec(shape, lambda i, j: (0,) * len(shape))
    chunk = 512
    est = 2 * (tm + 16) * d * 4 + 2 * d * n3 * 2 + 4 * tm * width * 2 + 10 * (tm + 16) * chunk * 4 + 4 * tm * d * 4
    return pl.pallas_call(
        functools.partial(_hyena_in_kernel, tm=tm, width=width, chunk=chunk),
        grid=(b, t // tm),
        in_specs=[pl.BlockSpec((1, tm, d), lambda i, j: (i, j, 0)),
                  pl.BlockSpec((1, SUBLANES, d), lambda i, j: (i, jnp.maximum(j * blocks8 - 1, 0), 0)),
                  pl.BlockSpec((1, SUBLANES, d), lambda i, j: (i, jnp.minimum((j + 1) * blocks8, last8), 0)),
                  pl.BlockSpec((1, 1, d), mod_map), pl.BlockSpec((1, 1, d), mod_map),
                  const((1, d)), const((d, n3)), const((1, n3)), const((HY_SHORT, n3)), const((1, n3))],
        out_specs=[pl.BlockSpec((1, tm, width), lambda i, j: (i, j, 0))] * 2,
        out_shape=[jax.ShapeDtypeStruct((b, t, width), BF16)] * 2,
        compiler_params=_cparams(2, est),
        name="hyena_in",
    )(x, x, x, shift, scale, g.reshape(1, d), w_in, b_in.reshape(1, n3), conv_w, conv_b.reshape(1, n3))


def _dft_matrix(length):
    n = 2 * length
    f = jnp.arange(length, dtype=I32)
    ang = ((f[:, None] * f[None, :]) % n).astype(F32) * (2.0 * math.pi / n)
    nyq = jnp.where(f % 2 == 0, 1.0, -1.0).astype(F32)
    msin = (-jnp.sin(ang)).at[0].set(nyq)
    return jnp.concatenate([jnp.cos(ang), msin], axis=0).astype(BF16)


def _filter_kernel(z_ref, w1_ref, b1_ref, fr_ref, w2_ref, b2_ref, w3f_ref, w3b_ref, b3f_ref, b3b_ref,
                   t_ref, dl_ref, wf_ref, a_ref, b_ref, c_ref, d_ref, *, length):
    freq = fr_ref[...]
    a1 = jnp.sin(freq * (_dot3(z_ref[...], w1_ref[...]) + b1_ref[...]))
    a2 = jnp.sin(freq * (_dot3(a1, w2_ref[...]) + b2_ref[...]))
    decay = jnp.exp(-t_ref[...] * dl_ref[...])
    rows = lax.broadcasted_iota(I32, (length, 1), 0)
    first = rows == 0
    hf = (_dot3(a2, w3f_ref[...]) + b3f_ref[...]) * decay
    hb = jnp.where(first, 0.0, (_dot3(a2, w3b_ref[...]) + b3b_ref[...]) * decay)
    inv = 1.0 / (jnp.sum(jnp.abs(hf), axis=0, keepdims=True) + jnp.sum(jnp.abs(hb), axis=0, keepdims=True) + EPS)
    hf = hf * inv
    hb = hb * inv
    p_hi, p_lo = _split(hf + hb)
    q_hi, q_lo = _split(hf - hb)
    wf = wf_ref[...]
    sp = _dot(wf, p_hi) + _dot(wf, p_lo)
    ki = _dot(wf[length:], q_hi) + _dot(wf[length:], q_lo)
    kr = sp[:length]
    knyq = sp[length:length + 1]
    n = 2.0 * length
    sc = jnp.where(first, 1.0 / n, 2.0 / n)
    a_ref[...] = kr * sc
    b_ref[...] = jnp.where(first, 0.0, -ki * sc)
    c_ref[...] = jnp.where(first, 0.0, ki * sc)
    d_ref[...] = jnp.where(first, knyq * (1.0 / n), kr * sc)


def _hyena_filter_spectrum(length, w1, b1, w2, b2, w3, b3, sin_freq, wf):
    width = w3.shape[1] // 2
    ffn = w1.shape[1]
    t = jnp.linspace(0.0, 1.0, length, dtype=F32)[:, None]
    w = 2.0 * math.pi * jnp.arange(length, dtype=F32)[:, None] / length
    f = jnp.linspace(1e-4, HY_BANDS - 1, HY_BANDS, dtype=F32)
    z = jnp.concatenate([t, jnp.cos(f * w), -jnp.sin(f * w)], axis=-1)
    zp = jnp.zeros((length, LANES), F32).at[:, :HY_EMB].set(z)
    pad2 = lambda m: jnp.zeros((LANES, m.shape[1] if m.shape[1] > LANES else LANES), F32).at[:m.shape[0], :m.shape[1]].set(m)
    padv = lambda v: jnp.zeros((1, LANES), F32).at[0, :v.shape[0]].set(v)
    max_decay = math.log(HY_TARGET) / HY_FAST_DECAY
    min_decay = math.log(HY_TARGET) / HY_SLOW_DECAY
    absdelta = jnp.abs(jnp.linspace(min_decay, max_decay, width, dtype=F32)).reshape(1, width)
    tc = LANES
    nct = width // tc
    const = lambda shape: pl.BlockSpec(shape, lambda j: (0,) * len(shape))
    w3p = pad2(w3)
    b3r = b3.reshape(1, 2 * width)
    plane = pl.BlockSpec((length, tc), lambda j: (0, j))
    est = 2 * length * length * 2 + 2 * 4 * length * tc * 4 + 24 * length * tc * 4
    return pl.pallas_call(
        functools.partial(_filter_kernel, length=length),
        grid=(nct,),
        in_specs=[const((length, LANES)), const((LANES, LANES)), const((1, LANES)), const((1, LANES)),
                  const((LANES, LANES)), const((1, LANES)),
                  pl.BlockSpec((LANES, tc), lambda j: (0, j)), pl.BlockSpec((LANES, tc), lambda j: (0, nct + j)),
                  pl.BlockSpec((1, tc), lambda j: (0, j)), pl.BlockSpec((1, tc), lambda j: (0, nct + j)),
                  const((length, 1)), pl.BlockSpec((1, tc), lambda j: (0, j)),
                  pl.BlockSpec((2 * length, length), lambda j: (0, 0), pipeline_mode=pl.Buffered(1))],
        out_specs=[plane] * 4,
        out_shape=[jax.ShapeDtypeStruct((length, width), F32)] * 4,
        compiler_params=_cparams(1, est),
        name="hyena_filter",
    )(zp, pad2(w1), padv(b1), padv(sin_freq), pad2(w2), padv(b2), w3p, w3p, b3r, b3r, t, absdelta, wf)


def _conv_fwd_kernel(z_ref, a_ref, b_ref, c_ref, d_ref, wf_ref, y_ref, *, length, chunk):
    z = z_ref[0]

    def body(s, carry):
        f0 = pl.multiple_of(s * chunk, chunk)
        vr = _dot(wf_ref[pl.ds(f0, chunk), :], z)
        vi = _dot(wf_ref[pl.ds(length + f0, chunk), :], z)
        sl = pl.ds(f0, chunk)
        y_ref[0, sl, :] = (vr * a_ref[sl, :] + vi * b_ref[sl, :]).astype(BF16)
        y_ref[0, pl.ds(length + f0, chunk), :] = (vr * c_ref[sl, :] + vi * d_ref[sl, :]).astype(BF16)
        return carry

    lax.fori_loop(0, length // chunk, body, 0, unroll=True)


def _conv_inv_kernel(y_ref, wi_ref, z_ref, x0_ref, db_ref, o_ref, *, length, chunk):
    yf = y_ref[0]

    def body(s, carry):
        sl = pl.ds(pl.multiple_of(s * chunk, chunk), chunk)
        y = _dot(wi_ref[sl, :], yf)
        o_ref[0, sl, :] = (x0_ref[0, sl, :].astype(F32)
                           * (y + z_ref[0, sl, :].astype(F32) * db_ref[...])).astype(BF16)
        return carry

    lax.fori_loop(0, length // chunk, body, 0, unroll=True)


def _long_conv(z, x0, planes, d_bias, wf, wi):
    b, length, width = z.shape
    tc = CONV_COLS
    chunk = min(CONV_CHUNK, length)
    single = pl.Buffered(1)
    plane = pl.BlockSpec((length, tc), lambda j, i: (0, j), pipeline_mode=single)
    est_f = 2 * length * length * 2 + 4 * length * tc * 4 + 2 * length * tc * 2 + 4 * length * tc * 2 + 8 * chunk * tc * 4
    spec = pl.pallas_call(
        functools.partial(_conv_fwd_kernel, length=length, chunk=chunk),
        grid=(width // tc, b),
        in_specs=[pl.BlockSpec((1, length, tc), lambda j, i: (i, 0, j))] + [plane] * 4
        + [pl.BlockSpec((2 * length, length), lambda j, i: (0, 0), pipeline_mode=single)],
        out_specs=pl.BlockSpec((1, 2 * length, tc), lambda j, i: (i, 0, j)),
        out_shape=jax.ShapeDtypeStruct((b, 2 * length, width), BF16),
        compiler_params=_cparams(2, est_f),
        name="hyena_conv_fwd",
    )(z, *planes, wf)
    col = lambda rows: pl.BlockSpec((1, rows, tc), lambda i, j: (i, 0, j))
    est_i = 2 * length * length * 2 + 4 * length * tc * 2 + 6 * length * tc * 2 + 8 * chunk * tc * 4
    return pl.pallas_call(
        functools.partial(_conv_inv_kernel, length=length, chunk=chunk),
        grid=(b, width // tc),
        in_specs=[col(2 * length),
                  pl.BlockSpec((length, 2 * length), lambda i, j: (0, 0), pipeline_mode=single),
                  col(length), col(length), pl.BlockSpec((1, tc), lambda i, j: (0, j))],
        out_specs=col(length),
        out_shape=jax.ShapeDtypeStruct((b, length, width), BF16),
        compiler_params=_cparams(2, est_i),
        name="hyena_conv_inv",
    )(spec, wi, z, x0, d_bias.reshape(1, width))


def _hyena_mixer(x, shift, scale, g, hy, wf, wi):
    w_in, b_in, conv_w, conv_b, w1, b1, w2, b2, w3, b3, sin_freq, d_bias = hy
    x0, z = _hyena_in(x, shift, scale, g, w_in, b_in, conv_w, conv_b)
    planes = _hyena_filter_spectrum(x.shape[1], w1, b1, w2, b2, w3, b3, sin_freq, wf)
    return _long_conv(z, x0, planes, d_bias, wf, wi)


def kernel(x, c, ctx, c_ctx, w_mod, b_mod, norm1_g, norm2_g, attn_w_qkv, attn_q_gain, attn_k_gain, attn_w_o, hy_w_in, hy_b_in, hy_conv_w, hy_conv_b, hy_filt_w1, hy_filt_b1, hy_filt_w2, hy_filt_b2, hy_filt_w3, hy_filt_b3, hy_sin_freq, hy_d_bias, hy_w_out, hy_b_out, moe_router_w, moe_router_b, moe_w_gu, moe_b_gu, moe_w_down, moe_b_down):
    b, length, d = x.shape
    n_ctx = ctx.shape[1]
    depth = w_mod.shape[0]
    n_q_heads = d // HEAD_DIM

    r_pad = -(-(b + 1) // SUBLANES) * SUBLANES
    cc = jnp.zeros((r_pad, d), F32).at[:b].set(c).at[b].set(c_ctx)
    mod_all = _modulation(cc, w_mod, b_mod).reshape(depth, r_pad, 6, d)

    rope_tabs = _rope_tables(length, n_q_heads + N_KV_HEADS)
    n_exp, _, f2 = moe_w_gu.shape[1:]
    b_gu = moe_b_gu.reshape(depth * n_exp, 1, f2)
    expert_w = (moe_w_gu.reshape(depth * n_exp, d, f2), moe_w_down.reshape(depth * n_exp, f2 // 2, d),
                b_gu[..., 0::2], b_gu[..., 1::2], moe_b_down.reshape(depth * n_exp, 1, d))
    dft = {}

    def dft_pair(n):
        if n not in dft:
            wf = _dft_matrix(n)
            dft[n] = (wf, wf.T)
        return dft[n]

    for i in range(depth):
        j = i // N_MIXERS
        is_attn = i % N_MIXERS == 0
        ctx_live = any(m % N_MIXERS == 0 for m in range(i + 1, depth))
        mod = [mod_all[i, :b, s].reshape(b, 1, d) for s in range(6)]
        mod_c = [mod_all[i, b, s].reshape(1, 1, d) for s in range(6)]
        if is_attn:
            w_qkv = attn_w_qkv[j].astype(BF16)
            n_q = n_q_heads * HEAD_DIM
            q, k, v = _qkv_project(x, mod[0], mod[1], norm1_g[i], w_qkv, attn_q_gain[j], attn_k_gain[j],
                                   rope_tabs, True)
            w_c = w_qkv if ctx_live else w_qkv[:, n_q:]
            ctx_out = _qkv_project(ctx, mod_c[0], mod_c[1], norm1_g[i], w_c, attn_q_gain[j], attn_k_gain[j],
                                   None, ctx_live)
            k_c, v_c = ctx_out[-2], ctx_out[-1]
            a = _attention(q, [(k, v), (k_c, v_c)])
            a_c = _attention(ctx_out[0], [(k_c, v_c)]) if ctx_live else None
            w_out, b_out = attn_w_o[j].astype(BF16), jnp.zeros((d,), F32)
        else:
            hy = (hy_w_in[j].astype(BF16), hy_b_in[j], hy_conv_w[j], hy_conv_b[j], hy_filt_w1[j], hy_filt_b1[j],
                  hy_filt_w2[j], hy_filt_b2[j], hy_filt_w3[j], hy_filt_b3[j], hy_sin_freq[j], hy_d_bias[j])
            a = _hyena_mixer(x, mod[0], mod[1], norm1_g[i], hy, *dft_pair(length))
            a_c = _hyena_mixer(ctx, mod_c[0], mod_c[1], norm1_g[i], hy, *dft_pair(n_ctx)) if ctx_live else None
            w_out, b_out = hy_w_out[j].astype(BF16), hy_b_out[j]

        post = _post_mixer(a, w_out, b_out, x, mod[2], mod[3], mod[4], norm2_g[i],
                           moe_router_w[i], moe_router_b[i], jnp.zeros((1, LANES), F32))
        streams = [tuple(post[:5]) + (mod[5],)]
        counts = post[5]
        if ctx_live:
            post = _post_mixer(a_c, w_out, b_out, ctx, mod_c[2], mod_c[3], mod_c[4], norm2_g[i],
                               moe_router_w[i], moe_router_b[i], counts)
            streams.append(tuple(post[:5]) + (mod_c[5],))
            counts = post[5]
        outs = _moe(streams, counts, i, expert_w)
        x = outs[0]
        if ctx_live:
            ctx = outs[1]
    return x
```

```python
import functools
import math

import jax
import jax.numpy as jnp
from jax import lax
from jax.experimental import pallas as pl
from jax.experimental.pallas import tpu as pltpu

F32 = jnp.float32
BF16 = jnp.bfloat16
I32 = jnp.int32

LANES = 128
SUBLANES = 8
VMEM_BYTES = 64 * 2**20
VMEM_CAP = VMEM_BYTES - 8 * 2**20

GRID_W = 64
EPS = 1e-6
HEAD_DIM = 64
N_KV_HEADS = 4
ROPE_THETA = 10000.0
ROPE_PAIRS = HEAD_DIM // 4
HY_SHORT = 3
HY_EMB = 33
HY_BANDS = (HY_EMB - 1) // 2
HY_FAST_DECAY = 0.3
HY_SLOW_DECAY = 1.5
HY_TARGET = 1e-2
N_EXPERTS = 32
TOP_K = 4
SWIGLU_LIMIT = 7.0
SWIGLU_ALPHA = 1.702
N_MIXERS = 2
NEG_BIG = -1e30

ROW_TILE = 512
ATT_Q_TILE = 256
EXPERT_TILE = 512
TOKEN_TILE = 256
CONV_COLS = 256
CONV_CHUNK = 512


def _cparams(n_axes, vmem_estimate):
    limit = int(min(max(vmem_estimate * 5 // 4, 32 * 2**20), VMEM_CAP))
    return pltpu.CompilerParams(dimension_semantics=("arbitrary",) * n_axes, vmem_limit_bytes=limit)


def _split(a):
    hi = a.astype(BF16)
    lo = (a - hi.astype(F32)).astype(BF16)
    return hi, lo


def _dot(a, b):
    return jnp.dot(a, b, preferred_element_type=F32)


def _dot3(a, b):
    a_hi, a_lo = _split(a)
    b_hi, b_lo = _split(b)
    return _dot(a_hi, b_hi) + _dot(a_lo, b_hi) + _dot(a_hi, b_lo)


def _norm_mod(x, g, shift, scale):
    y = x * lax.rsqrt(jnp.mean(x * x, axis=-1, keepdims=True) + EPS)
    return (y * g) * (1.0 + scale) + shift


def _pack(a, b):
    ua = lax.bitcast_convert_type(a.astype(BF16).astype(F32), I32)
    ub = lax.bitcast_convert_type(b.astype(BF16).astype(F32), I32)
    return (ua & jnp.int32(-65536)) | lax.shift_right_logical(ub, jnp.int32(16))


def _unpack(p):
    a = lax.bitcast_convert_type(p & jnp.int32(-65536), F32)
    b = lax.bitcast_convert_type(lax.shift_left(p, jnp.int32(16)), F32)
    return a, b


def _row_chunks(d):
    return d // (2 * LANES)


def _store_token_rows(ref, lead, y):
    m, d = y.shape
    c = _row_chunks(d)
    for j in range(c):
        lo = y[:, LANES * j:LANES * (j + 1)]
        hi = y[:, d // 2 + LANES * j:d // 2 + LANES * (j + 1)]
        ref[lead + (pl.ds(j, m, stride=c), slice(None))] = _pack(lo, hi)


def _load_token_rows(ref, lead, m, c):
    halves = [_unpack(ref[lead + (pl.ds(j, m, stride=c), slice(None))]) for j in range(c)]
    return jnp.concatenate([h[0] for h in halves] + [h[1] for h in halves], axis=1)


def _mod_kernel(c_ref, w_ref, b_ref, o_ref):
    c = c_ref[...]
    a = c * jax.nn.sigmoid(c)
    o_ref[0] = _dot3(a, w_ref[0]) + b_ref[0]


def _modulation(cc, w_mod, b_mod):
    depth, d, n = w_mod.shape
    r = cc.shape[0]
    tn = 1536
    return pl.pallas_call(
        _mod_kernel,
        grid=(depth, n // tn),
        in_specs=[
            pl.BlockSpec((r, d), lambda i, j: (0, 0)),
            pl.BlockSpec((1, d, tn), lambda i, j: (i, 0, j)),
            pl.BlockSpec((1, 1, tn), lambda i, j: (i, 0, j)),
        ],
        out_specs=pl.BlockSpec((1, r, tn), lambda i, j: (i, 0, j)),
        out_shape=jax.ShapeDtypeStruct((depth, r, n), F32),
        compiler_params=_cparams(2, 2 * d * tn * 4 + 3 * d * tn * 2),
        name="modulation",
    )(cc, w_mod, b_mod.reshape(depth, 1, n))


def _qkv_kernel(*refs, n_qk, rope, with_q):
    x_ref, sh_ref, sc_ref, g_ref, w_ref, gain_ref, e_ref, et_ref = refs[:8]
    pos = 8
    if rope:
        cos_ref, sa_ref, sb_ref = refs[pos:pos + 3]
        pos += 3
    outs = refs[pos:]
    n_q = n_qk - N_KV_HEADS * HEAD_DIM
    tm = x_ref.shape[1]
    n_sub = 2 if tm % (2 * SUBLANES) == 0 else 1
    for s in range(n_sub):
        rows = slice(s * (tm // n_sub), (s + 1) * (tm // n_sub))
        h = _norm_mod(x_ref[0, rows, :], g_ref[...], sh_ref[0], sc_ref[0]).astype(BF16)
        acc = _dot(h, w_ref[...])
        qk = acc[:, :n_qk]
        ssq = _dot((qk * qk).astype(BF16), e_ref[...])
        r_hi, r_lo = _split(lax.rsqrt(ssq * (1.0 / HEAD_DIM) + EPS))
        y = qk * (_dot(r_hi, et_ref[...]) + _dot(r_lo, et_ref[...])) * gain_ref[...]
        if rope:
            y = (y * cos_ref[rows, :] + pltpu.roll(y, n_qk - ROPE_PAIRS, 1) * sa_ref[rows, :]
                 + pltpu.roll(y, ROPE_PAIRS, 1) * sb_ref[rows, :])
        kv_outs = outs
        if with_q:
            outs[0][0, rows, :] = (y[:, :n_q] * (HEAD_DIM ** -0.5)).astype(BF16)
            kv_outs = outs[1:]
        kv_outs[0][0, rows, :] = y[:, n_q:].astype(BF16)
        kv_outs[1][0, rows, :] = acc[:, n_qk:].astype(BF16)


def _qkv_project(x, shift, scale, g, w, q_gain, k_gain, rope_tabs, with_q):
    b, t, d = x.shape
    kv = N_KV_HEADS * HEAD_DIM
    nc = w.shape[1]
    n_qk = nc - kv
    n_heads = n_qk // HEAD_DIM
    tm = min(ROW_TILE, t)
    gains = ([q_gain] * (n_heads - N_KV_HEADS)) + [k_gain] * N_KV_HEADS
    gain = jnp.concatenate(gains).reshape(1, n_qk).astype(F32)
    head_of_col = jnp.arange(n_qk) // HEAD_DIM
    e = (head_of_col[:, None] == jnp.arange(LANES)[None, :]).astype(BF16)
    et = e.T
    bm = shift.shape[0]
    mod_map = (lambda i, j: (j, 0, 0)) if bm > 1 else (lambda i, j: (0, 0, 0))
    in_specs = [
        pl.BlockSpec((1, tm, d), lambda i, j: (j, i, 0)),
        pl.BlockSpec((1, 1, d), mod_map),
        pl.BlockSpec((1, 1, d), mod_map),
        pl.BlockSpec((1, d), lambda i, j: (0, 0)),
        pl.BlockSpec((d, nc), lambda i, j: (0, 0)),
        pl.BlockSpec((1, n_qk), lambda i, j: (0, 0)),
        pl.BlockSpec((n_qk, LANES), lambda i, j: (0, 0)),
        pl.BlockSpec((LANES, n_qk), lambda i, j: (0, 0)),
    ]
    args = [x, shift, scale, g.reshape(1, d), w, gain, e, et]
    rope = rope_tabs is not None
    if rope:
        in_specs += [pl.BlockSpec((tm, n_qk), lambda i, j: (i, 0))] * 3
        args += list(rope_tabs)
    out_shape, out_specs = [], []
    widths = ([n_qk - kv] if with_q else []) + [kv, kv]
    for wd in widths:
        out_shape.append(jax.ShapeDtypeStruct((b, t, wd), BF16))
        out_specs.append(pl.BlockSpec((1, tm, wd), lambda i, j: (j, i, 0)))
    est = 2 * tm * d * 4 + 2 * d * nc * 2 + 6 * tm * n_qk * 4 + 8 * tm * nc * 4
    return pl.pallas_call(
        functools.partial(_qkv_kernel, n_qk=n_qk, rope=rope, with_q=with_q),
        grid=(t // tm, b),
        in_specs=in_specs,
        out_specs=out_specs,
        out_shape=out_shape,
        compiler_params=_cparams(2, est),
        name="qkv_project",
    )(*args)


def _rope_tables(n_tokens, n_heads):
    n_rows = n_tokens // GRID_W
    row = jnp.repeat(jnp.arange(n_rows, dtype=F32), GRID_W)
    col = jnp.tile(jnp.arange(GRID_W, dtype=F32), n_rows)
    inv_freq = ROPE_THETA ** (-jnp.arange(ROPE_PAIRS, dtype=F32) / ROPE_PAIRS)
    ar, ac = row[:, None] * inv_freq, col[:, None] * inv_freq
    zero = jnp.zeros_like(ar)
    cos = jnp.concatenate([jnp.cos(ar), jnp.cos(ar), jnp.cos(ac), jnp.cos(ac)], axis=1)
    sin_a = jnp.concatenate([-jnp.sin(ar), zero, -jnp.sin(ac), zero], axis=1)
    sin_b = jnp.concatenate([zero, jnp.sin(ar), zero, jnp.sin(ac)], axis=1)
    return tuple(jnp.tile(tab, (1, n_heads)) for tab in (cos, sin_a, sin_b))


def _attn_kernel(*refs, n_seg, tq):
    q_ref = refs[0]
    kv_refs = refs[1:1 + 2 * n_seg]
    o_ref = refs[1 + 2 * n_seg]
    q_per_kv = q_ref.shape[2] // (N_KV_HEADS * HEAD_DIM)
    n_halves = 2 if tq % (2 * SUBLANES) == 0 else 1
    rows_u = tq // n_halves
    units = [(g, half) for g in range(N_KV_HEADS) for half in range(n_halves)]

    def scores_of(unit):
        g, half = unit
        r0 = half * rows_u
        qs = jnp.concatenate([q_ref[0, r0:r0 + rows_u, (g * q_per_kv + j) * HEAD_DIM:(g * q_per_kv + j + 1) * HEAD_DIM]
                              for j in range(q_per_kv)], axis=0)
        ksl = slice(g * HEAD_DIM, (g + 1) * HEAD_DIM)
        return [lax.dot_general(qs, kv_refs[2 * s][0, :, ksl], (((1,), (1,)), ((), ())),
                                preferred_element_type=F32) for s in range(n_seg)]

    def finish(unit, scores):
        g, half = unit
        r0 = half * rows_u
        ksl = slice(g * HEAD_DIM, (g + 1) * HEAD_DIM)
        m = functools.reduce(jnp.maximum, [jnp.max(sc, axis=-1, keepdims=True) for sc in scores])
        probs = [jnp.exp(sc - m) for sc in scores]
        denom = functools.reduce(jnp.add, [jnp.sum(p, axis=-1, keepdims=True) for p in probs])
        o = functools.reduce(jnp.add, [_dot(probs[s].astype(BF16), kv_refs[2 * s + 1][0, :, ksl])
                                       for s in range(n_seg)])
        o = o / denom
        for j in range(q_per_kv):
            h = g * q_per_kv + j
            o_ref[0, r0:r0 + rows_u, h * HEAD_DIM:(h + 1) * HEAD_DIM] = o[j * rows_u:(j + 1) * rows_u].astype(BF16)

    pending = None
    for unit in units:
        scores = scores_of(unit)
        if pending is not None:
            finish(*pending)
        pending = (unit, scores)
    finish(*pending)


def _attention(q, segments):
    b, t, dq = q.shape
    tq = min(ATT_Q_TILE, t)
    in_specs = [pl.BlockSpec((1, tq, dq), lambda i, j: (i, j, 0))]
    args = [q]
    t_keys = 0
    for k, v in segments:
        ts, kv = k.shape[1], k.shape[2]
        in_specs += [pl.BlockSpec((1, ts, kv), lambda i, j: (i, 0, 0))] * 2
        args += [k, v]
        t_keys += ts
    q_rows = tq * dq // (N_KV_HEADS * HEAD_DIM)
    est = 4 * tq * dq * 2 + 8 * t_keys * N_KV_HEADS * HEAD_DIM * 2 + 4 * q_rows * t_keys * 4
    return pl.pallas_call(
        functools.partial(_attn_kernel, n_seg=len(segments), tq=tq),
        grid=(b, t // tq),
        in_specs=in_specs,
        out_specs=pl.BlockSpec((1, tq, dq), lambda i, j: (i, j, 0)),
        out_shape=jax.ShapeDtypeStruct((b, t, dq), BF16),
        compiler_params=_cparams(2, est),
        name="attention",
    )(*args)


def _post_kernel(a_ref, w_ref, bias_ref, x_ref, gate_ref, sh_ref, sc_ref, g_ref, rw_ref, rb_ref, tri_ref, base_ref,
                 xo_ref, hp_ref, te_ref, tr_ref, tg_ref, cnt_ref, run_ref, *, n_sub):
    @pl.when(jnp.logical_and(pl.program_id(0) == 0, pl.program_id(1) == 0))
    def _():
        run_ref[...] = base_ref[...]

    tm = x_ref.shape[1]
    rs = tm // n_sub
    c = hp_ref.shape[1] // tm
    running = run_ref[...]
    for s in range(n_sub):
        rows = slice(s * rs, (s + 1) * rs)
        o = _dot(a_ref[0, rows, :], w_ref[...]) + bias_ref[...]
        x_new = x_ref[0, rows, :] + gate_ref[0] * o
        xo_ref[0, rows, :] = x_new
        h = _norm_mod(x_new, g_ref[...], sh_ref[0], sc_ref[0])
        d = h.shape[1]
        for j in range(c):
            hp_ref[0, pl.ds(s * rs * c + j, rs, stride=c), :] = _pack(
                h[:, LANES * j:LANES * (j + 1)], h[:, d // 2 + LANES * j:d // 2 + LANES * (j + 1)])
        h_hi, h_lo = _split(h)
        wide = _dot(h_hi, rw_ref[...])
        logits = (wide[:, :LANES] + wide[:, LANES:] + _dot(h_lo, rw_ref[:, :LANES])
                  + rb_ref[...])
        lane = lax.broadcasted_iota(I32, logits.shape, 1)
        e_out = jnp.zeros(logits.shape, I32)
        g_out = jnp.zeros(logits.shape, F32)
        picked = jnp.zeros(logits.shape, F32)
        top = None
        total = None
        exps = []
        for k in range(TOP_K):
            m = jnp.max(logits, axis=1, keepdims=True)
            idx = jnp.min(jnp.where(logits == m, lane, LANES), axis=1, keepdims=True)
            hit = lane == idx
            logits = jnp.where(hit, -jnp.inf, logits)
            e_out = jnp.where(lane == k, idx, e_out)
            picked = jnp.where(hit, 1.0, picked)
            top = m if top is None else top
            ex = jnp.exp(m - top)
            exps.append(ex)
            total = ex if total is None else total + ex
        before = running + _dot(tri_ref[...], picked.astype(BF16))
        for k in range(TOP_K):
            g_out = jnp.where(lane == k, exps[k] / total, g_out)
        running = running + jnp.sum(picked, axis=0, keepdims=True)
        te_ref[0, rows, :] = e_out
        tr_ref[0, rows, :] = jnp.where(picked > 0.0, before, 0.0).astype(I32)
        tg_ref[0, rows, :] = g_out
    run_ref[...] = running
    cnt_ref[...] = running


def _post_mixer(a, w, bias, x, gate, shift, scale, g, router_w, router_b, base):
    b, t, d = x.shape
    k_in = a.shape[2]
    c = _row_chunks(d)
    tm = min(ROW_TILE, t)
    n_sub = 1
    rs = tm // n_sub
    bm = gate.shape[0]
    mod_map = (lambda i, j: (i, 0, 0)) if bm > 1 else (lambda i, j: (0, 0, 0))
    rw = jnp.concatenate(_split(jnp.zeros((d, LANES), F32).at[:, :N_EXPERTS].set(router_w)), axis=1)
    rb = jnp.full((1, LANES), NEG_BIG, F32).at[0, :N_EXPERTS].set(router_b)
    tri = (jnp.arange(rs)[:, None] > jnp.arange(rs)[None, :]).astype(BF16)
    row_spec = lambda wd: pl.BlockSpec((1, tm, wd), lambda i, j: (i, j, 0))
    const = lambda shape: pl.BlockSpec(shape, lambda i, j: (0,) * len(shape))
    est = 2 * tm * (k_in * 2 + d * 4 * 2 + d * 2 + 3 * LANES * 4) + 2 * k_in * d * 2 + 8 * tm * d * 4
    return pl.pallas_call(
        functools.partial(_post_kernel, n_sub=n_sub),
        grid=(b, t // tm),
        in_specs=[row_spec(k_in), const((k_in, d)), const((1, d)), row_spec(d),
                  pl.BlockSpec((1, 1, d), mod_map), pl.BlockSpec((1, 1, d), mod_map),
                  pl.BlockSpec((1, 1, d), mod_map), const((1, d)), const((d, 2 * LANES)), const((1, LANES)),
                  const((rs, rs)), const((1, LANES))],
        out_specs=[row_spec(d), pl.BlockSpec((1, tm * c, LANES), lambda i, j: (i, j, 0)),
                   row_spec(LANES), row_spec(LANES), row_spec(LANES), const((1, LANES))],
        out_shape=[jax.ShapeDtypeStruct((b, t, d), F32), jax.ShapeDtypeStruct((b, t * c, LANES), I32),
                   jax.ShapeDtypeStruct((b, t, LANES), I32), jax.ShapeDtypeStruct((b, t, LANES), I32),
                   jax.ShapeDtypeStruct((b, t, LANES), F32), jax.ShapeDtypeStruct((1, LANES), F32)],
        scratch_shapes=[pltpu.VMEM((1, LANES), F32)],
        compiler_params=_cparams(2, est),
        name="post_mixer",
    )(a, w, bias.reshape(1, d), x, gate, shift, scale, g.reshape(1, d), rw, rb, tri, base)


def _route(counts, n_rows_max):
    counts = counts[0, :N_EXPERTS].astype(I32)
    padded = (counts + EXPERT_TILE - 1) // EXPERT_TILE * EXPERT_TILE
    pad_end = jnp.cumsum(padded)
    pad_start = pad_end - padded
    tile_start = jnp.arange(n_rows_max // EXPERT_TILE, dtype=I32) * EXPERT_TILE
    tile_e = jnp.minimum(jnp.sum((pad_end[None, :] <= tile_start[:, None]).astype(I32), axis=1), N_EXPERTS - 1)
    n_valid = (pad_end[-1] // EXPERT_TILE).astype(I32).reshape(1)
    return tile_e.astype(I32), n_valid, pad_start.astype(I32), pad_end.astype(I32), padded.astype(I32)


def _pair_dest(top_e, rank, pad_start):
    e = top_e[..., :TOP_K].reshape(-1, TOP_K)
    row = rank[..., :N_EXPERTS].reshape(-1, 1, N_EXPERTS) + pad_start
    onehot = e[..., None] == jnp.arange(N_EXPERTS, dtype=I32)
    return jnp.sum(jnp.where(onehot, row, 0), axis=-1).reshape(-1).astype(I32)


def _dispatch_kernel(pend_ref, padded_ref, dest_ref, h_ref, *rest, tt, c, first):
    if first:
        xs_hbm, zbuf, zsem, stage, row_sems = rest
    else:
        _, xs_hbm, stage, row_sems = rest
    i = pl.program_id(0)

    if first:
        @pl.when(i == 0)
        def _():
            zbuf[...] = jnp.zeros(zbuf.shape, zbuf.dtype)

            def fill(e, carry):
                @pl.when(padded_ref[e] > 0)
                def _():
                    start = pl.multiple_of((pend_ref[e] - EXPERT_TILE) * c, EXPERT_TILE * c)
                    cp = pltpu.make_async_copy(zbuf, xs_hbm.at[pl.ds(start, EXPERT_TILE * c)], zsem)
                    cp.start()
                    cp.wait()
                return carry

            lax.fori_loop(0, N_EXPERTS, fill, 0)

            def fill_tail(t, carry):
                cp = pltpu.make_async_copy(
                    zbuf, xs_hbm.at[pl.ds(pl.multiple_of(t * (EXPERT_TILE * c), EXPERT_TILE * c), EXPERT_TILE * c)],
                    zsem)
                cp.start()
                cp.wait()
                return carry

            lax.fori_loop(pend_ref[N_EXPERTS - 1] // EXPERT_TILE, xs_hbm.shape[0] // (EXPERT_TILE * c),
                          fill_tail, 0)

    slot = i % 2
    stage[slot] = h_ref[...]

    def start(r, carry):
        for k in range(TOP_K):
            dst = pl.multiple_of(dest_ref[0, 0, TOP_K * r + k] * c, c)
            pltpu.make_async_copy(stage.at[slot, pl.ds(pl.multiple_of(r * c, c), c)], xs_hbm.at[pl.ds(dst, c)],
                                  row_sems.at[slot]).start(priority=k % 2)
        return carry

    lax.fori_loop(0, tt, start, 0, unroll=8)

    def wait_all(s):
        for _ in range(TOP_K):
            pltpu.make_async_copy(stage.at[s], stage.at[s], row_sems.at[s]).wait()

    @pl.when(i > 0)
    def _():
        wait_all(1 - slot)

    @pl.when(i == pl.num_programs(0) - 1)
    def _():
        wait_all(slot)


def _dispatch(dest, h_rows, xs, pad_end, padded, n_rows, c):
    n = h_rows.shape[0] // c
    tt = min(TOKEN_TILE, n)
    first = xs is None
    in_specs = [pl.BlockSpec((1, 1, tt * TOP_K), lambda i, pe, pd: (i, 0, 0), memory_space=pltpu.SMEM),
                pl.BlockSpec((tt * c, LANES), lambda i, pe, pd: (i, 0))]
    args = [pad_end, padded, dest.reshape(n // tt, 1, tt * TOP_K), h_rows]
    scratch = [pltpu.VMEM((2, tt * c, LANES), I32), pltpu.SemaphoreType.DMA((2,))]
    if first:
        scratch = [pltpu.VMEM((EXPERT_TILE * c, LANES), I32), pltpu.SemaphoreType.DMA] + scratch
    else:
        in_specs.append(pl.BlockSpec(memory_space=pl.ANY))
        args.append(xs)
    return pl.pallas_call(
        functools.partial(_dispatch_kernel, tt=tt, c=c, first=first),
        grid_spec=pltpu.PrefetchScalarGridSpec(
            num_scalar_prefetch=2, grid=(n // tt,), in_specs=in_specs,
            out_specs=pl.BlockSpec(memory_space=pl.ANY), scratch_shapes=scratch),
        out_shape=jax.ShapeDtypeStruct((n_rows * c, LANES), I32),
        input_output_aliases={} if first else {4: 0},
        compiler_params=pltpu.CompilerParams(dimension_semantics=("arbitrary",), has_side_effects=True),
        name="moe_dispatch",
    )(*args)


def _expert_kernel(te_ref, nv_ref, xs_ref, wgu_ref, wd_ref, bg_ref, bu_ref, bd_ref, sel_ref, ys_ref,
                   wg_s, wu_s, wd_s):
    i = pl.program_id(0)
    valid = i < nv_ref[0]
    new_expert = jnp.logical_or(i == 0, te_ref[i] != te_ref[jnp.maximum(i - 1, 0)])

    @pl.when(jnp.logical_and(valid, new_expert))
    def _():
        for blk in range(wgu_ref.shape[2] // (2 * LANES)):
            cols = wgu_ref[0, :, 2 * LANES * blk:2 * LANES * (blk + 1)].astype(BF16)
            r = _dot(cols, sel_ref[...])
            wg_s[:, LANES * blk:LANES * (blk + 1)] = r[:, :LANES].astype(BF16)
            wu_s[:, LANES * blk:LANES * (blk + 1)] = r[:, LANES:].astype(BF16)
        wd_s[...] = wd_ref[0].astype(BF16)

    @pl.when(valid)
    def _():
        c = _row_chunks(wg_s.shape[0])
        xb = _load_token_rows(xs_ref, (), xs_ref.shape[0] // c, c).astype(BF16)
        g = jnp.minimum(_dot(xb, wg_s[...]) + bg_ref[0], SWIGLU_LIMIT)
        u = jnp.clip(_dot(xb, wu_s[...]) + bu_ref[0], -SWIGLU_LIMIT, SWIGLU_LIMIT)
        act = (u + 1.0) * g * jax.nn.sigmoid(SWIGLU_ALPHA * g)
        y = _dot(act.astype(BF16), wd_s[...]) + bd_ref[0]
        _store_token_rows(ys_ref, (), y)

    @pl.when(jnp.logical_not(valid))
    def _():
        ys_ref[...] = jnp.zeros(ys_ref.shape, ys_ref.dtype)


def _gate_up_selector():
    col = jnp.arange(2 * LANES)
    src = jnp.where(col < LANES, 2 * col, 2 * (col - LANES) + 1)
    return (col[:, None] == src[None, :]).astype(BF16)


def _experts(xs, tile_e, n_valid, w_gu, w_down, bg, bu, bd):
    d, f2 = w_gu.shape[1:]
    f = f2 // 2
    c = _row_chunks(d)
    rows = xs.shape[0] // c
    tm = EXPERT_TILE
    tile_map = lambda i, te, nv: (jnp.minimum(i, nv[0] - 1), 0)
    w_map = lambda i, te, nv: (te[jnp.minimum(i, nv[0] - 1)], 0, 0)
    est = 4 * tm * d * 2 + 2 * 3 * d * f * 4 + 3 * d * f * 2 + 8 * tm * f * 4
    return pl.pallas_call(
        _expert_kernel,
        grid_spec=pltpu.PrefetchScalarGridSpec(
            num_scalar_prefetch=2,
            grid=(rows // tm,),
            in_specs=[pl.BlockSpec((tm * c, LANES), tile_map),
                      pl.BlockSpec((1, d, f2), w_map), pl.BlockSpec((1, f, d), w_map),
                      pl.BlockSpec((1, 1, f), w_map), pl.BlockSpec((1, 1, f), w_map),
                      pl.BlockSpec((1, 1, d), w_map),
                      pl.BlockSpec((2 * LANES, 2 * LANES), lambda i, te, nv: (0, 0))],
            out_specs=pl.BlockSpec((tm * c, LANES), lambda i, te, nv: (i, 0)),
            scratch_shapes=[pltpu.VMEM((d, f), BF16), pltpu.VMEM((d, f), BF16), pltpu.VMEM((f, d), BF16)],
        ),
        out_shape=jax.ShapeDtypeStruct((rows * c, LANES), I32),
        compiler_params=_cparams(1, est),
        name="moe_experts",
    )(tile_e, n_valid, xs, w_gu, w_down, bg, bu, bd, _gate_up_selector())


def _combine_kernel(dest_ref, dest_next_ref, ys_hbm, gates_ref, x_ref, gate_ref, o_ref, buf, sems, *, tt, c):
    i = pl.program_id(0)
    slot = i % 2

    def issue(idx_ref, s):
        def body(r, carry):
            for k in range(TOP_K):
                src = pl.multiple_of(idx_ref[0, 0, TOP_K * r + k] * c, c)
                pltpu.make_async_copy(ys_hbm.at[pl.ds(src, c)],
                                      buf.at[s, k, pl.ds(pl.multiple_of(r * c, c), c)],
                                      sems.at[s]).start(priority=k % 2)
            return carry

        lax.fori_loop(0, tt, body, 0, unroll=8)

    @pl.when(i == 0)
    def _():
        issue(dest_ref, 0)

    @pl.when(i + 1 < pl.num_programs(0))
    def _():
        issue(dest_next_ref, 1 - slot)

    pltpu.make_async_copy(buf.at[slot], buf.at[slot], sems.at[slot]).wait()
    gates = gates_ref[...]
    y = None
    for k in range(TOP_K):
        yk = _load_token_rows(buf, (slot, k), tt, c) * gates[:, k:k + 1]
        y = yk if y is None else y + yk
    o_ref[...] = x_ref[...] + gate_ref[0] * y


def _combine(dest, ys, gates, x, gate):
    b, t, d = x.shape
    n = b * t
    tt = min(TOKEN_TILE, t)
    steps = n // tt
    steps_per_batch = t // tt
    bm = gate.shape[0]
    gate_map = (lambda i: (i // steps_per_batch, 0, 0)) if bm > 1 else (lambda i: (0, 0, 0))
    est = 2 * TOP_K * tt * d * 2 + 4 * tt * d * 4 + 2 * tt * LANES * 4 + 6 * tt * d * 4
    dest2 = dest.reshape(steps, 1, tt * TOP_K)
    c = _row_chunks(d)
    out = pl.pallas_call(
        functools.partial(_combine_kernel, tt=tt, c=c),
        grid=(steps,),
        in_specs=[pl.BlockSpec((1, 1, tt * TOP_K), lambda i: (i, 0, 0), memory_space=pltpu.SMEM),
                  pl.BlockSpec((1, 1, tt * TOP_K), lambda i: (jnp.minimum(i + 1, steps - 1), 0, 0),
                               memory_space=pltpu.SMEM),
                  pl.BlockSpec(memory_space=pl.ANY),
                  pl.BlockSpec((tt, LANES), lambda i: (i, 0)),
                  pl.BlockSpec((tt, d), lambda i: (i, 0)),
                  pl.BlockSpec((1, 1, d), gate_map)],
        out_specs=pl.BlockSpec((tt, d), lambda i: (i, 0)),
        out_shape=jax.ShapeDtypeStruct((n, d), F32),
        scratch_shapes=[pltpu.VMEM((2, TOP_K, tt * c, LANES), I32), pltpu.SemaphoreType.DMA((2,))],
        compiler_params=_cparams(1, est),
        name="moe_combine",
    )(dest2, dest2, ys, gates.reshape(n, LANES), x.reshape(n, d), gate)
    return out.reshape(b, t, d)


def _moe(streams, counts, layer, expert_w):
    n_pairs = sum(s[0].shape[0] * s[0].shape[1] for s in streams) * TOP_K
    n_tiles_max = (n_pairs + N_EXPERTS * (EXPERT_TILE - 1)) // EXPERT_TILE
    n_rows_max = n_tiles_max * EXPERT_TILE
    tile_e, n_valid, pad_start, pad_end, padded = _route(counts, n_rows_max)
    c = _row_chunks(streams[0][0].shape[2])
    xs = None
    dests = []
    for x_new, rows, top_e, rank, _, _ in streams:
        n = x_new.shape[0] * x_new.shape[1]
        dests.append(_pair_dest(top_e, rank, pad_start))
        xs = _dispatch(dests[-1], rows.reshape(n * c, LANES), xs, pad_end, padded, n_rows_max, c)
    ys = _experts(xs, tile_e + layer * N_EXPERTS, n_valid, *expert_w)
    return [_combine(dests[i], ys, s[4], s[0], s[5]) for i, s in enumerate(streams)]


def _hyena_in_kernel(xm_ref, xp_ref, xn_ref, sh_ref, sc_ref, g_ref, w_ref, b_ref, cw_ref, cb_ref,
                     x0_ref, z_ref, *, tm, width, chunk):
    i = pl.program_id(1)
    last = pl.num_programs(1) - 1
    xe = jnp.concatenate([xp_ref[0], xm_ref[0], xn_ref[0]], axis=0)
    h = _norm_mod(xe, g_ref[...], sh_ref[0], sc_ref[0]).astype(BF16)
    rows = lax.broadcasted_iota(I32, (tm + 2 * SUBLANES, 1), 0)
    inside = jnp.logical_and(jnp.logical_or(rows >= SUBLANES, i > 0),
                             jnp.logical_or(rows < tm + SUBLANES, i < last))
    n_ext = tm + 2 * SUBLANES

    def conv(col):
        u = _dot(h, w_ref[:, col:col + chunk]) + b_ref[:, col:col + chunk]
        u = jnp.where(inside, u, 0.0)
        prev = pltpu.roll(u, 1, 0)[SUBLANES:SUBLANES + tm]
        nxt = pltpu.roll(u, n_ext - 1, 0)[SUBLANES:SUBLANES + tm]
        cw = cw_ref[:, col:col + chunk]
        return (prev * cw[0:1] + u[SUBLANES:SUBLANES + tm] * cw[1:2] + nxt * cw[2:3]
                + cb_ref[:, col:col + chunk])

    for c in range(width // chunk):
        x0 = conv(c * chunk)
        x1 = conv(width + c * chunk)
        v = conv(2 * width + c * chunk)
        x0_ref[0, :, c * chunk:(c + 1) * chunk] = x0.astype(BF16)
        z_ref[0, :, c * chunk:(c + 1) * chunk] = (v * x1).astype(BF16)


def _hyena_in(x, shift, scale, g, w_in, b_in, conv_w, conv_b):
    b, t, d = x.shape
    n3 = w_in.shape[1]
    width = n3 // 3
    tm = min(ROW_TILE, t)
    blocks8 = tm // SUBLANES
    last8 = t // SUBLANES - 1
    bm = shift.shape[0]
    mod_map = (lambda i, j: (i, 0, 0)) if bm > 1 else (lambda i, j: (0, 0, 0))
    const = lambda shape: pl.BlockSpec(shape, lambda i, j: (0,) * len(shape))
    chunk = 512
    est = 2 * (tm + 16) * d * 4 + 2 * d * n3 * 2 + 4 * tm * width * 2 + 10 * (tm + 16) * chunk * 4 + 4 * tm * d * 4
    return pl.pallas_call(
        functools.partial(_hyena_in_kernel, tm=tm, width=width, chunk=chunk),
        grid=(b, t // tm),
        in_specs=[pl.BlockSpec((1, tm, d), lambda i, j: (i, j, 0)),
                  pl.BlockSpec((1, SUBLANES, d), lambda i, j: (i, jnp.maximum(j * blocks8 - 1, 0), 0)),
                  pl.BlockSpec((1, SUBLANES, d), lambda i, j: (i, jnp.minimum((j + 1) * blocks8, last8), 0)),
                  pl.BlockSpec((1, 1, d), mod_map), pl.BlockSpec((1, 1, d), mod_map),
                  const((1, d)), const((d, n3)), const((1, n3)), const((HY_SHORT, n3)), const((1, n3))],
        out_specs=[pl.BlockSpec((1, tm, width), lambda i, j: (i, j, 0))] * 2,
        out_shape=[jax.ShapeDtypeStruct((b, t, width), BF16)] * 2,
        compiler_params=_cparams(2, est),
        name="hyena_in",
    )(x, x, x, shift, scale, g.reshape(1, d), w_in, b_in.reshape(1, n3), conv_w, conv_b.reshape(1, n3))


def _dft_matrix(length):
    n = 2 * length
    f = jnp.arange(length, dtype=I32)
    ang = ((f[:, None] * f[None, :]) % n).astype(F32) * (2.0 * math.pi / n)
    nyq = jnp.where(f % 2 == 0, 1.0, -1.0).astype(F32)
    msin = (-jnp.sin(ang)).at[0].set(nyq)
    return jnp.concatenate([jnp.cos(ang), msin], axis=0).astype(BF16)


def _filter_kernel(z_ref, w1_ref, b1_ref, fr_ref, w2_ref, b2_ref, w3f_ref, w3b_ref, b3f_ref, b3b_ref,
                   t_ref, dl_ref, wf_ref, a_ref, b_ref, c_ref, d_ref, *, length):
    freq = fr_ref[...]
    a1 = jnp.sin(freq * (_dot3(z_ref[...], w1_ref[...]) + b1_ref[...]))
    a2 = jnp.sin(freq * (_dot3(a1, w2_ref[...]) + b2_ref[...]))
    decay = jnp.exp(-t_ref[...] * dl_ref[...])
    rows = lax.broadcasted_iota(I32, (length, 1), 0)
    first = rows == 0
    hf = (_dot3(a2, w3f_ref[...]) + b3f_ref[...]) * decay
    hb = jnp.where(first, 0.0, (_dot3(a2, w3b_ref[...]) + b3b_ref[...]) * decay)
    inv = 1.0 / (jnp.sum(jnp.abs(hf), axis=0, keepdims=True) + jnp.sum(jnp.abs(hb), axis=0, keepdims=True) + EPS)
    hf = hf * inv
    hb = hb * inv
    p_hi, p_lo = _split(hf + hb)
    q_hi, q_lo = _split(hf - hb)
    wf = wf_ref[...]
    sp = _dot(wf, p_hi) + _dot(wf, p_lo)
    ki = _dot(wf[length:], q_hi) + _dot(wf[length:], q_lo)
    kr = sp[:length]
    knyq = sp[length:length + 1]
    n = 2.0 * length
    sc = jnp.where(first, 1.0 / n, 2.0 / n)
    a_ref[...] = kr * sc
    b_ref[...] = jnp.where(first, 0.0, -ki * sc)
    c_ref[...] = jnp.where(first, 0.0, ki * sc)
    d_ref[...] = jnp.where(first, knyq * (1.0 / n), kr * sc)


def _hyena_filter_spectrum(length, w1, b1, w2, b2, w3, b3, sin_freq, wf):
    width = w3.shape[1] // 2
    ffn = w1.shape[1]
    t = jnp.linspace(0.0, 1.0, length, dtype=F32)[:, None]
    w = 2.0 * math.pi * jnp.arange(length, dtype=F32)[:, None] / length
    f = jnp.linspace(1e-4, HY_BANDS - 1, HY_BANDS, dtype=F32)
    z = jnp.concatenate([t, jnp.cos(f * w), -jnp.sin(f * w)], axis=-1)
    zp = jnp.zeros((length, LANES), F32).at[:, :HY_EMB].set(z)
    pad2 = lambda m: jnp.zeros((LANES, m.shape[1] if m.shape[1] > LANES else LANES), F32).at[:m.shape[0], :m.shape[1]].set(m)
    padv = lambda v: jnp.zeros((1, LANES), F32).at[0, :v.shape[0]].set(v)
    max_decay = math.log(HY_TARGET) / HY_FAST_DECAY
    min_decay = math.log(HY_TARGET) / HY_SLOW_DECAY
    absdelta = jnp.abs(jnp.linspace(min_decay, max_decay, width, dtype=F32)).reshape(1, width)
    tc = LANES
    nct = width // tc
    const = lambda shape: pl.BlockSpec(shape, lambda j: (0,) * len(shape))
    w3p = pad2(w3)
    b3r = b3.reshape(1, 2 * width)
    plane = pl.BlockSpec((length, tc), lambda j: (0, j))
    est = 2 * length * length * 2 + 2 * 4 * length * tc * 4 + 24 * length * tc * 4
    return pl.pallas_call(
        functools.partial(_filter_kernel, length=length),
        grid=(nct,),
        in_specs=[const((length, LANES)), const((LANES, LANES)), const((1, LANES)), const((1, LANES)),
                  const((LANES, LANES)), const((1, LANES)),
                  pl.BlockSpec((LANES, tc), lambda j: (0, j)), pl.BlockSpec((LANES, tc), lambda j: (0, nct + j)),
                  pl.BlockSpec((1, tc), lambda j: (0, j)), pl.BlockSpec((1, tc), lambda j: (0, nct + j)),
                  const((length, 1)), pl.BlockSpec((1, tc), lambda j: (0, j)),
                  pl.BlockSpec((2 * length, length), lambda j: (0, 0), pipeline_mode=pl.Buffered(1))],
        out_specs=[plane] * 4,
        out_shape=[jax.ShapeDtypeStruct((length, width), F32)] * 4,
        compiler_params=_cparams(1, est),
        name="hyena_filter",
    )(zp, pad2(w1), padv(b1), padv(sin_freq), pad2(w2), padv(b2), w3p, w3p, b3r, b3r, t, absdelta, wf)


def _conv_fwd_kernel(z_ref, a_ref, b_ref, c_ref, d_ref, wf_ref, y_ref, *, length, chunk):
    z = z_ref[0]

    def body(s, carry):
        f0 = pl.multiple_of(s * chunk, chunk)
        vr = _dot(wf_ref[pl.ds(f0, chunk), :], z)
        vi = _dot(wf_ref[pl.ds(length + f0, chunk), :], z)
        sl = pl.ds(f0, chunk)
        y_ref[0, sl, :] = (vr * a_ref[sl, :] + vi * b_ref[sl, :]).astype(BF16)
        y_ref[0, pl.ds(length + f0, chunk), :] = (vr * c_ref[sl, :] + vi * d_ref[sl, :]).astype(BF16)
        return carry

    lax.fori_loop(0, length // chunk, body, 0, unroll=True)


def _conv_inv_kernel(y_ref, wi_ref, z_ref, x0_ref, db_ref, o_ref, *, length, chunk):
    yf = y_ref[0]

    def body(s, carry):
        sl = pl.ds(pl.multiple_of(s * chunk, chunk), chunk)
        y = _dot(wi_ref[sl, :], yf)
        o_ref[0, sl, :] = (x0_ref[0, sl, :].astype(F32)
                           * (y + z_ref[0, sl, :].astype(F32) * db_ref[...])).astype(BF16)
        return carry

    lax.fori_loop(0, length // chunk, body, 0, unroll=True)


def _long_conv(z, x0, planes, d_bias, wf, wi):
    b, length, width = z.shape
    tc = CONV_COLS
    chunk = min(CONV_CHUNK, length)
    single = pl.Buffered(1)
    plane = pl.BlockSpec((length, tc), lambda j, i: (0, j), pipeline_mode=single)
    est_f = 2 * length * length * 2 + 4 * length * tc * 4 + 2 * length * tc * 2 + 4 * length * tc * 2 + 8 * chunk * tc * 4
    spec = pl.pallas_call(
        functools.partial(_conv_fwd_kernel, length=length, chunk=chunk),
        grid=(width // tc, b),
        in_specs=[pl.BlockSpec((1, length, tc), lambda j, i: (i, 0, j))] + [plane] * 4
        + [pl.BlockSpec((2 * length, length), lambda j, i: (0, 0), pipeline_mode=single)],
        out_specs=pl.BlockSpec((1, 2 * length, tc), lambda j, i: (i, 0, j)),
        out_shape=jax.ShapeDtypeStruct((b, 2 * length, width), BF16),
        compiler_params=_cparams(2, est_f),
        name="hyena_conv_fwd",
    )(z, *planes, wf)
    col = lambda rows: pl.BlockSpec((1, rows, tc), lambda i, j: (i, 0, j))
    est_i = 2 * length * length * 2 + 4 * length * tc * 2 + 6 * length * tc * 2 + 8 * chunk * tc * 4
    return pl.pallas_call(
        functools.partial(_conv_inv_kernel, length=length, chunk=chunk),
        grid=(b, width // tc),
        in_specs=[col(2 * length),
                  pl.BlockSpec((length, 2 * length), lambda i, j: (0, 0), pipeline_mode=single),
                  col(length), col(length), pl.BlockSpec((1, tc), lambda i, j: (0, j))],
        out_specs=col(length),
        out_shape=jax.ShapeDtypeStruct((b, length, width), BF16),
        compiler_params=_cparams(2, est_i),
        name="hyena_conv_inv",
    )(spec, wi, z, x0, d_bias.reshape(1, width))


def _hyena_mixer(x, shift, scale, g, hy, wf, wi):
    w_in, b_in, conv_w, conv_b, w1, b1, w2, b2, w3, b3, sin_freq, d_bias = hy
    x0, z = _hyena_in(x, shift, scale, g, w_in, b_in, conv_w, conv_b)
    planes = _hyena_filter_spectrum(x.shape[1], w1, b1, w2, b2, w3, b3, sin_freq, wf)
    return _long_conv(z, x0, planes, d_bias, wf, wi)


def kernel(x, c, ctx, c_ctx, w_mod, b_mod, norm1_g, norm2_g, attn_w_qkv, attn_q_gain, attn_k_gain, attn_w_o, hy_w_in, hy_b_in, hy_conv_w, hy_conv_b, hy_filt_w1, hy_filt_b1, hy_filt_w2, hy_filt_b2, hy_filt_w3, hy_filt_b3, hy_sin_freq, hy_d_bias, hy_w_out, hy_b_out, moe_router_w, moe_router_b, moe_w_gu, moe_b_gu, moe_w_down, moe_b_down):
    b, length, d = x.shape
    n_ctx = ctx.shape[1]
    depth = w_mod.shape[0]
    n_q_heads = d // HEAD_DIM

    r_pad = -(-(b + 1) // SUBLANES) * SUBLANES
    cc = jnp.zeros((r_pad, d), F32).at[:b].set(c).at[b].set(c_ctx)
    mod_all = _modulation(cc, w_mod, b_mod).reshape(depth, r_pad, 6, d)

    rope_tabs = _rope_tables(length, n_q_heads + N_KV_HEADS)
    n_exp, _, f2 = moe_w_gu.shape[1:]
    b_gu = moe_b_gu.reshape(depth * n_exp, 1, f2)
    expert_w = (moe_w_gu.reshape(depth * n_exp, d, f2), moe_w_down.reshape(depth * n_exp, f2 // 2, d),
                b_gu[..., 0::2], b_gu[..., 1::2], moe_b_down.reshape(depth * n_exp, 1, d))
    dft = {}

    def dft_pair(n):
        if n not in dft:
            wf = _dft_matrix(n)
            dft[n] = (wf, wf.T)
        return dft[n]

    for i in range(depth):
        j = i // N_MIXERS
        is_attn = i % N_MIXERS == 0
        ctx_live = any(m % N_MIXERS == 0 for m in range(i + 1, depth))
        mod = [mod_all[i, :b, s].reshape(b, 1, d) for s in range(6)]
        mod_c = [mod_all[i, b, s].reshape(1, 1, d) for s in range(6)]
        if is_attn:
            w_qkv = attn_w_qkv[j].astype(BF16)
            n_q = n_q_heads * HEAD_DIM
            q, k, v = _qkv_project(x, mod[0], mod[1], norm1_g[i], w_qkv, attn_q_gain[j], attn_k_gain[j],
                                   rope_tabs, True)
            w_c = w_qkv if ctx_live else w_qkv[:, n_q:]
            ctx_out = _qkv_project(ctx, mod_c[0], mod_c[1], norm1_g[i], w_c, attn_q_gain[j], attn_k_gain[j],
                                   None, ctx_live)
            k_c, v_c = ctx_out[-2], ctx_out[-1]
            a = _attention(q, [(k, v), (k_c, v_c)])
            a_c = _attention(ctx_out[0], [(k_c, v_c)]) if ctx_live else None
            w_out, b_out = attn_w_o[j].astype(BF16), jnp.zeros((d,), F32)
        else:
            hy = (hy_w_in[j].astype(BF16), hy_b_in[j], hy_conv_w[j], hy_conv_b[j], hy_filt_w1[j], hy_filt_b1[j],
                  hy_filt_w2[j], hy_filt_b2[j], hy_filt_w3[j], hy_filt_b3[j], hy_sin_freq[j], hy_d_bias[j])
            a = _hyena_mixer(x, mod[0], mod[1], norm1_g[i], hy, *dft_pair(length))
            a_c = _hyena_mixer(ctx, mod_c[0], mod_c[1], norm1_g[i], hy, *dft_pair(n_ctx)) if ctx_live else None
            w_out, b_out = hy_w_out[j].astype(BF16), hy_b_out[j]

        post = _post_mixer(a, w_out, b_out, x, mod[2], mod[3], mod[4], norm2_g[i],
                           moe_router_w[i], moe_router_b[i], jnp.zeros((1, LANES), F32))
        streams = [tuple(post[:5]) + (mod[5],)]
        counts = post[5]
        if ctx_live:
            post = _post_mixer(a_c, w_out, b_out, ctx, mod_c[2], mod_c[3], mod_c[4], norm2_g[i],
                               moe_router_w[i], moe_router_b[i], counts)
            streams.append(tuple(post[:5]) + (mod_c[5],))
            counts = post[5]
        outs = _moe(streams, counts, i, expert_w)
        x = outs[0]
        if ctx_live:
            ctx = outs[1]
    return x
```

```python
import functools
import math

import jax
import jax.numpy as jnp
from jax import lax
from jax.experimental import pallas as pl
from jax.experimental.pallas import tpu as pltpu

F32 = jnp.float32
BF16 = jnp.bfloat16
I32 = jnp.int32

LANES = 128
SUBLANES = 8
VMEM_BYTES = 64 * 2**20
VMEM_CAP = VMEM_BYTES - 8 * 2**20

GRID_W = 64
EPS = 1e-6
HEAD_DIM = 64
N_KV_HEADS = 4
ROPE_THETA = 10000.0
ROPE_PAIRS = HEAD_DIM // 4
HY_SHORT = 3
HY_EMB = 33
HY_BANDS = (HY_EMB - 1) // 2
HY_FAST_DECAY = 0.3
HY_SLOW_DECAY = 1.5
HY_TARGET = 1e-2
N_EXPERTS = 32
TOP_K = 4
SWIGLU_LIMIT = 7.0
SWIGLU_ALPHA = 1.702
N_MIXERS = 2
NEG_BIG = -1e30

ROW_TILE = 512
ATT_Q_TILE = 256
EXPERT_TILE = 512
TOKEN_TILE = 256
CONV_COLS = 256
CONV_CHUNK = 512


def _cparams(n_axes, vmem_estimate):
    limit = int(min(max(vmem_estimate * 5 // 4, 32 * 2**20), VMEM_CAP))
    return pltpu.CompilerParams(dimension_semantics=("arbitrary",) * n_axes, vmem_limit_bytes=limit)


def _split(a):
    hi = a.astype(BF16)
    lo = (a - hi.astype(F32)).astype(BF16)
    return hi, lo


def _dot(a, b):
    return jnp.dot(a, b, preferred_element_type=F32)


def _dot3(a, b):
    a_hi, a_lo = _split(a)
    b_hi, b_lo = _split(b)
    return _dot(a_hi, b_hi) + _dot(a_lo, b_hi) + _dot(a_hi, b_lo)


def _norm_mod(x, g, shift, scale):
    y = x * lax.rsqrt(jnp.mean(x * x, axis=-1, keepdims=True) + EPS)
    return (y * g) * (1.0 + scale) + shift


def _pack(a, b):
    ua = lax.bitcast_convert_type(a.astype(BF16).astype(F32), I32)
    ub = lax.bitcast_convert_type(b.astype(BF16).astype(F32), I32)
    return (ua & jnp.int32(-65536)) | lax.shift_right_logical(ub, jnp.int32(16))


def _unpack(p):
    a = lax.bitcast_convert_type(p & jnp.int32(-65536), F32)
    b = lax.bitcast_convert_type(lax.shift_left(p, jnp.int32(16)), F32)
    return a, b


def _row_chunks(d):
    return d // (2 * LANES)


def _store_token_rows(ref, lead, y):
    m, d = y.shape
    c = _row_chunks(d)
    for j in range(c):
        lo = y[:, LANES * j:LANES * (j + 1)]
        hi = y[:, d // 2 + LANES * j:d // 2 + LANES * (j + 1)]
        ref[lead + (pl.ds(j, m, stride=c), slice(None))] = _pack(lo, hi)


def _load_token_rows(ref, lead, m, c):
    halves = [_unpack(ref[lead + (pl.ds(j, m, stride=c), slice(None))]) for j in range(c)]
    return jnp.concatenate([h[0] for h in halves] + [h[1] for h in halves], axis=1)


def _mod_kernel(c_ref, w_ref, b_ref, o_ref):
    c = c_ref[...]
    a = c * jax.nn.sigmoid(c)
    o_ref[0] = _dot3(a, w_ref[0]) + b_ref[0]


def _modulation(cc, w_mod, b_mod):
    depth, d, n = w_mod.shape
    r = cc.shape[0]
    tn = 1536
    return pl.pallas_call(
        _mod_kernel,
        grid=(depth, n // tn),
        in_specs=[
            pl.BlockSpec((r, d), lambda i, j: (0, 0)),
            pl.BlockSpec((1, d, tn), lambda i, j: (i, 0, j)),
            pl.BlockSpec((1, 1, tn), lambda i, j: (i, 0, j)),
        ],
        out_specs=pl.BlockSpec((1, r, tn), lambda i, j: (i, 0, j)),
        out_shape=jax.ShapeDtypeStruct((depth, r, n), F32),
        compiler_params=_cparams(2, 2 * d * tn * 4 + 3 * d * tn * 2),
        name="modulation",
    )(cc, w_mod, b_mod.reshape(depth, 1, n))


def _qkv_kernel(*refs, n_qk, rope, with_q):
    x_ref, sh_ref, sc_ref, g_ref, w_ref, gain_ref, e_ref, et_ref = refs[:8]
    pos = 8
    if rope:
        cos_ref, sa_ref, sb_ref = refs[pos:pos + 3]
        pos += 3
    outs = refs[pos:]
    n_q = n_qk - N_KV_HEADS * HEAD_DIM
    tm = x_ref.shape[1]
    n_sub = 2 if tm % (2 * SUBLANES) == 0 else 1
    for s in range(n_sub):
        rows = slice(s * (tm // n_sub), (s + 1) * (tm // n_sub))
        h = _norm_mod(x_ref[0, rows, :], g_ref[...], sh_ref[0], sc_ref[0]).astype(BF16)
        acc = _dot(h, w_ref[...])
        qk = acc[:, :n_qk]
        ssq = _dot((qk * qk).astype(BF16), e_ref[...])
        r_hi, r_lo = _split(lax.rsqrt(ssq * (1.0 / HEAD_DIM) + EPS))
        y = qk * (_dot(r_hi, et_ref[...]) + _dot(r_lo, et_ref[...])) * gain_ref[...]
        if rope:
            y = (y * cos_ref[rows, :] + pltpu.roll(y, n_qk - ROPE_PAIRS, 1) * sa_ref[rows, :]
                 + pltpu.roll(y, ROPE_PAIRS, 1) * sb_ref[rows, :])
        kv_outs = outs
        if with_q:
            outs[0][0, rows, :] = (y[:, :n_q] * (HEAD_DIM ** -0.5)).astype(BF16)
            kv_outs = outs[1:]
        kv_outs[0][0, rows, :] = y[:, n_q:].astype(BF16)
        kv_outs[1][0, rows, :] = acc[:, n_qk:].astype(BF16)


def _qkv_project(x, shift, scale, g, w, q_gain, k_gain, rope_tabs, with_q):
    b, t, d = x.shape
    kv = N_KV_HEADS * HEAD_DIM
    nc = w.shape[1]
    n_qk = nc - kv
    n_heads = n_qk // HEAD_DIM
    tm = min(ROW_TILE, t)
    gains = ([q_gain] * (n_heads - N_KV_HEADS)) + [k_gain] * N_KV_HEADS
    gain = jnp.concatenate(gains).reshape(1, n_qk).astype(F32)
    head_of_col = jnp.arange(n_qk) // HEAD_DIM
    e = (head_of_col[:, None] == jnp.arange(LANES)[None, :]).astype(BF16)
    et = e.T
    bm = shift.shape[0]
    mod_map = (lambda i, j: (j, 0, 0)) if bm > 1 else (lambda i, j: (0, 0, 0))
    in_specs = [
        pl.BlockSpec((1, tm, d), lambda i, j: (j, i, 0)),
        pl.BlockSpec((1, 1, d), mod_map),
        pl.BlockSpec((1, 1, d), mod_map),
        pl.BlockSpec((1, d), lambda i, j: (0, 0)),
        pl.BlockSpec((d, nc), lambda i, j: (0, 0)),
        pl.BlockSpec((1, n_qk), lambda i, j: (0, 0)),
        pl.BlockSpec((n_qk, LANES), lambda i, j: (0, 0)),
        pl.BlockSpec((LANES, n_qk), lambda i, j: (0, 0)),
    ]
    args = [x, shift, scale, g.reshape(1, d), w, gain, e, et]
    rope = rope_tabs is not None
    if rope:
        in_specs += [pl.BlockSpec((tm, n_qk), lambda i, j: (i, 0))] * 3
        args += list(rope_tabs)
    out_shape, out_specs = [], []
    widths = ([n_qk - kv] if with_q else []) + [kv, kv]
    for wd in widths:
        out_shape.append(jax.ShapeDtypeStruct((b, t, wd), BF16))
        out_specs.append(pl.BlockSpec((1, tm, wd), lambda i, j: (j, i, 0)))
    est = 2 * tm * d * 4 + 2 * d * nc * 2 + 6 * tm * n_qk * 4 + 8 * tm * nc * 4
    return pl.pallas_call(
        functools.partial(_qkv_kernel, n_qk=n_qk, rope=rope, with_q=with_q),
        grid=(t // tm, b),
        in_specs=in_specs,
        out_specs=out_specs,
        out_shape=out_shape,
        compiler_params=_cparams(2, est),
        name="qkv_project",
    )(*args)


def _rope_tables(n_tokens, n_heads):
    n_rows = n_tokens // GRID_W
    row = jnp.repeat(jnp.arange(n_rows, dtype=F32), GRID_W)
    col = jnp.tile(jnp.arange(GRID_W, dtype=F32), n_rows)
    inv_freq = ROPE_THETA ** (-jnp.arange(ROPE_PAIRS, dtype=F32) / ROPE_PAIRS)
    ar, ac = row[:, None] * inv_freq, col[:, None] * inv_freq
    zero = jnp.zeros_like(ar)
    cos = jnp.concatenate([jnp.cos(ar), jnp.cos(ar), jnp.cos(ac), jnp.cos(ac)], axis=1)
    sin_a = jnp.concatenate([-jnp.sin(ar), zero, -jnp.sin(ac), zero], axis=1)
    sin_b = jnp.concatenate([zero, jnp.sin(ar), zero, jnp.sin(ac)], axis=1)
    return tuple(jnp.tile(tab, (1, n_heads)) for tab in (cos, sin_a, sin_b))


def _attn_kernel(*refs, n_seg, tq):
    q_ref = refs[0]
    kv_refs = refs[1:1 + 2 * n_seg]
    o_ref = refs[1 + 2 * n_seg]
    q_per_kv = q_ref.shape[2] // (N_KV_HEADS * HEAD_DIM)
    n_halves = 2 if tq % (2 * SUBLANES) == 0 else 1
    rows_u = tq // n_halves
    units = [(g, half) for g in range(N_KV_HEADS) for half in range(n_halves)]

    def scores_of(unit):
        g, half = unit
        r0 = half * rows_u
        qs = jnp.concatenate([q_ref[0, r0:r0 + rows_u, (g * q_per_kv + j) * HEAD_DIM:(g * q_per_kv + j + 1) * HEAD_DIM]
                              for j in range(q_per_kv)], axis=0)
        ksl = slice(g * HEAD_DIM, (g + 1) * HEAD_DIM)
        return [lax.dot_general(qs, kv_refs[2 * s][0, :, ksl], (((1,), (1,)), ((), ())),
                                preferred_element_type=F32) for s in range(n_seg)]

    def finish(unit, scores):
        g, half = unit
        r0 = half * rows_u
        ksl = slice(g * HEAD_DIM, (g + 1) * HEAD_DIM)
        m = functools.reduce(jnp.maximum, [jnp.max(sc, axis=-1, keepdims=True) for sc in scores])
        probs = [jnp.exp(sc - m) for sc in scores]
        denom = functools.reduce(jnp.add, [jnp.sum(p, axis=-1, keepdims=True) for p in probs])
        o = functools.reduce(jnp.add, [_dot(probs[s].astype(BF16), kv_refs[2 * s + 1][0, :, ksl])
                                       for s in range(n_seg)])
        o = o / denom
        for j in range(q_per_kv):
            h = g * q_per_kv + j
            o_ref[0, r0:r0 + rows_u, h * HEAD_DIM:(h + 1) * HEAD_DIM] = o[j * rows_u:(j + 1) * rows_u].astype(BF16)

    pending = None
    for unit in units:
        scores = scores_of(unit)
        if pending is not None:
            finish(*pending)
        pending = (unit, scores)
    finish(*pending)


def _attention(q, segments):
    b, t, dq = q.shape
    tq = min(ATT_Q_TILE, t)
    in_specs = [pl.BlockSpec((1, tq, dq), lambda i, j: (i, j, 0))]
    args = [q]
    t_keys = 0
    for k, v in segments:
        ts, kv = k.shape[1], k.shape[2]
        in_specs += [pl.BlockSpec((1, ts, kv), lambda i, j: (i, 0, 0))] * 2
        args += [k, v]
        t_keys += ts
    q_rows = tq * dq // (N_KV_HEADS * HEAD_DIM)
    est = 4 * tq * dq * 2 + 8 * t_keys * N_KV_HEADS * HEAD_DIM * 2 + 4 * q_rows * t_keys * 4
    return pl.pallas_call(
        functools.partial(_attn_kernel, n_seg=len(segments), tq=tq),
        grid=(b, t // tq),
        in_specs=in_specs,
        out_specs=pl.BlockSpec((1, tq, dq), lambda i, j: (i, j, 0)),
        out_shape=jax.ShapeDtypeStruct((b, t, dq), BF16),
        compiler_params=_cparams(2, est),
        name="attention",
    )(*args)


def _post_kernel(a_ref, w_ref, bias_ref, x_ref, gate_ref, sh_ref, sc_ref, g_ref, rw_ref, rb_ref, tri_ref, base_ref,
                 xo_ref, hp_ref, te_ref, tr_ref, tg_ref, cnt_ref, run_ref, *, n_sub):
    @pl.when(jnp.logical_and(pl.program_id(0) == 0, pl.program_id(1) == 0))
    def _():
        run_ref[...] = base_ref[...]

    tm = x_ref.shape[1]
    rs = tm // n_sub
    c = hp_ref.shape[1] // tm
    running = run_ref[...]
    for s in range(n_sub):
        rows = slice(s * rs, (s + 1) * rs)
        o = _dot(a_ref[0, rows, :], w_ref[...]) + bias_ref[...]
        x_new = x_ref[0, rows, :] + gate_ref[0] * o
        xo_ref[0, rows, :] = x_new
        h = _norm_mod(x_new, g_ref[...], sh_ref[0], sc_ref[0])
        d = h.shape[1]
        for j in range(c):
            hp_ref[0, pl.ds(s * rs * c + j, rs, stride=c), :] = _pack(
                h[:, LANES * j:LANES * (j + 1)], h[:, d // 2 + LANES * j:d // 2 + LANES * (j + 1)])
        h_hi, h_lo = _split(h)
        wide = _dot(h_hi, rw_ref[...])
        logits = (wide[:, :LANES] + wide[:, LANES:] + _dot(h_lo, rw_ref[:, :LANES])
                  + rb_ref[...])
        lane = lax.broadcasted_iota(I32, logits.shape, 1)
        e_out = jnp.zeros(logits.shape, I32)
        g_out = jnp.zeros(logits.shape, F32)
        picked = jnp.zeros(logits.shape, F32)
        top = None
        total = None
        exps = []
        for k in range(TOP_K):
            m = jnp.max(logits, axis=1, keepdims=True)
            idx = jnp.min(jnp.where(logits == m, lane, LANES), axis=1, keepdims=True)
            hit = lane == idx
            logits = jnp.where(hit, -jnp.inf, logits)
            e_out = jnp.where(lane == k, idx, e_out)
            picked = jnp.where(hit, 1.0, picked)
            top = m if top is None else top
            ex = jnp.exp(m - top)
            exps.append(ex)
            total = ex if total is None else total + ex
        before = running + _dot(tri_ref[...], picked.astype(BF16))
        for k in range(TOP_K):
            g_out = jnp.where(lane == k, exps[k] / total, g_out)
        running = running + jnp.sum(picked, axis=0, keepdims=True)
        te_ref[0, rows, :] = e_out
        tr_ref[0, rows, :] = jnp.where(picked > 0.0, before, 0.0).astype(I32)
        tg_ref[0, rows, :] = g_out
    run_ref[...] = running
    cnt_ref[...] = running


def _post_mixer(a, w, bias, x, gate, shift, scale, g, router_w, router_b, base):
    b, t, d = x.shape
    k_in = a.shape[2]
    c = _row_chunks(d)
    tm = min(ROW_TILE, t)
    n_sub = 1
    rs = tm // n_sub
    bm = gate.shape[0]
    mod_map = (lambda i, j: (i, 0, 0)) if bm > 1 else (lambda i, j: (0, 0, 0))
    rw = jnp.concatenate(_split(jnp.zeros((d, LANES), F32).at[:, :N_EXPERTS].set(router_w)), axis=1)
    rb = jnp.full((1, LANES), NEG_BIG, F32).at[0, :N_EXPERTS].set(router_b)
    tri = (jnp.arange(rs)[:, None] > jnp.arange(rs)[None, :]).astype(BF16)
    row_spec = lambda wd: pl.BlockSpec((1, tm, wd), lambda i, j: (i, j, 0))
    const = lambda shape: pl.BlockSpec(shape, lambda i, j: (0,) * len(shape))
    est = 2 * tm * (k_in * 2 + d * 4 * 2 + d * 2 + 3 * LANES * 4) + 2 * k_in * d * 2 + 8 * tm * d * 4
    return pl.pallas_call(
        functools.partial(_post_kernel, n_sub=n_sub),
        grid=(b, t // tm),
        in_specs=[row_spec(k_in), const((k_in, d)), const((1, d)), row_spec(d),
                  pl.BlockSpec((1, 1, d), mod_map), pl.BlockSpec((1, 1, d), mod_map),
                  pl.BlockSpec((1, 1, d), mod_map), const((1, d)), const((d, 2 * LANES)), const((1, LANES)),
                  const((rs, rs)), const((1, LANES))],
        out_specs=[row_spec(d), pl.BlockSpec((1, tm * c, LANES), lambda i, j: (i, j, 0)),
                   row_spec(LANES), row_spec(LANES), row_spec(LANES), const((1, LANES))],
        out_shape=[jax.ShapeDtypeStruct((b, t, d), F32), jax.ShapeDtypeStruct((b, t * c, LANES), I32),
                   jax.ShapeDtypeStruct((b, t, LANES), I32), jax.ShapeDtypeStruct((b, t, LANES), I32),
                   jax.ShapeDtypeStruct((b, t, LANES), F32), jax.ShapeDtypeStruct((1, LANES), F32)],
        scratch_shapes=[pltpu.VMEM((1, LANES), F32)],
        compiler_params=_cparams(2, est),
        name="post_mixer",
    )(a, w, bias.reshape(1, d), x, gate, shift, scale, g.reshape(1, d), rw, rb, tri, base)


def _route(counts, n_rows_max):
    counts = counts[0, :N_EXPERTS].astype(I32)
    padded = (counts + EXPERT_TILE - 1) // EXPERT_TILE * EXPERT_TILE
    pad_end = jnp.cumsum(padded)
    pad_start = pad_end - padded
    tile_start = jnp.arange(n_rows_max // EXPERT_TILE, dtype=I32) * EXPERT_TILE
    tile_e = jnp.minimum(jnp.sum((pad_end[None, :] <= tile_start[:, None]).astype(I32), axis=1), N_EXPERTS - 1)
    n_valid = (pad_end[-1] // EXPERT_TILE).astype(I32).reshape(1)
    return tile_e.astype(I32), n_valid, pad_start.astype(I32), pad_end.astype(I32), padded.astype(I32)


def _pair_dest(top_e, rank, pad_start, c):
    e = top_e[..., :TOP_K].reshape(-1, TOP_K)
    row = rank[..., :N_EXPERTS].reshape(-1, 1, N_EXPERTS) + pad_start
    onehot = e[..., None] == jnp.arange(N_EXPERTS, dtype=I32)
    return (jnp.sum(jnp.where(onehot, row, 0), axis=-1) * c).reshape(-1).astype(I32)


def _dispatch_kernel(pend_ref, padded_ref, dest_ref, h_ref, *rest, tt, c, first):
    if first:
        xs_hbm, zbuf, zsem, stage, row_sems = rest
    else:
        _, xs_hbm, stage, row_sems = rest
    i = pl.program_id(0)

    if first:
        @pl.when(i == 0)
        def _():
            zbuf[...] = jnp.zeros(zbuf.shape, zbuf.dtype)

            def fill(e, carry):
                @pl.when(padded_ref[e] > 0)
                def _():
                    start = pl.multiple_of((pend_ref[e] - EXPERT_TILE) * c, EXPERT_TILE * c)
                    cp = pltpu.make_async_copy(zbuf, xs_hbm.at[pl.ds(start, EXPERT_TILE * c)], zsem)
                    cp.start()
                    cp.wait()
                return carry

            lax.fori_loop(0, N_EXPERTS, fill, 0)

            def fill_tail(t, carry):
                cp = pltpu.make_async_copy(
                    zbuf, xs_hbm.at[pl.ds(pl.multiple_of(t * (EXPERT_TILE * c), EXPERT_TILE * c), EXPERT_TILE * c)],
                    zsem)
                cp.start()
                cp.wait()
                return carry

            lax.fori_loop(pend_ref[N_EXPERTS - 1] // EXPERT_TILE, xs_hbm.shape[0] // (EXPERT_TILE * c),
                          fill_tail, 0)

    slot = i % 2
    stage[slot] = h_ref[...]

    def start(r, carry):
        for k in range(TOP_K):
            dst = pl.multiple_of(dest_ref[0, 0, TOP_K * r + k], c)
            pltpu.make_async_copy(stage.at[slot, pl.ds(pl.multiple_of(r * c, c), c)], xs_hbm.at[pl.ds(dst, c)],
                                  row_sems.at[slot]).start(priority=k % 2)
        return carry

    lax.fori_loop(0, tt, start, 0, unroll=8)

    def wait_all(s):
        for _ in range(TOP_K):
            pltpu.make_async_copy(stage.at[s], stage.at[s], row_sems.at[s]).wait()

    @pl.when(i > 0)
    def _():
        wait_all(1 - slot)

    @pl.when(i == pl.num_programs(0) - 1)
    def _():
        wait_all(slot)


def _dispatch(dest, h_rows, xs, pad_end, padded, n_rows, c):
    n = h_rows.shape[0] // c
    tt = min(TOKEN_TILE, n)
    first = xs is None
    in_specs = [pl.BlockSpec((1, 1, tt * TOP_K), lambda i, pe, pd: (i, 0, 0), memory_space=pltpu.SMEM),
                pl.BlockSpec((tt * c, LANES), lambda i, pe, pd: (i, 0))]
    args = [pad_end, padded, dest.reshape(n // tt, 1, tt * TOP_K), h_rows]
    scratch = [pltpu.VMEM((2, tt * c, LANES), I32), pltpu.SemaphoreType.DMA((2,))]
    if first:
        scratch = [pltpu.VMEM((EXPERT_TILE * c, LANES), I32), pltpu.SemaphoreType.DMA] + scratch
    else:
        in_specs.append(pl.BlockSpec(memory_space=pl.ANY))
        args.append(xs)
    return pl.pallas_call(
        functools.partial(_dispatch_kernel, tt=tt, c=c, first=first),
        grid_spec=pltpu.PrefetchScalarGridSpec(
            num_scalar_prefetch=2, grid=(n // tt,), in_specs=in_specs,
            out_specs=pl.BlockSpec(memory_space=pl.ANY), scratch_shapes=scratch),
        out_shape=jax.ShapeDtypeStruct((n_rows * c, LANES), I32),
        input_output_aliases={} if first else {4: 0},
        compiler_params=pltpu.CompilerParams(dimension_semantics=("arbitrary",), has_side_effects=True),
        name="moe_dispatch",
    )(*args)


def _expert_kernel(te_ref, nv_ref, xs_ref, wgu_ref, wd_ref, bg_ref, bu_ref, bd_ref, sel_ref, ys_ref,
                   wg_s, wu_s, wd_s):
    i = pl.program_id(0)
    valid = i < nv_ref[0]
    new_expert = jnp.logical_or(i == 0, te_ref[i] != te_ref[jnp.maximum(i - 1, 0)])

    @pl.when(jnp.logical_and(valid, new_expert))
    def _():
        for blk in range(wgu_ref.shape[2] // (2 * LANES)):
            cols = wgu_ref[0, :, 2 * LANES * blk:2 * LANES * (blk + 1)].astype(BF16)
            r = _dot(cols, sel_ref[...])
            wg_s[:, LANES * blk:LANES * (blk + 1)] = r[:, :LANES].astype(BF16)
            wu_s[:, LANES * blk:LANES * (blk + 1)] = r[:, LANES:].astype(BF16)
        wd_s[...] = wd_ref[0].astype(BF16)

    @pl.when(valid)
    def _():
        c = _row_chunks(wg_s.shape[0])
        xb = _load_token_rows(xs_ref, (), xs_ref.shape[0] // c, c).astype(BF16)
        g = jnp.minimum(_dot(xb, wg_s[...]) + bg_ref[0], SWIGLU_LIMIT)
        u = jnp.clip(_dot(xb, wu_s[...]) + bu_ref[0], -SWIGLU_LIMIT, SWIGLU_LIMIT)
        act = (u + 1.0) * g * jax.nn.sigmoid(SWIGLU_ALPHA * g)
        y = _dot(act.astype(BF16), wd_s[...]) + bd_ref[0]
        _store_token_rows(ys_ref, (), y)

    @pl.when(jnp.logical_not(valid))
    def _():
        ys_ref[...] = jnp.zeros(ys_ref.shape, ys_ref.dtype)


def _gate_up_selector():
    col = jnp.arange(2 * LANES)
    src = jnp.where(col < LANES, 2 * col, 2 * (col - LANES) + 1)
    return (col[:, None] == src[None, :]).astype(BF16)


def _experts(xs, tile_e, n_valid, w_gu, w_down, bg, bu, bd):
    d, f2 = w_gu.shape[1:]
    f = f2 // 2
    c = _row_chunks(d)
    rows = xs.shape[0] // c
    tm = EXPERT_TILE
    tile_map = lambda i, te, nv: (jnp.minimum(i, nv[0] - 1), 0)
    w_map = lambda i, te, nv: (te[jnp.minimum(i, nv[0] - 1)], 0, 0)
    est = 4 * tm * d * 2 + 2 * 3 * d * f * 4 + 3 * d * f * 2 + 8 * tm * f * 4
    return pl.pallas_call(
        _expert_kernel,
        grid_spec=pltpu.PrefetchScalarGridSpec(
            num_scalar_prefetch=2,
            grid=(rows // tm,),
            in_specs=[pl.BlockSpec((tm * c, LANES), tile_map),
                      pl.BlockSpec((1, d, f2), w_map), pl.BlockSpec((1, f, d), w_map),
                      pl.BlockSpec((1, 1, f), w_map), pl.BlockSpec((1, 1, f), w_map),
                      pl.BlockSpec((1, 1, d), w_map),
                      pl.BlockSpec((2 * LANES, 2 * LANES), lambda i, te, nv: (0, 0))],
            out_specs=pl.BlockSpec((tm * c, LANES), lambda i, te, nv: (i, 0)),
            scratch_shapes=[pltpu.VMEM((d, f), BF16), pltpu.VMEM((d, f), BF16), pltpu.VMEM((f, d), BF16)],
        ),
        out_shape=jax.ShapeDtypeStruct((rows * c, LANES), I32),
        compiler_params=_cparams(1, est),
        name="moe_experts",
    )(tile_e, n_valid, xs, w_gu, w_down, bg, bu, bd, _gate_up_selector())


def _combine_kernel(dest_ref, dest_next_ref, ys_hbm, gates_ref, x_ref, gate_ref, o_ref, buf, sems, *, tt, c):
    i = pl.program_id(0)
    slot = i % 2

    def issue(idx_ref, s):
        def body(r, carry):
            for k in range(TOP_K):
                src = pl.multiple_of(idx_ref[0, 0, TOP_K * r + k], c)
                pltpu.make_async_copy(ys_hbm.at[pl.ds(src, c)],
                                      buf.at[s, k, pl.ds(pl.multiple_of(r * c, c), c)],
                                      sems.at[s]).start(priority=k % 2)
            return carry

        lax.fori_loop(0, tt, body, 0, unroll=8)

    @pl.when(i == 0)
    def _():
        issue(dest_ref, 0)

    @pl.when(i + 1 < pl.num_programs(0))
    def _():
        issue(dest_next_ref, 1 - slot)

    pltpu.make_async_copy(buf.at[slot], buf.at[slot], sems.at[slot]).wait()
    gates = gates_ref[...]
    y = None
    for k in range(TOP_K):
        yk = _load_token_rows(buf, (slot, k), tt, c) * gates[:, k:k + 1]
        y = yk if y is None else y + yk
    o_ref[...] = x_ref[...] + gate_ref[0] * y


def _combine(dest, ys, gates, x, gate):
    b, t, d = x.shape
    n = b * t
    tt = min(TOKEN_TILE, t)
    steps = n // tt
    steps_per_batch = t // tt
    bm = gate.shape[0]
    gate_map = (lambda i: (i // steps_per_batch, 0, 0)) if bm > 1 else (lambda i: (0, 0, 0))
    est = 2 * TOP_K * tt * d * 2 + 4 * tt * d * 4 + 2 * tt * LANES * 4 + 6 * tt * d * 4
    dest2 = dest.reshape(steps, 1, tt * TOP_K)
    c = _row_chunks(d)
    out = pl.pallas_call(
        functools.partial(_combine_kernel, tt=tt, c=c),
        grid=(steps,),
        in_specs=[pl.BlockSpec((1, 1, tt * TOP_K), lambda i: (i, 0, 0), memory_space=pltpu.SMEM),
                  pl.BlockSpec((1, 1, tt * TOP_K), lambda i: (jnp.minimum(i + 1, steps - 1), 0, 0),
                               memory_space=pltpu.SMEM),
                  pl.BlockSpec(memory_space=pl.ANY),
                  pl.BlockSpec((tt, LANES), lambda i: (i, 0)),
                  pl.BlockSpec((tt, d), lambda i: (i, 0)),
                  pl.BlockSpec((1, 1, d), gate_map)],
        out_specs=pl.BlockSpec((tt, d), lambda i: (i, 0)),
        out_shape=jax.ShapeDtypeStruct((n, d), F32),
        scratch_shapes=[pltpu.VMEM((2, TOP_K, tt * c, LANES), I32), pltpu.SemaphoreType.DMA((2,))],
        compiler_params=_cparams(1, est),
        name="moe_combine",
    )(dest2, dest2, ys, gates.reshape(n, LANES), x.reshape(n, d), gate)
    return out.reshape(b, t, d)


def _moe(streams, counts, layer, expert_w):
    n_pairs = sum(s[0].shape[0] * s[0].shape[1] for s in streams) * TOP_K
    n_tiles_max = (n_pairs + N_EXPERTS * (EXPERT_TILE - 1)) // EXPERT_TILE
    n_rows_max = n_tiles_max * EXPERT_TILE
    tile_e, n_valid, pad_start, pad_end, padded = _route(counts, n_rows_max)
    c = _row_chunks(streams[0][0].shape[2])
    xs = None
    dests = []
    for x_new, rows, top_e, rank, _, _ in streams:
        n = x_new.shape[0] * x_new.shape[1]
        dests.append(_pair_dest(top_e, rank, pad_start, c))
        xs = _dispatch(dests[-1], rows.reshape(n * c, LANES), xs, pad_end, padded, n_rows_max, c)
    ys = _experts(xs, tile_e + layer * N_EXPERTS, n_valid, *expert_w)
    return [_combine(dests[i], ys, s[4], s[0], s[5]) for i, s in enumerate(streams)]


def _hyena_in_kernel(xm_ref, xp_ref, xn_ref, sh_ref, sc_ref, g_ref, w_ref, b_ref, cw_ref, cb_ref,
                     x0_ref, z_ref, *, tm, width, chunk):
    i = pl.program_id(1)
    last = pl.num_programs(1) - 1
    xe = jnp.concatenate([xp_ref[0], xm_ref[0], xn_ref[0]], axis=0)
    h = _norm_mod(xe, g_ref[...], sh_ref[0], sc_ref[0]).astype(BF16)
    rows = lax.broadcasted_iota(I32, (tm + 2 * SUBLANES, 1), 0)
    inside = jnp.logical_and(jnp.logical_or(rows >= SUBLANES, i > 0),
                             jnp.logical_or(rows < tm + SUBLANES, i < last))
    n_ext = tm + 2 * SUBLANES

    def conv(col):
        u = _dot(h, w_ref[:, col:col + chunk]) + b_ref[:, col:col + chunk]
        u = jnp.where(inside, u, 0.0)
        prev = pltpu.roll(u, 1, 0)[SUBLANES:SUBLANES + tm]
        nxt = pltpu.roll(u, n_ext - 1, 0)[SUBLANES:SUBLANES + tm]
        cw = cw_ref[:, col:col + chunk]
        return (prev * cw[0:1] + u[SUBLANES:SUBLANES + tm] * cw[1:2] + nxt * cw[2:3]
                + cb_ref[:, col:col + chunk])

    for c in range(width // chunk):
        x0 = conv(c * chunk)
        x1 = conv(width + c * chunk)
        v = conv(2 * width + c * chunk)
        x0_ref[0, :, c * chunk:(c + 1) * chunk] = x0.astype(BF16)
        z_ref[0, :, c * chunk:(c + 1) * chunk] = (v * x1).astype(BF16)


def _hyena_in(x, shift, scale, g, w_in, b_in, conv_w, conv_b):
    b, t, d = x.shape
    n3 = w_in.shape[1]
    width = n3 // 3
    tm = min(ROW_TILE, t)
    blocks8 = tm // SUBLANES
    last8 = t // SUBLANES - 1
    bm = shift.shape[0]
    mod_map = (lambda i, j: (i, 0, 0)) if bm > 1 else (lambda i, j: (0, 0, 0))
    const = lambda shape: pl.BlockSpec(shape, lambda i, j: (0,) * len(shape))
    chunk = 512
    est = 2 * (tm + 16) * d * 4 + 2 * d * n3 * 2 + 4 * tm * width * 2 + 10 * (tm + 16) * chunk * 4 + 4 * tm * d * 4
    return pl.pallas_call(
        functools.partial(_hyena_in_kernel, tm=tm, width=width, chunk=chunk),
        grid=(b, t // tm),
        in_specs=[pl.BlockSpec((1, tm, d), lambda i, j: (i, j, 0)),
                  pl.BlockSpec((1, SUBLANES, d), lambda i, j: (i, jnp.maximum(j * blocks8 - 1, 0), 0)),
                  pl.BlockSpec((1, SUBLANES, d), lambda i, j: (i, jnp.minimum((j + 1) * blocks8, last8), 0)),
                  pl.BlockSpec((1, 1, d), mod_map), pl.BlockSpec((1, 1, d), mod_map),
                  const((1, d)), const((d, n3)), const((1, n3)), const((HY_SHORT, n3)), const((1, n3))],
        out_specs=[pl.BlockSpec((1, tm, width), lambda i, j: (i, j, 0))] * 2,
        out_shape=[jax.ShapeDtypeStruct((b, t, width), BF16)] * 2,
        compiler_params=_cparams(2, est),
        name="hyena_in",
    )(x, x, x, shift, scale, g.reshape(1, d), w_in, b_in.reshape(1, n3), conv_w, conv_b.reshape(1, n3))


def _dft_matrix(length):
    n = 2 * length
    f = jnp.arange(length, dtype=I32)
    ang = ((f[:, None] * f[None, :]) % n).astype(F32) * (2.0 * math.pi / n)
    nyq = jnp.where(f % 2 == 0, 1.0, -1.0).astype(F32)
    msin = (-jnp.sin(ang)).at[0].set(nyq)
    return jnp.concatenate([jnp.cos(ang), msin], axis=0).astype(BF16)


def _filter_kernel(z_ref, w1_ref, b1_ref, fr_ref, w2_ref, b2_ref, w3f_ref, w3b_ref, b3f_ref, b3b_ref,
                   t_ref, dl_ref, wf_ref, a_ref, b_ref, c_ref, d_ref, *, length):
    freq = fr_ref[...]
    a1 = jnp.sin(freq * (_dot3(z_ref[...], w1_ref[...]) + b1_ref[...]))
    a2 = jnp.sin(freq * (_dot3(a1, w2_ref[...]) + b2_ref[...]))
    decay = jnp.exp(-t_ref[...] * dl_ref[...])
    rows = lax.broadcasted_iota(I32, (length, 1), 0)
    first = rows == 0
    hf = (_dot3(a2, w3f_ref[...]) + b3f_ref[...]) * decay
    hb = jnp.where(first, 0.0, (_dot3(a2, w3b_ref[...]) + b3b_ref[...]) * decay)
    inv = 1.0 / (jnp.sum(jnp.abs(hf), axis=0, keepdims=True) + jnp.sum(jnp.abs(hb), axis=0, keepdims=True) + EPS)
    hf = hf * inv
    hb = hb * inv
    p_hi, p_lo = _split(hf + hb)
    q_hi, q_lo = _split(hf - hb)
    wf = wf_ref[...]
    sp = _dot(wf, p_hi) + _dot(wf, p_lo)
    ki = _dot(wf[length:], q_hi) + _dot(wf[length:], q_lo)
    kr = sp[:length]
    knyq = sp[length:length + 1]
    n = 2.0 * length
    sc = jnp.where(first, 1.0 / n, 2.0 / n)
    a_ref[...] = kr * sc
    b_ref[...] = jnp.where(first, 0.0, -ki * sc)
    c_ref[...] = jnp.where(first, 0.0, ki * sc)
    d_ref[...] = jnp.where(first, knyq * (1.0 / n), kr * sc)


def _hyena_filter_spectrum(length, w1, b1, w2, b2, w3, b3, sin_freq, wf):
    width = w3.shape[1] // 2
    ffn = w1.shape[1]
    t = jnp.linspace(0.0, 1.0, length, dtype=F32)[:, None]
    w = 2.0 * math.pi * jnp.arange(length, dtype=F32)[:, None] / length
    f = jnp.linspace(1e-4, HY_BANDS - 1, HY_BANDS, dtype=F32)
    z = jnp.concatenate([t, jnp.cos(f * w), -jnp.sin(f * w)], axis=-1)
    zp = jnp.zeros((length, LANES), F32).at[:, :HY_EMB].set(z)
    pad2 = lambda m: jnp.zeros((LANES, m.shape[1] if m.shape[1] > LANES else LANES), F32).at[:m.shape[0], :m.shape[1]].set(m)
    padv = lambda v: jnp.zeros((1, LANES), F32).at[0, :v.shape[0]].set(v)
    max_decay = math.log(HY_TARGET) / HY_FAST_DECAY
    min_decay = math.log(HY_TARGET) / HY_SLOW_DECAY
    absdelta = jnp.abs(jnp.linspace(min_decay, max_decay, width, dtype=F32)).reshape(1, width)
    tc = LANES
    nct = width // tc
    const = lambda shape: pl.BlockSpec(shape, lambda j: (0,) * len(shape))
    w3p = pad2(w3)
    b3r = b3.reshape(1, 2 * width)
    plane = pl.BlockSpec((length, tc), lambda j: (0, j))
    est = 2 * length * length * 2 + 2 * 4 * length * tc * 4 + 24 * length * tc * 4
    return pl.pallas_call(
        functools.partial(_filter_kernel, length=length),
        grid=(nct,),
        in_specs=[const((length, LANES)), const((LANES, LANES)), const((1, LANES)), const((1, LANES)),
                  const((LANES, LANES)), const((1, LANES)),
                  pl.BlockSpec((LANES, tc), lambda j: (0, j)), pl.BlockSpec((LANES, tc), lambda j: (0, nct + j)),
                  pl.BlockSpec((1, tc), lambda j: (0, j)), pl.BlockSpec((1, tc), lambda j: (0, nct + j)),
                  const((length, 1)), pl.BlockSpec((1, tc), lambda j: (0, j)),
                  pl.BlockSpec((2 * length, length), lambda j: (0, 0), pipeline_mode=pl.Buffered(1))],
        out_specs=[plane] * 4,
        out_shape=[jax.ShapeDtypeStruct((length, width), F32)] * 4,
        compiler_params=_cparams(1, est),
        name="hyena_filter",
    )(zp, pad2(w1), padv(b1), padv(sin_freq), pad2(w2), padv(b2), w3p, w3p, b3r, b3r, t, absdelta, wf)


def _conv_fwd_kernel(z_ref, a_ref, b_ref, c_ref, d_ref, wf_ref, y_ref, *, length, chunk):
    z = z_ref[0]

    def body(s, carry):
        f0 = pl.multiple_of(s * chunk, chunk)
        vr = _dot(wf_ref[pl.ds(f0, chunk), :], z)
        vi = _dot(wf_ref[pl.ds(length + f0, chunk), :], z)
        sl = pl.ds(f0, chunk)
        y_ref[0, sl, :] = (vr * a_ref[sl, :] + vi * b_ref[sl, :]).astype(BF16)
        y_ref[0, pl.ds(length + f0, chunk), :] = (vr * c_ref[sl, :] + vi * d_ref[sl, :]).astype(BF16)
        return carry

    lax.fori_loop(0, length // chunk, body, 0, unroll=True)


def _conv_inv_kernel(y_ref, wi_ref, z_ref, x0_ref, db_ref, o_ref, *, length, chunk):
    yf = y_ref[0]

    def body(s, carry):
        sl = pl.ds(pl.multiple_of(s * chunk, chunk), chunk)
        y = _dot(wi_ref[sl, :], yf)
        o_ref[0, sl, :] = (x0_ref[0, sl, :].astype(F32)
                           * (y + z_ref[0, sl, :].astype(F32) * db_ref[...])).astype(BF16)
        return carry

    lax.fori_loop(0, length // chunk, body, 0, unroll=True)


def _long_conv(z, x0, planes, d_bias, wf, wi):
    b, length, width = z.shape
    tc = CONV_COLS
    chunk = min(CONV_CHUNK, length)
    single = pl.Buffered(1)
    plane = pl.BlockSpec((length, tc), lambda j, i: (0, j), pipeline_mode=single)
    est_f = 2 * length * length * 2 + 4 * length * tc * 4 + 2 * length * tc * 2 + 4 * length * tc * 2 + 8 * chunk * tc * 4
    spec = pl.pallas_call(
        functools.partial(_conv_fwd_kernel, length=length, chunk=chunk),
        grid=(width // tc, b),
        in_specs=[pl.BlockSpec((1, length, tc), lambda j, i: (i, 0, j))] + [plane] * 4
        + [pl.BlockSpec((2 * length, length), lambda j, i: (0, 0), pipeline_mode=single)],
        out_specs=pl.BlockSpec((1, 2 * length, tc), lambda j, i: (i, 0, j)),
        out_shape=jax.ShapeDtypeStruct((b, 2 * length, width), BF16),
        compiler_params=_cparams(2, est_f),
        name="hyena_conv_fwd",
    )(z, *planes, wf)
    col = lambda rows: pl.BlockSpec((1, rows, tc), lambda i, j: (i, 0, j))
    est_i = 2 * length * length * 2 + 4 * length * tc * 2 + 6 * length * tc * 2 + 8 * chunk * tc * 4
    return pl.pallas_call(
        functools.partial(_conv_inv_kernel, length=length, chunk=chunk),
        grid=(b, width // tc),
        in_specs=[col(2 * length),
                  pl.BlockSpec((length, 2 * length), lambda i, j: (0, 0), pipeline_mode=single),
                  col(length), col(length), pl.BlockSpec((1, tc), lambda i, j: (0, j))],
        out_specs=col(length),
        out_shape=jax.ShapeDtypeStruct((b, length, width), BF16),
        compiler_params=_cparams(2, est_i),
        name="hyena_conv_inv",
    )(spec, wi, z, x0, d_bias.reshape(1, width))


def _hyena_mixer(x, shift, scale, g, hy, wf, wi):
    w_in, b_in, conv_w, conv_b, w1, b1, w2, b2, w3, b3, sin_freq, d_bias = hy
    x0, z = _hyena_in(x, shift, scale, g, w_in, b_in, conv_w, conv_b)
    planes = _hyena_filter_spectrum(x.shape[1], w1, b1, w2, b2, w3, b3, sin_freq, wf)
    return _long_conv(z, x0, planes, d_bias, wf, wi)


def kernel(x, c, ctx, c_ctx, w_mod, b_mod, norm1_g, norm2_g, attn_w_qkv, attn_q_gain, attn_k_gain, attn_w_o, hy_w_in, hy_b_in, hy_conv_w, hy_conv_b, hy_filt_w1, hy_filt_b1, hy_filt_w2, hy_filt_b2, hy_filt_w3, hy_filt_b3, hy_sin_freq, hy_d_bias, hy_w_out, hy_b_out, moe_router_w, moe_router_b, moe_w_gu, moe_b_gu, moe_w_down, moe_b_down):
    b, length, d = x.shape
    n_ctx = ctx.shape[1]
    depth = w_mod.shape[0]
    n_q_heads = d // HEAD_DIM

    r_pad = -(-(b + 1) // SUBLANES) * SUBLANES
    cc = jnp.zeros((r_pad, d), F32).at[:b].set(c).at[b].set(c_ctx)
    mod_all = _modulation(cc, w_mod, b_mod).reshape(depth, r_pad, 6, d)

    rope_tabs = _rope_tables(length, n_q_heads + N_KV_HEADS)
    n_exp, _, f2 = moe_w_gu.shape[1:]
    b_gu = moe_b_gu.reshape(depth * n_exp, 1, f2)
    expert_w = (moe_w_gu.reshape(depth * n_exp, d, f2), moe_w_down.reshape(depth * n_exp, f2 // 2, d),
                b_gu[..., 0::2], b_gu[..., 1::2], moe_b_down.reshape(depth * n_exp, 1, d))
    dft = {}

    def dft_pair(n):
        if n not in dft:
            wf = _dft_matrix(n)
            dft[n] = (wf, wf.T)
        return dft[n]

    for i in range(depth):
        j = i // N_MIXERS
        is_attn = i % N_MIXERS == 0
        ctx_live = any(m % N_MIXERS == 0 for m in range(i + 1, depth))
        mod = [mod_all[i, :b, s].reshape(b, 1, d) for s in range(6)]
        mod_c = [mod_all[i, b, s].reshape(1, 1, d) for s in range(6)]
        if is_attn:
            w_qkv = attn_w_qkv[j].astype(BF16)
            n_q = n_q_heads * HEAD_DIM
            q, k, v = _qkv_project(x, mod[0], mod[1], norm1_g[i], w_qkv, attn_q_gain[j], attn_k_gain[j],
                                   rope_tabs, True)
            w_c = w_qkv if ctx_live else w_qkv[:, n_q:]
            ctx_out = _qkv_project(ctx, mod_c[0], mod_c[1], norm1_g[i], w_c, attn_q_gain[j], attn_k_gain[j],
                                   None, ctx_live)
            k_c, v_c = ctx_out[-2], ctx_out[-1]
            a = _attention(q, [(k, v), (k_c, v_c)])
            a_c = _attention(ctx_out[0], [(k_c, v_c)]) if ctx_live else None
            w_out, b_out = attn_w_o[j].astype(BF16), jnp.zeros((d,), F32)
        else:
            hy = (hy_w_in[j].astype(BF16), hy_b_in[j], hy_conv_w[j], hy_conv_b[j], hy_filt_w1[j], hy_filt_b1[j],
                  hy_filt_w2[j], hy_filt_b2[j], hy_filt_w3[j], hy_filt_b3[j], hy_sin_freq[j], hy_d_bias[j])
            a = _hyena_mixer(x, mod[0], mod[1], norm1_g[i], hy, *dft_pair(length))
            a_c = _hyena_mixer(ctx, mod_c[0], mod_c[1], norm1_g[i], hy, *dft_pair(n_ctx)) if ctx_live else None
            w_out, b_out = hy_w_out[j].astype(BF16), hy_b_out[j]

        post = _post_mixer(a, w_out, b_out, x, mod[2], mod[3], mod[4], norm2_g[i],
                           moe_router_w[i], moe_router_b[i], jnp.zeros((1, LANES), F32))
        streams = [tuple(post[:5]) + (mod[5],)]
        counts = post[5]
        if ctx_live:
            post = _post_mixer(a_c, w_out, b_out, ctx, mod_c[2], mod_c[3], mod_c[4], norm2_g[i],
                               moe_router_w[i], moe_router_b[i], counts)
            streams.append(tuple(post[:5]) + (mod_c[5],))
            counts = post[5]
        outs = _moe(streams, counts, i, expert_w)
        x = outs[0]
        if ctx_live:
            ctx = outs[1]
    return x
```

```python
import functools
import math

import jax
import jax.numpy as jnp
from jax import lax
from jax.experimental import pallas as pl
from jax.experimental.pallas import tpu as pltpu

F32 = jnp.float32
BF16 = jnp.bfloat16
I32 = jnp.int32

LANES = 128
SUBLANES = 8
VMEM_BYTES = 64 * 2**20
VMEM_CAP = VMEM_BYTES - 8 * 2**20

GRID_W = 64
EPS = 1e-6
HEAD_DIM = 64
N_KV_HEADS = 4
ROPE_THETA = 10000.0
ROPE_PAIRS = HEAD_DIM // 4
HY_SHORT = 3
HY_EMB = 33
HY_BANDS = (HY_EMB - 1) // 2
HY_FAST_DECAY = 0.3
HY_SLOW_DECAY = 1.5
HY_TARGET = 1e-2
N_EXPERTS = 32
TOP_K = 4
SWIGLU_LIMIT = 7.0
SWIGLU_ALPHA = 1.702
N_MIXERS = 2
NEG_BIG = -1e30

ROW_TILE = 512
ATT_Q_TILE = 256
EXPERT_TILE = 512
TOKEN_TILE = 512
CONV_COLS = 256
CONV_CHUNK = 512


def _cparams(n_axes, vmem_estimate):
    limit = int(min(max(vmem_estimate * 5 // 4, 32 * 2**20), VMEM_CAP))
    return pltpu.CompilerParams(dimension_semantics=("arbitrary",) * n_axes, vmem_limit_bytes=limit)


def _split(a):
    hi = a.astype(BF16)
    lo = (a - hi.astype(F32)).astype(BF16)
    return hi, lo


def _dot(a, b):
    return jnp.dot(a, b, preferred_element_type=F32)


def _dot3(a, b):
    a_hi, a_lo = _split(a)
    b_hi, b_lo = _split(b)
    return _dot(a_hi, b_hi) + _dot(a_lo, b_hi) + _dot(a_hi, b_lo)


def _norm_mod(x, g, shift, scale):
    y = x * lax.rsqrt(jnp.mean(x * x, axis=-1, keepdims=True) + EPS)
    return (y * g) * (1.0 + scale) + shift


def _pack(a, b):
    ua = lax.bitcast_convert_type(a.astype(BF16).astype(F32), I32)
    ub = lax.bitcast_convert_type(b.astype(BF16).astype(F32), I32)
    return (ua & jnp.int32(-65536)) | lax.shift_right_logical(ub, jnp.int32(16))


def _unpack(p):
    a = lax.bitcast_convert_type(p & jnp.int32(-65536), F32)
    b = lax.bitcast_convert_type(lax.shift_left(p, jnp.int32(16)), F32)
    return a, b


def _row_chunks(d):
    return d // (2 * LANES)


def _store_token_rows(ref, lead, y):
    m, d = y.shape
    c = _row_chunks(d)
    for j in range(c):
        lo = y[:, LANES * j:LANES * (j + 1)]
        hi = y[:, d // 2 + LANES * j:d // 2 + LANES * (j + 1)]
        ref[lead + (pl.ds(j, m, stride=c), slice(None))] = _pack(lo, hi)


def _load_token_rows(ref, lead, m, c):
    halves = [_unpack(ref[lead + (pl.ds(j, m, stride=c), slice(None))]) for j in range(c)]
    return jnp.concatenate([h[0] for h in halves] + [h[1] for h in halves], axis=1)


def _mod_kernel(c_ref, w_ref, b_ref, o_ref):
    c = c_ref[...]
    a = c * jax.nn.sigmoid(c)
    o_ref[0] = _dot3(a, w_ref[0]) + b_ref[0]


def _modulation(cc, w_mod, b_mod):
    depth, d, n = w_mod.shape
    r = cc.shape[0]
    tn = 1536
    return pl.pallas_call(
        _mod_kernel,
        grid=(depth, n // tn),
        in_specs=[
            pl.BlockSpec((r, d), lambda i, j: (0, 0)),
            pl.BlockSpec((1, d, tn), lambda i, j: (i, 0, j)),
            pl.BlockSpec((1, 1, tn), lambda i, j: (i, 0, j)),
        ],
        out_specs=pl.BlockSpec((1, r, tn), lambda i, j: (i, 0, j)),
        out_shape=jax.ShapeDtypeStruct((depth, r, n), F32),
        compiler_params=_cparams(2, 2 * d * tn * 4 + 3 * d * tn * 2),
        name="modulation",
    )(cc, w_mod, b_mod.reshape(depth, 1, n))


def _qkv_kernel(*refs, n_qk, rope, with_q):
    x_ref, sh_ref, sc_ref, g_ref, w_ref, gain_ref, e_ref, et_ref = refs[:8]
    pos = 8
    if rope:
        cos_ref, sa_ref, sb_ref = refs[pos:pos + 3]
        pos += 3
    outs = refs[pos:]
    n_q = n_qk - N_KV_HEADS * HEAD_DIM
    tm = x_ref.shape[1]
    n_sub = 2 if tm % (2 * SUBLANES) == 0 else 1
    for s in range(n_sub):
        rows = slice(s * (tm // n_sub), (s + 1) * (tm // n_sub))
        h = _norm_mod(x_ref[0, rows, :], g_ref[...], sh_ref[0], sc_ref[0]).astype(BF16)
        acc = _dot(h, w_ref[...])
        qk = acc[:, :n_qk]
        ssq = _dot((qk * qk).astype(BF16), e_ref[...])
        r_hi, r_lo = _split(lax.rsqrt(ssq * (1.0 / HEAD_DIM) + EPS))
        y = qk * (_dot(r_hi, et_ref[...]) + _dot(r_lo, et_ref[...])) * gain_ref[...]
        if rope:
            y = (y * cos_ref[rows, :] + pltpu.roll(y, n_qk - ROPE_PAIRS, 1) * sa_ref[rows, :]
                 + pltpu.roll(y, ROPE_PAIRS, 1) * sb_ref[rows, :])
        kv_outs = outs
        if with_q:
            outs[0][0, rows, :] = (y[:, :n_q] * (HEAD_DIM ** -0.5)).astype(BF16)
            kv_outs = outs[1:]
        kv_outs[0][0, rows, :] = y[:, n_q:].astype(BF16)
        kv_outs[1][0, rows, :] = acc[:, n_qk:].astype(BF16)


def _qkv_project(x, shift, scale, g, w, q_gain, k_gain, rope_tabs, with_q):
    b, t, d = x.shape
    kv = N_KV_HEADS * HEAD_DIM
    nc = w.shape[1]
    n_qk = nc - kv
    n_heads = n_qk // HEAD_DIM
    tm = min(ROW_TILE, t)
    gains = ([q_gain] * (n_heads - N_KV_HEADS)) + [k_gain] * N_KV_HEADS
    gain = jnp.concatenate(gains).reshape(1, n_qk).astype(F32)
    head_of_col = jnp.arange(n_qk) // HEAD_DIM
    e = (head_of_col[:, None] == jnp.arange(LANES)[None, :]).astype(BF16)
    et = e.T
    bm = shift.shape[0]
    mod_map = (lambda i, j: (j, 0, 0)) if bm > 1 else (lambda i, j: (0, 0, 0))
    in_specs = [
        pl.BlockSpec((1, tm, d), lambda i, j: (j, i, 0)),
        pl.BlockSpec((1, 1, d), mod_map),
        pl.BlockSpec((1, 1, d), mod_map),
        pl.BlockSpec((1, d), lambda i, j: (0, 0)),
        pl.BlockSpec((d, nc), lambda i, j: (0, 0)),
        pl.BlockSpec((1, n_qk), lambda i, j: (0, 0)),
        pl.BlockSpec((n_qk, LANES), lambda i, j: (0, 0)),
        pl.BlockSpec((LANES, n_qk), lambda i, j: (0, 0)),
    ]
    args = [x, shift, scale, g.reshape(1, d), w, gain, e, et]
    rope = rope_tabs is not None
    if rope:
        in_specs += [pl.BlockSpec((tm, n_qk), lambda i, j: (i, 0))] * 3
        args += list(rope_tabs)
    out_shape, out_specs = [], []
    widths = ([n_qk - kv] if with_q else []) + [kv, kv]
    for wd in widths:
        out_shape.append(jax.ShapeDtypeStruct((b, t, wd), BF16))
        out_specs.append(pl.BlockSpec((1, tm, wd), lambda i, j: (j, i, 0)))
    est = 2 * tm * d * 4 + 2 * d * nc * 2 + 6 * tm * n_qk * 4 + 8 * tm * nc * 4
    return pl.pallas_call(
        functools.partial(_qkv_kernel, n_qk=n_qk, rope=rope, with_q=with_q),
        grid=(t // tm, b),
        in_specs=in_specs,
        out_specs=out_specs,
        out_shape=out_shape,
        compiler_params=_cparams(2, est),
        name="qkv_project",
    )(*args)


def _rope_tables(n_tokens, n_heads):
    n_rows = n_tokens // GRID_W
    row = jnp.repeat(jnp.arange(n_rows, dtype=F32), GRID_W)
    col = jnp.tile(jnp.arange(GRID_W, dtype=F32), n_rows)
    inv_freq = ROPE_THETA ** (-jnp.arange(ROPE_PAIRS, dtype=F32) / ROPE_PAIRS)
    ar, ac = row[:, None] * inv_freq, col[:, None] * inv_freq
    zero = jnp.zeros_like(ar)
    cos = jnp.concatenate([jnp.cos(ar), jnp.cos(ar), jnp.cos(ac), jnp.cos(ac)], axis=1)
    sin_a = jnp.concatenate([-jnp.sin(ar), zero, -jnp.sin(ac), zero], axis=1)
    sin_b = jnp.concatenate([zero, jnp.sin(ar), zero, jnp.sin(ac)], axis=1)
    return tuple(jnp.tile(tab, (1, n_heads)) for tab in (cos, sin_a, sin_b))


def _attn_kernel(*refs, n_seg, tq):
    q_ref = refs[0]
    kv_refs = refs[1:1 + 2 * n_seg]
    o_ref = refs[1 + 2 * n_seg]
    q_per_kv = q_ref.shape[2] // (N_KV_HEADS * HEAD_DIM)
    n_halves = 2 if tq % (2 * SUBLANES) == 0 else 1
    rows_u = tq // n_halves
    units = [(g, half) for g in range(N_KV_HEADS) for half in range(n_halves)]

    def scores_of(unit):
        g, half = unit
        r0 = half * rows_u
        qs = jnp.concatenate([q_ref[0, r0:r0 + rows_u, (g * q_per_kv + j) * HEAD_DIM:(g * q_per_kv + j + 1) * HEAD_DIM]
                              for j in range(q_per_kv)], axis=0)
        ksl = slice(g * HEAD_DIM, (g + 1) * HEAD_DIM)
        return [lax.dot_general(qs, kv_refs[2 * s][0, :, ksl], (((1,), (1,)), ((), ())),
                                preferred_element_type=F32) for s in range(n_seg)]

    def finish(unit, scores):
        g, half = unit
        r0 = half * rows_u
        ksl = slice(g * HEAD_DIM, (g + 1) * HEAD_DIM)
        m = functools.reduce(jnp.maximum, [jnp.max(sc, axis=-1, keepdims=True) for sc in scores])
        probs = [jnp.exp(sc - m) for sc in scores]
        denom = functools.reduce(jnp.add, [jnp.sum(p, axis=-1, keepdims=True) for p in probs])
        o = functools.reduce(jnp.add, [_dot(probs[s].astype(BF16), kv_refs[2 * s + 1][0, :, ksl])
                                       for s in range(n_seg)])
        o = o / denom
        for j in range(q_per_kv):
            h = g * q_per_kv + j
            o_ref[0, r0:r0 + rows_u, h * HEAD_DIM:(h + 1) * HEAD_DIM] = o[j * rows_u:(j + 1) * rows_u].astype(BF16)

    pending = None
    for unit in units:
        scores = scores_of(unit)
        if pending is not None:
            finish(*pending)
        pending = (unit, scores)
    finish(*pending)


def _attention(q, segments):
    b, t, dq = q.shape
    tq = min(ATT_Q_TILE, t)
    in_specs = [pl.BlockSpec((1, tq, dq), lambda i, j: (i, j, 0))]
    args = [q]
    t_keys = 0
    for k, v in segments:
        ts, kv = k.shape[1], k.shape[2]
        in_specs += [pl.BlockSpec((1, ts, kv), lambda i, j: (i, 0, 0))] * 2
        args += [k, v]
        t_keys += ts
    q_rows = tq * dq // (N_KV_HEADS * HEAD_DIM)
    est = 4 * tq * dq * 2 + 8 * t_keys * N_KV_HEADS * HEAD_DIM * 2 + 4 * q_rows * t_keys * 4
    return pl.pallas_call(
        functools.partial(_attn_kernel, n_seg=len(segments), tq=tq),
        grid=(b, t // tq),
        in_specs=in_specs,
        out_specs=pl.BlockSpec((1, tq, dq), lambda i, j: (i, j, 0)),
        out_shape=jax.ShapeDtypeStruct((b, t, dq), BF16),
        compiler_params=_cparams(2, est),
        name="attention",
    )(*args)


def _post_kernel(a_ref, w_ref, bias_ref, x_ref, gate_ref, sh_ref, sc_ref, g_ref, rw_ref, rb_ref, tri_ref, base_ref,
                 xo_ref, hp_ref, te_ref, tr_ref, tg_ref, cnt_ref, run_ref, *, n_sub):
    @pl.when(jnp.logical_and(pl.program_id(0) == 0, pl.program_id(1) == 0))
    def _():
        run_ref[...] = base_ref[...]

    tm = x_ref.shape[1]
    rs = tm // n_sub
    c = hp_ref.shape[1] // tm
    running = run_ref[...]
    for s in range(n_sub):
        rows = slice(s * rs, (s + 1) * rs)
        o = _dot(a_ref[0, rows, :], w_ref[...]) + bias_ref[...]
        x_new = x_ref[0, rows, :] + gate_ref[0] * o
        xo_ref[0, rows, :] = x_new
        h = _norm_mod(x_new, g_ref[...], sh_ref[0], sc_ref[0])
        d = h.shape[1]
        for j in range(c):
            hp_ref[0, pl.ds(s * rs * c + j, rs, stride=c), :] = _pack(
                h[:, LANES * j:LANES * (j + 1)], h[:, d // 2 + LANES * j:d // 2 + LANES * (j + 1)])
        h_hi, h_lo = _split(h)
        wide = _dot(h_hi, rw_ref[...])
        logits = (wide[:, :LANES] + wide[:, LANES:] + _dot(h_lo, rw_ref[:, :LANES])
                  + rb_ref[...])
        lane = lax.broadcasted_iota(I32, logits.shape, 1)
        e_out = jnp.zeros(logits.shape, I32)
        g_out = jnp.zeros(logits.shape, F32)
        picked = jnp.zeros(logits.shape, F32)
        top = None
        total = None
        exps = []
        for k in range(TOP_K):
            m = jnp.max(logits, axis=1, keepdims=True)
            idx = jnp.min(jnp.where(logits == m, lane, LANES), axis=1, keepdims=True)
            hit = lane == idx
            logits = jnp.where(hit, -jnp.inf, logits)
            e_out = jnp.where(lane == k, idx, e_out)
            picked = jnp.where(hit, 1.0, picked)
            top = m if top is None else top
            ex = jnp.exp(m - top)
            exps.append(ex)
            total = ex if total is None else total + ex
        before = running + _dot(tri_ref[...], picked.astype(BF16))
        for k in range(TOP_K):
            g_out = jnp.where(lane == k, exps[k] / total, g_out)
        running = running + jnp.sum(picked, axis=0, keepdims=True)
        te_ref[0, rows, :] = e_out
        tr_ref[0, rows, :] = jnp.where(picked > 0.0, before, 0.0).astype(I32)
        tg_ref[0, rows, :] = g_out
    run_ref[...] = running
    cnt_ref[...] = running


def _post_mixer(a, w, bias, x, gate, shift, scale, g, router_w, router_b, base):
    b, t, d = x.shape
    k_in = a.shape[2]
    c = _row_chunks(d)
    tm = min(ROW_TILE, t)
    n_sub = 1
    rs = tm // n_sub
    bm = gate.shape[0]
    mod_map = (lambda i, j: (i, 0, 0)) if bm > 1 else (lambda i, j: (0, 0, 0))
    rw = jnp.concatenate(_split(jnp.zeros((d, LANES), F32).at[:, :N_EXPERTS].set(router_w)), axis=1)
    rb = jnp.full((1, LANES), NEG_BIG, F32).at[0, :N_EXPERTS].set(router_b)
    tri = (jnp.arange(rs)[:, None] > jnp.arange(rs)[None, :]).astype(BF16)
    row_spec = lambda wd: pl.BlockSpec((1, tm, wd), lambda i, j: (i, j, 0))
    const = lambda shape: pl.BlockSpec(shape, lambda i, j: (0,) * len(shape))
    est = 2 * tm * (k_in * 2 + d * 4 * 2 + d * 2 + 3 * LANES * 4) + 2 * k_in * d * 2 + 8 * tm * d * 4
    return pl.pallas_call(
        functools.partial(_post_kernel, n_sub=n_sub),
        grid=(b, t // tm),
        in_specs=[row_spec(k_in), const((k_in, d)), const((1, d)), row_spec(d),
                  pl.BlockSpec((1, 1, d), mod_map), pl.BlockSpec((1, 1, d), mod_map),
                  pl.BlockSpec((1, 1, d), mod_map), const((1, d)), const((d, 2 * LANES)), const((1, LANES)),
                  const((rs, rs)), const((1, LANES))],
        out_specs=[row_spec(d), pl.BlockSpec((1, tm * c, LANES), lambda i, j: (i, j, 0)),
                   row_spec(LANES), row_spec(LANES), row_spec(LANES), const((1, LANES))],
        out_shape=[jax.ShapeDtypeStruct((b, t, d), F32), jax.ShapeDtypeStruct((b, t * c, LANES), I32),
                   jax.ShapeDtypeStruct((b, t, LANES), I32), jax.ShapeDtypeStruct((b, t, LANES), I32),
                   jax.ShapeDtypeStruct((b, t, LANES), F32), jax.ShapeDtypeStruct((1, LANES), F32)],
        scratch_shapes=[pltpu.VMEM((1, LANES), F32)],
        compiler_params=_cparams(2, est),
        name="post_mixer",
    )(a, w, bias.reshape(1, d), x, gate, shift, scale, g.reshape(1, d), rw, rb, tri, base)


def _route(counts, n_rows_max):
    counts = counts[0, :N_EXPERTS].astype(I32)
    padded = (counts + EXPERT_TILE - 1) // EXPERT_TILE * EXPERT_TILE
    pad_end = jnp.cumsum(padded)
    pad_start = pad_end - padded
    tile_start = jnp.arange(n_rows_max // EXPERT_TILE, dtype=I32) * EXPERT_TILE
    tile_e = jnp.minimum(jnp.sum((pad_end[None, :] <= tile_start[:, None]).astype(I32), axis=1), N_EXPERTS - 1)
    n_valid = (pad_end[-1] // EXPERT_TILE).astype(I32).reshape(1)
    return tile_e.astype(I32), n_valid, pad_start.astype(I32), pad_end.astype(I32), padded.astype(I32)


def _pair_dest(top_e, rank, pad_start, c):
    e = top_e[..., :TOP_K].reshape(-1, TOP_K)
    row = rank[..., :N_EXPERTS].reshape(-1, 1, N_EXPERTS) + pad_start
    onehot = e[..., None] == jnp.arange(N_EXPERTS, dtype=I32)
    return (jnp.sum(jnp.where(onehot, row, 0), axis=-1) * c).reshape(-1).astype(I32)


def _dispatch_kernel(pend_ref, padded_ref, dest_ref, h_ref, *rest, tt, c, first):
    if first:
        xs_hbm, zbuf, zsem, stage, row_sems = rest
    else:
        _, xs_hbm, stage, row_sems = rest
    i = pl.program_id(0)

    if first:
        @pl.when(i == 0)
        def _():
            zbuf[...] = jnp.zeros(zbuf.shape, zbuf.dtype)

            def fill(e, carry):
                @pl.when(padded_ref[e] > 0)
                def _():
                    start = pl.multiple_of((pend_ref[e] - EXPERT_TILE) * c, EXPERT_TILE * c)
                    cp = pltpu.make_async_copy(zbuf, xs_hbm.at[pl.ds(start, EXPERT_TILE * c)], zsem)
                    cp.start()
                    cp.wait()
                return carry

            lax.fori_loop(0, N_EXPERTS, fill, 0)

            def fill_tail(t, carry):
                cp = pltpu.make_async_copy(
                    zbuf, xs_hbm.at[pl.ds(pl.multiple_of(t * (EXPERT_TILE * c), EXPERT_TILE * c), EXPERT_TILE * c)],
                    zsem)
                cp.start()
                cp.wait()
                return carry

            lax.fori_loop(pend_ref[N_EXPERTS - 1] // EXPERT_TILE, xs_hbm.shape[0] // (EXPERT_TILE * c),
                          fill_tail, 0)

    slot = i % 2
    stage[slot] = h_ref[...]

    def start(r, carry):
        for k in range(TOP_K):
            dst = pl.multiple_of(dest_ref[0, 0, TOP_K * r + k], c)
            pltpu.make_async_copy(stage.at[slot, pl.ds(pl.multiple_of(r * c, c), c)], xs_hbm.at[pl.ds(dst, c)],
                                  row_sems.at[slot]).start(priority=k % 2)
        return carry

    lax.fori_loop(0, tt, start, 0, unroll=8)

    def wait_all(s):
        for _ in range(TOP_K):
            pltpu.make_async_copy(stage.at[s], stage.at[s], row_sems.at[s]).wait()

    @pl.when(i > 0)
    def _():
        wait_all(1 - slot)

    @pl.when(i == pl.num_programs(0) - 1)
    def _():
        wait_all(slot)


def _dispatch(dest, h_rows, xs, pad_end, padded, n_rows, c):
    n = h_rows.shape[0] // c
    tt = min(TOKEN_TILE, n)
    first = xs is None
    in_specs = [pl.BlockSpec((1, 1, tt * TOP_K), lambda i, pe, pd: (i, 0, 0), memory_space=pltpu.SMEM),
                pl.BlockSpec((tt * c, LANES), lambda i, pe, pd: (i, 0))]
    args = [pad_end, padded, dest.reshape(n // tt, 1, tt * TOP_K), h_rows]
    scratch = [pltpu.VMEM((2, tt * c, LANES), I32), pltpu.SemaphoreType.DMA((2,))]
    if first:
        scratch = [pltpu.VMEM((EXPERT_TILE * c, LANES), I32), pltpu.SemaphoreType.DMA] + scratch
    else:
        in_specs.append(pl.BlockSpec(memory_space=pl.ANY))
        args.append(xs)
    return pl.pallas_call(
        functools.partial(_dispatch_kernel, tt=tt, c=c, first=first),
        grid_spec=pltpu.PrefetchScalarGridSpec(
            num_scalar_prefetch=2, grid=(n // tt,), in_specs=in_specs,
            out_specs=pl.BlockSpec(memory_space=pl.ANY), scratch_shapes=scratch),
        out_shape=jax.ShapeDtypeStruct((n_rows * c, LANES), I32),
        input_output_aliases={} if first else {4: 0},
        compiler_params=pltpu.CompilerParams(dimension_semantics=("arbitrary",), has_side_effects=True),
        name="moe_dispatch",
    )(*args)


def _expert_kernel(te_ref, nv_ref, xs_ref, wgu_ref, wd_ref, bg_ref, bu_ref, bd_ref, sel_ref, ys_ref,
                   wg_s, wu_s, wd_s):
    i = pl.program_id(0)
    valid = i < nv_ref[0]
    new_expert = jnp.logical_or(i == 0, te_ref[i] != te_ref[jnp.maximum(i - 1, 0)])

    @pl.when(jnp.logical_and(valid, new_expert))
    def _():
        for blk in range(wgu_ref.shape[2] // (2 * LANES)):
            cols = wgu_ref[0, :, 2 * LANES * blk:2 * LANES * (blk + 1)].astype(BF16)
            r = _dot(cols, sel_ref[...])
            wg_s[:, LANES * blk:LANES * (blk + 1)] = r[:, :LANES].astype(BF16)
            wu_s[:, LANES * blk:LANES * (blk + 1)] = r[:, LANES:].astype(BF16)
        wd_s[...] = wd_ref[0].astype(BF16)

    @pl.when(valid)
    def _():
        c = _row_chunks(wg_s.shape[0])
        xb = _load_token_rows(xs_ref, (), xs_ref.shape[0] // c, c).astype(BF16)
        g = jnp.minimum(_dot(xb, wg_s[...]) + bg_ref[0], SWIGLU_LIMIT)
        u = jnp.clip(_dot(xb, wu_s[...]) + bu_ref[0], -SWIGLU_LIMIT, SWIGLU_LIMIT)
        act = (u + 1.0) * g * jax.nn.sigmoid(SWIGLU_ALPHA * g)
        y = _dot(act.astype(BF16), wd_s[...]) + bd_ref[0]
        _store_token_rows(ys_ref, (), y)

    @pl.when(jnp.logical_not(valid))
    def _():
        ys_ref[...] = jnp.zeros(ys_ref.shape, ys_ref.dtype)


def _gate_up_selector():
    col = jnp.arange(2 * LANES)
    src = jnp.where(col < LANES, 2 * col, 2 * (col - LANES) + 1)
    return (col[:, None] == src[None, :]).astype(BF16)


def _experts(xs, tile_e, n_valid, w_gu, w_down, bg, bu, bd):
    d, f2 = w_gu.shape[1:]
    f = f2 // 2
    c = _row_chunks(d)
    rows = xs.shape[0] // c
    tm = EXPERT_TILE
    tile_map = lambda i, te, nv: (jnp.minimum(i, nv[0] - 1), 0)
    w_map = lambda i, te, nv: (te[jnp.minimum(i, nv[0] - 1)], 0, 0)
    est = 4 * tm * d * 2 + 2 * 3 * d * f * 4 + 3 * d * f * 2 + 8 * tm * f * 4
    return pl.pallas_call(
        _expert_kernel,
        grid_spec=pltpu.PrefetchScalarGridSpec(
            num_scalar_prefetch=2,
            grid=(rows // tm,),
            in_specs=[pl.BlockSpec((tm * c, LANES), tile_map),
                      pl.BlockSpec((1, d, f2), w_map), pl.BlockSpec((1, f, d), w_map),
                      pl.BlockSpec((1, 1, f), w_map), pl.BlockSpec((1, 1, f), w_map),
                      pl.BlockSpec((1, 1, d), w_map),
                      pl.BlockSpec((2 * LANES, 2 * LANES), lambda i, te, nv: (0, 0))],
            out_specs=pl.BlockSpec((tm * c, LANES), lambda i, te, nv: (i, 0)),
            scratch_shapes=[pltpu.VMEM((d, f), BF16), pltpu.VMEM((d, f), BF16), pltpu.VMEM((f, d), BF16)],
        ),
        out_shape=jax.ShapeDtypeStruct((rows * c, LANES), I32),
        compiler_params=_cparams(1, est),
        name="moe_experts",
    )(tile_e, n_valid, xs, w_gu, w_down, bg, bu, bd, _gate_up_selector())


def _combine_kernel(dest_ref, dest_next_ref, ys_hbm, gates_ref, x_ref, gate_ref, o_ref, buf, sems, *, tt, c):
    i = pl.program_id(0)
    slot = i % 2

    def issue(idx_ref, s):
        def body(r, carry):
            for k in range(TOP_K):
                src = pl.multiple_of(idx_ref[0, 0, TOP_K * r + k], c)
                pltpu.make_async_copy(ys_hbm.at[pl.ds(src, c)],
                                      buf.at[s, k, pl.ds(pl.multiple_of(r * c, c), c)],
                                      sems.at[s]).start(priority=k % 2)
            return carry

        lax.fori_loop(0, tt, body, 0, unroll=8)

    @pl.when(i == 0)
    def _():
        issue(dest_ref, 0)

    @pl.when(i + 1 < pl.num_programs(0))
    def _():
        issue(dest_next_ref, 1 - slot)

    pltpu.make_async_copy(buf.at[slot], buf.at[slot], sems.at[slot]).wait()
    gates = gates_ref[...]
    y = None
    for k in range(TOP_K):
        yk = _load_token_rows(buf, (slot, k), tt, c) * gates[:, k:k + 1]
        y = yk if y is None else y + yk
    o_ref[...] = x_ref[...] + gate_ref[0] * y


def _combine(dest, ys, gates, x, gate):
    b, t, d = x.shape
    n = b * t
    tt = min(TOKEN_TILE, t)
    steps = n // tt
    steps_per_batch = t // tt
    bm = gate.shape[0]
    gate_map = (lambda i: (i // steps_per_batch, 0, 0)) if bm > 1 else (lambda i: (0, 0, 0))
    est = 2 * TOP_K * tt * d * 2 + 4 * tt * d * 4 + 2 * tt * LANES * 4 + 6 * tt * d * 4
    dest2 = dest.reshape(steps, 1, tt * TOP_K)
    c = _row_chunks(d)
    out = pl.pallas_call(
        functools.partial(_combine_kernel, tt=tt, c=c),
        grid=(steps,),
        in_specs=[pl.BlockSpec((1, 1, tt * TOP_K), lambda i: (i, 0, 0), memory_space=pltpu.SMEM),
                  pl.BlockSpec((1, 1, tt * TOP_K), lambda i: (jnp.minimum(i + 1, steps - 1), 0, 0),
                               memory_space=pltpu.SMEM),
                  pl.BlockSpec(memory_space=pl.ANY),
                  pl.BlockSpec((tt, LANES), lambda i: (i, 0)),
                  pl.BlockSpec((tt, d), lambda i: (i, 0)),
                  pl.BlockSpec((1, 1, d), gate_map)],
        out_specs=pl.BlockSpec((tt, d), lambda i: (i, 0)),
        out_shape=jax.ShapeDtypeStruct((n, d), F32),
        scratch_shapes=[pltpu.VMEM((2, TOP_K, tt * c, LANES), I32), pltpu.SemaphoreType.DMA((2,))],
        compiler_params=_cparams(1, est),
        name="moe_combine",
    )(dest2, dest2, ys, gates.reshape(n, LANES), x.reshape(n, d), gate)
    return out.reshape(b, t, d)


def _moe(streams, counts, layer, expert_w):
    n_pairs = sum(s[0].shape[0] * s[0].shape[1] for s in streams) * TOP_K
    n_tiles_max = (n_pairs + N_EXPERTS * (EXPERT_TILE - 1)) // EXPERT_TILE
    n_rows_max = n_tiles_max * EXPERT_TILE
    tile_e, n_valid, pad_start, pad_end, padded = _route(counts, n_rows_max)
    c = _row_chunks(streams[0][0].shape[2])
    xs = None
    dests = []
    for x_new, rows, top_e, rank, _, _ in streams:
        n = x_new.shape[0] * x_new.shape[1]
        dests.append(_pair_dest(top_e, rank, pad_start, c))
        xs = _dispatch(dests[-1], rows.reshape(n * c, LANES), xs, pad_end, padded, n_rows_max, c)
    ys = _experts(xs, tile_e + layer * N_EXPERTS, n_valid, *expert_w)
    return [_combine(dests[i], ys, s[4], s[0], s[5]) for i, s in enumerate(streams)]


def _hyena_in_kernel(xm_ref, xp_ref, xn_ref, sh_ref, sc_ref, g_ref, w_ref, b_ref, cw_ref, cb_ref,
                     x0_ref, z_ref, *, tm, width, chunk):
    i = pl.program_id(1)
    last = pl.num_programs(1) - 1
    xe = jnp.concatenate([xp_ref[0], xm_ref[0], xn_ref[0]], axis=0)
    h = _norm_mod(xe, g_ref[...], sh_ref[0], sc_ref[0]).astype(BF16)
    rows = lax.broadcasted_iota(I32, (tm + 2 * SUBLANES, 1), 0)
    inside = jnp.logical_and(jnp.logical_or(rows >= SUBLANES, i > 0),
                             jnp.logical_or(rows < tm + SUBLANES, i < last))
    n_ext = tm + 2 * SUBLANES

    def conv(col):
        u = _dot(h, w_ref[:, col:col + chunk]) + b_ref[:, col:col + chunk]
        u = jnp.where(inside, u, 0.0)
        prev = pltpu.roll(u, 1, 0)[SUBLANES:SUBLANES + tm]
        nxt = pltpu.roll(u, n_ext - 1, 0)[SUBLANES:SUBLANES + tm]
        cw = cw_ref[:, col:col + chunk]
        return (prev * cw[0:1] + u[SUBLANES:SUBLANES + tm] * cw[1:2] + nxt * cw[2:3]
                + cb_ref[:, col:col + chunk])

    for c in range(width // chunk):
        x0 = conv(c * chunk)
        x1 = conv(width + c * chunk)
        v = conv(2 * width + c * chunk)
        x0_ref[0, :, c * chunk:(c + 1) * chunk] = x0.astype(BF16)
        z_ref[0, :, c * chunk:(c + 1) * chunk] = (v * x1).astype(BF16)


def _hyena_in(x, shift, scale, g, w_in, b_in, conv_w, conv_b):
    b, t, d = x.shape
    n3 = w_in.shape[1]
    width = n3 // 3
    tm = min(ROW_TILE, t)
    blocks8 = tm // SUBLANES
    last8 = t // SUBLANES - 1
    bm = shift.shape[0]
    mod_map = (lambda i, j: (i, 0, 0)) if bm > 1 else (lambda i, j: (0, 0, 0))
    const = lambda shape: pl.BlockSpec(shape, lambda i, j: (0,) * len(shape))
    chunk = 512
    est = 2 * (tm + 16) * d * 4 + 2 * d * n3 * 2 + 4 * tm * width * 2 + 10 * (tm + 16) * chunk * 4 + 4 * tm * d * 4
    return pl.pallas_call(
        functools.partial(_hyena_in_kernel, tm=tm, width=width, chunk=chunk),
        grid=(b, t // tm),
        in_specs=[pl.BlockSpec((1, tm, d), lambda i, j: (i, j, 0)),
                  pl.BlockSpec((1, SUBLANES, d), lambda i, j: (i, jnp.maximum(j * blocks8 - 1, 0), 0)),
                  pl.BlockSpec((1, SUBLANES, d), lambda i, j: (i, jnp.minimum((j + 1) * blocks8, last8), 0)),
                  pl.BlockSpec((1, 1, d), mod_map), pl.BlockSpec((1, 1, d), mod_map),
                  const((1, d)), const((d, n3)), const((1, n3)), const((HY_SHORT, n3)), const((1, n3))],
        out_specs=[pl.BlockSpec((1, tm, width), lambda i, j: (i, j, 0))] * 2,
        out_shape=[jax.ShapeDtypeStruct((b, t, width), BF16)] * 2,
        compiler_params=_cparams(2, est),
        name="hyena_in",
    )(x, x, x, shift, scale, g.reshape(1, d), w_in, b_in.reshape(1, n3), conv_w, conv_b.reshape(1, n3))


def _dft_matrix(length):
    n = 2 * length
    f = jnp.arange(length, dtype=I32)
    ang = ((f[:, None] * f[None, :]) % n).astype(F32) * (2.0 * math.pi / n)
    nyq = jnp.where(f % 2 == 0, 1.0, -1.0).astype(F32)
    msin = (-jnp.sin(ang)).at[0].set(nyq)
    return jnp.concatenate([jnp.cos(ang), msin], axis=0).astype(BF16)


def _filter_kernel(z_ref, w1_ref, b1_ref, fr_ref, w2_ref, b2_ref, w3f_ref, w3b_ref, b3f_ref, b3b_ref,
                   t_ref, dl_ref, wf_ref, a_ref, b_ref, c_ref, d_ref, *, length):
    freq = fr_ref[...]
    a1 = jnp.sin(freq * (_dot3(z_ref[...], w1_ref[...]) + b1_ref[...]))
    a2 = jnp.sin(freq * (_dot3(a1, w2_ref[...]) + b2_ref[...]))
    decay = jnp.exp(-t_ref[...] * dl_ref[...])
    rows = lax.broadcasted_iota(I32, (length, 1), 0)
    first = rows == 0
    hf = (_dot3(a2, w3f_ref[...]) + b3f_ref[...]) * decay
    hb = jnp.where(first, 0.0, (_dot3(a2, w3b_ref[...]) + b3b_ref[...]) * decay)
    inv = 1.0 / (jnp.sum(jnp.abs(hf), axis=0, keepdims=True) + jnp.sum(jnp.abs(hb), axis=0, keepdims=True) + EPS)
    hf = hf * inv
    hb = hb * inv
    p_hi, p_lo = _split(hf + hb)
    q_hi, q_lo = _split(hf - hb)
    wf = wf_ref[...]
    sp = _dot(wf, p_hi) + _dot(wf, p_lo)
    ki = _dot(wf[length:], q_hi) + _dot(wf[length:], q_lo)
    kr = sp[:length]
    knyq = sp[length:length + 1]
    n = 2.0 * length
    sc = jnp.where(first, 1.0 / n, 2.0 / n)
    a_ref[...] = kr * sc
    b_ref[...] = jnp.where(first, 0.0, -ki * sc)
    c_ref[...] = jnp.where(first, 0.0, ki * sc)
    d_ref[...] = jnp.where(first, knyq * (1.0 / n), kr * sc)


def _hyena_filter_spectrum(length, w1, b1, w2, b2, w3, b3, sin_freq, wf):
    width = w3.shape[1] // 2
    ffn = w1.shape[1]
    t = jnp.linspace(0.0, 1.0, length, dtype=F32)[:, None]
    w = 2.0 * math.pi * jnp.arange(length, dtype=F32)[:, None] / length
    f = jnp.linspace(1e-4, HY_BANDS - 1, HY_BANDS, dtype=F32)
    z = jnp.concatenate([t, jnp.cos(f * w), -jnp.sin(f * w)], axis=-1)
    zp = jnp.zeros((length, LANES), F32).at[:, :HY_EMB].set(z)
    pad2 = lambda m: jnp.zeros((LANES, m.shape[1] if m.shape[1] > LANES else LANES), F32).at[:m.shape[0], :m.shape[1]].set(m)
    padv = lambda v: jnp.zeros((1, LANES), F32).at[0, :v.shape[0]].set(v)
    max_decay = math.log(HY_TARGET) / HY_FAST_DECAY
    min_decay = math.log(HY_TARGET) / HY_SLOW_DECAY
    absdelta = jnp.abs(jnp.linspace(min_decay, max_decay, width, dtype=F32)).reshape(1, width)
    tc = LANES
    nct = width // tc
    const = lambda shape: pl.BlockSpec(shape, lambda j: (0,) * len(shape))
    w3p = pad2(w3)
    b3r = b3.reshape(1, 2 * width)
    plane = pl.BlockSpec((length, tc), lambda j: (0, j))
    est = 2 * length * length * 2 + 2 * 4 * length * tc * 4 + 24 * length * tc * 4
    return pl.pallas_call(
        functools.partial(_filter_kernel, length=length),
        grid=(nct,),
        in_specs=[const((length, LANES)), const((LANES, LANES)), const((1, LANES)), const((1, LANES)),
                  const((LANES, LANES)), const((1, LANES)),
                  pl.BlockSpec((LANES, tc), lambda j: (0, j)), pl.BlockSpec((LANES, tc), lambda j: (0, nct + j)),
                  pl.BlockSpec((1, tc), lambda j: (0, j)), pl.BlockSpec((1, tc), lambda j: (0, nct + j)),
                  const((length, 1)), pl.BlockSpec((1, tc), lambda j: (0, j)),
                  pl.BlockSpec((2 * length, length), lambda j: (0, 0), pipeline_mode=pl.Buffered(1))],
        out_specs=[plane] * 4,
        out_shape=[jax.ShapeDtypeStruct((length, width), F32)] * 4,
        compiler_params=_cparams(1, est),
        name="hyena_filter",
    )(zp, pad2(w1), padv(b1), padv(sin_freq), pad2(w2), padv(b2), w3p, w3p, b3r, b3r, t, absdelta, wf)


def _conv_fwd_kernel(z_ref, a_ref, b_ref, c_ref, d_ref, wf_ref, y_ref, *, length, chunk):
    z = z_ref[0]

    def body(s, carry):
        f0 = pl.multiple_of(s * chunk, chunk)
        vr = _dot(wf_ref[pl.ds(f0, chunk), :], z)
        vi = _dot(wf_ref[pl.ds(length + f0, chunk), :], z)
        sl = pl.ds(f0, chunk)
        y_ref[0, sl, :] = (vr * a_ref[sl, :] + vi * b_ref[sl, :]).astype(BF16)
        y_ref[0, pl.ds(length + f0, chunk), :] = (vr * c_ref[sl, :] + vi * d_ref[sl, :]).astype(BF16)
        return carry

    lax.fori_loop(0, length // chunk, body, 0, unroll=True)


def _conv_inv_kernel(y_ref, wi_ref, z_ref, x0_ref, db_ref, o_ref, *, length, chunk):
    yf = y_ref[0]

    def body(s, carry):
        sl = pl.ds(pl.multiple_of(s * chunk, chunk), chunk)
        y = _dot(wi_ref[sl, :], yf)
        o_ref[0, sl, :] = (x0_ref[0, sl, :].astype(F32)
                           * (y + z_ref[0, sl, :].astype(F32) * db_ref[...])).astype(BF16)
        return carry

    lax.fori_loop(0, length // chunk, body, 0, unroll=True)


def _long_conv(z, x0, planes, d_bias, wf, wi):
    b, length, width = z.shape
    tc = CONV_COLS
    chunk = min(CONV_CHUNK, length)
    single = pl.Buffered(1)
    plane = pl.BlockSpec((length, tc), lambda j, i: (0, j), pipeline_mode=single)
    est_f = 2 * length * length * 2 + 4 * length * tc * 4 + 2 * length * tc * 2 + 4 * length * tc * 2 + 8 * chunk * tc * 4
    spec = pl.pallas_call(
        functools.partial(_conv_fwd_kernel, length=length, chunk=chunk),
        grid=(width // tc, b),
        in_specs=[pl.BlockSpec((1, length, tc), lambda j, i: (i, 0, j))] + [plane] * 4
        + [pl.BlockSpec((2 * length, length), lambda j, i: (0, 0), pipeline_mode=single)],
        out_specs=pl.BlockSpec((1, 2 * length, tc), lambda j, i: (i, 0, j)),
        out_shape=jax.ShapeDtypeStruct((b, 2 * length, width), BF16),
        compiler_params=_cparams(2, est_f),
        name="hyena_conv_fwd",
    )(z, *planes, wf)
    col = lambda rows: pl.BlockSpec((1, rows, tc), lambda i, j: (i, 0, j))
    est_i = 2 * length * length * 2 + 4 * length * tc * 2 + 6 * length * tc * 2 + 8 * chunk * tc * 4
    return pl.pallas_call(
        functools.partial(_conv_inv_kernel, length=length, chunk=chunk),
        grid=(b, width // tc),
        in_specs=[col(2 * length),
                  pl.BlockSpec((length, 2 * length), lambda i, j: (0, 0), pipeline_mode=single),
                  col(length), col(length), pl.BlockSpec((1, tc), lambda i, j: (0, j))],
        out_specs=col(length),
        out_shape=jax.ShapeDtypeStruct((b, length, width), BF16),
        compiler_params=_cparams(2, est_i),
        name="hyena_conv_inv",
    )(spec, wi, z, x0, d_bias.reshape(1, width))


def _hyena_mixer(x, shift, scale, g, hy, wf, wi):
    w_in, b_in, conv_w, conv_b, w1, b1, w2, b2, w3, b3, sin_freq, d_bias = hy
    x0, z = _hyena_in(x, shift, scale, g, w_in, b_in, conv_w, conv_b)
    planes = _hyena_filter_spectrum(x.shape[1], w1, b1, w2, b2, w3, b3, sin_freq, wf)
    return _long_conv(z, x0, planes, d_bias, wf, wi)


def kernel(x, c, ctx, c_ctx, w_mod, b_mod, norm1_g, norm2_g, attn_w_qkv, attn_q_gain, attn_k_gain, attn_w_o, hy_w_in, hy_b_in, hy_conv_w, hy_conv_b, hy_filt_w1, hy_filt_b1, hy_filt_w2, hy_filt_b2, hy_filt_w3, hy_filt_b3, hy_sin_freq, hy_d_bias, hy_w_out, hy_b_out, moe_router_w, moe_router_b, moe_w_gu, moe_b_gu, moe_w_down, moe_b_down):
    b, length, d = x.shape
    n_ctx = ctx.shape[1]
    depth = w_mod.shape[0]
    n_q_heads = d // HEAD_DIM

    r_pad = -(-(b + 1) // SUBLANES) * SUBLANES
    cc = jnp.zeros((r_pad, d), F32).at[:b].set(c).at[b].set(c_ctx)
    mod_all = _modulation(cc, w_mod, b_mod).reshape(depth, r_pad, 6, d)

    rope_tabs = _rope_tables(length, n_q_heads + N_KV_HEADS)
    n_exp, _, f2 = moe_w_gu.shape[1:]
    b_gu = moe_b_gu.reshape(depth * n_exp, 1, f2)
    expert_w = (moe_w_gu.reshape(depth * n_exp, d, f2), moe_w_down.reshape(depth * n_exp, f2 // 2, d),
                b_gu[..., 0::2], b_gu[..., 1::2], moe_b_down.reshape(depth * n_exp, 1, d))
    dft = {}

    def dft_pair(n):
        if n not in dft:
            wf = _dft_matrix(n)
            dft[n] = (wf, wf.T)
        return dft[n]

    for i in range(depth):
        j = i // N_MIXERS
        is_attn = i % N_MIXERS == 0
        ctx_live = any(m % N_MIXERS == 0 for m in range(i + 1, depth))
        mod = [mod_all[i, :b, s].reshape(b, 1, d) for s in range(6)]
        mod_c = [mod_all[i, b, s].reshape(1, 1, d) for s in range(6)]
        if is_attn:
            w_qkv = attn_w_qkv[j].astype(BF16)
            n_q = n_q_heads * HEAD_DIM
            q, k, v = _qkv_project(x, mod[0], mod[1], norm1_g[i], w_qkv, attn_q_gain[j], attn_k_gain[j],
                                   rope_tabs, True)
            w_c = w_qkv if ctx_live else w_qkv[:, n_q:]
            ctx_out = _qkv_project(ctx, mod_c[0], mod_c[1], norm1_g[i], w_c, attn_q_gain[j], attn_k_gain[j],
                                   None, ctx_live)
            k_c, v_c = ctx_out[-2], ctx_out[-1]
            a = _attention(q, [(k, v), (k_c, v_c)])
            a_c = _attention(ctx_out[0], [(k_c, v_c)]) if ctx_live else None
            w_out, b_out = attn_w_o[j].astype(BF16), jnp.zeros((d,), F32)
        else:
            hy = (hy_w_in[j].astype(BF16), hy_b_in[j], hy_conv_w[j], hy_conv_b[j], hy_filt_w1[j], hy_filt_b1[j],
                  hy_filt_w2[j], hy_filt_b2[j], hy_filt_w3[j], hy_filt_b3[j], hy_sin_freq[j], hy_d_bias[j])
            a = _hyena_mixer(x, mod[0], mod[1], norm1_g[i], hy, *dft_pair(length))
            a_c = _hyena_mixer(ctx, mod_c[0], mod_c[1], norm1_g[i], hy, *dft_pair(n_ctx)) if ctx_live else None
            w_out, b_out = hy_w_out[j].astype(BF16), hy_b_out[j]

        post = _post_mixer(a, w_out, b_out, x, mod[2], mod[3], mod[4], norm2_g[i],
                           moe_router_w[i], moe_router_b[i], jnp.zeros((1, LANES), F32))
        streams = [tuple(post[:5]) + (mod[5],)]
        counts = post[5]
        if ctx_live:
            post = _post_mixer(a_c, w_out, b_out, ctx, mod_c[2], mod_c[3], mod_c[4], norm2_g[i],
                               moe_router_w[i], moe_router_b[i], counts)
            streams.append(tuple(post[:5]) + (mod_c[5],))
            counts = post[5]
        outs = _moe(streams, counts, i, expert_w)
        x = outs[0]
        if ctx_live:
            ctx = outs[1]
    return x
```
